```python
import math
import jax, jax.numpy as jnp
from jax import lax
import numpy as np

D_MODEL = 2048
BATCH = 2
SEQ = 4096
DEPTH = 1

ATTN_HEADS = 8
ATTN_HEAD_DIM = 128
ATTN_WIDTH = ATTN_HEADS * ATTN_HEAD_DIM
MOBA_BLOCK = 256
MOBA_TOPK = 3
MOBA_Q_CHUNK = 32

RWKV_HEAD_DIM = 64
RWKV_WIDTH = D_MODEL - ATTN_WIDTH
RWKV_HEADS = RWKV_WIDTH // RWKV_HEAD_DIM
DECAY_LORA = 96
AAA_LORA = 96
GATE_LORA = 256
RWKV_GN_EPS = RWKV_HEAD_DIM * 1e-5

PEER_HEADS = 8
PEER_NKEYS = 128
PEER_EXPERTS = PEER_NKEYS * PEER_NKEYS
PEER_DKEY = 256
PEER_HALF = PEER_DKEY // 2
PEER_TOPK = 16
PEER_TOK_CHUNK = 128

RMS_EPS = 1e-6
NEG = -1e30

RWKV_COLS = 3 * RWKV_WIDTH + DECAY_LORA + AAA_LORA + GATE_LORA
IN_COLS = 3 * ATTN_WIDTH + RWKV_COLS

kernel_name = "hymba_moba_rwkv7_peer_block"


def rmsnorm(x, g):
    xf = x.astype(jnp.float32)
    y = xf * lax.rsqrt(jnp.mean(xf * xf, axis=-1, keepdims=True) + RMS_EPS)
    return (y * g.astype(jnp.float32)).astype(x.dtype)


def alibi_slopes(n):
    return 2.0 ** (-8.0 * (jnp.arange(n, dtype=jnp.float32) + 1.0) / n)


def moba_attention(q, k, v):
    B, S, _ = q.shape
    H, hd, L = ATTN_HEADS, ATTN_HEAD_DIM, MOBA_BLOCK
    nb = -(-S // L)
    s_pad = nb * L
    to_heads = lambda t: t.reshape(B, S, H, hd).transpose(0, 2, 1, 3)
    q, k, v = to_heads(q), to_heads(k), to_heads(v)
    pad = ((0, 0), (0, 0), (0, s_pad - S), (0, 0))
    k = jnp.pad(k, pad)
    v = jnp.pad(v, pad)
    kb = k.reshape(B, H, nb, L, hd)
    vb = v.reshape(B, H, nb, L, hd)
    kmean = jnp.mean(kb.astype(jnp.float32), axis=3)
    gate = jnp.einsum('bhsd,bhnd->bhsn', q.astype(jnp.float32), kmean)
    qblk = jnp.arange(S, dtype=jnp.int32) // L
    past = jnp.arange(nb, dtype=jnp.int32)[None, :] < qblk[:, None]
    gate = jnp.where(past[None, None], gate, NEG)
    n_sel = min(MOBA_TOPK, nb)
    _, sel = lax.top_k(gate, n_sel)
    sel = sel.astype(jnp.int32)
    sel_valid = sel < qblk[None, None, :, None]
    scale = hd ** -0.5
    slopes = alibi_slopes(H)
    offs = jnp.arange(L, dtype=jnp.int32)
    gather = jax.vmap(jax.vmap(lambda blocks, idx: blocks[idx]))
    Q = MOBA_Q_CHUNK

    def chunk(c0):
        qc = lax.dynamic_slice_in_dim(q, c0, Q, axis=2)
        ic = lax.dynamic_slice_in_dim(sel, c0, Q, axis=2)
        mc = lax.dynamic_slice_in_dim(sel_valid, c0, Q, axis=2)
        qpos = c0 + jnp.arange(Q, dtype=jnp.int32)
        ksel = gather(kb, ic)
        vsel = gather(vb, ic)
        s_sel = jnp.einsum('bhqd,bhqnld->bhqnl', qc, ksel).astype(jnp.float32) * scale
        kpos_sel = ic[..., None] * L + offs
        dist_sel = (qpos[None, None, :, None, None] - kpos_sel).astype(jnp.float32)
        s_sel = s_sel - slopes[None, :, None, None, None] * dist_sel
        s_sel = jnp.where(mc[..., None], s_sel, NEG).reshape(B, H, Q, n_sel * L)
        own0 = (c0 // L) * L
        kown = lax.dynamic_slice_in_dim(k, own0, L, axis=2)
        vown = lax.dynamic_slice_in_dim(v, own0, L, axis=2)
        s_own = jnp.einsum('bhqd,bhld->bhql', qc, kown).astype(jnp.float32) * scale
        dist_own = qpos[:, None] - (own0 + offs)[None, :]
        s_own = jnp.where(dist_own[None, None] >= 0,
                          s_own - slopes[None, :, None, None] * dist_own.astype(jnp.float32)[None, None],
                          NEG)
        p = jax.nn.softmax(jnp.concatenate([s_sel, s_own], axis=-1), axis=-1)
        p_sel = p[..., :n_sel * L].reshape(B, H, Q, n_sel, L).astype(v.dtype)
        p_own = p[..., n_sel * L:].astype(v.dtype)
        return (jnp.einsum('bhqnl,bhqnld->bhqd', p_sel, vsel)
                + jnp.einsum('bhql,bhld->bhqd', p_own, vown))

    starts = jnp.arange(0, S, Q, dtype=jnp.int32)
    out = lax.map(chunk, starts)
    return out.transpose(1, 0, 3, 2, 4).reshape(B, S, H * hd)


def token_shift_lerp(p, mu):
    prev = jnp.pad(p, ((0, 0), (1, 0), (0, 0)))[:, :-1]
    return p + (prev - p) * mu


def rwkv7_time_mix(p, w0, w2, a0, a2, g2, k_k, k_a, r_k, ln_g, ln_b):
    B, S, _ = p.shape
    H, N, W = RWKV_HEADS, RWKV_HEAD_DIM, RWKV_WIDTH
    f32 = jnp.float32
    r, k, v = p[..., :W], p[..., W:2 * W], p[..., 2 * W:3 * W]
    o = 3 * W
    xw = p[..., o:o + DECAY_LORA]
    o += DECAY_LORA
    xa = p[..., o:o + AAA_LORA]
    o += AAA_LORA
    xg = p[..., o:o + GATE_LORA]
    w_log = -jax.nn.softplus(-(w0 + jnp.tanh(xw) @ w2)) - 0.5
    decay = jnp.exp(-jnp.exp(w_log.astype(f32)))
    a = jax.nn.sigmoid(a0 + xa @ a2)
    g = jax.nn.sigmoid(xg) @ g2
    heads = lambda t: t.reshape(B, S, H, N)
    kk = heads(k * k_k).astype(f32)
    kk = kk / jnp.maximum(jnp.sqrt(jnp.sum(kk * kk, axis=-1, keepdims=True)), 1e-12)
    k = k * (1.0 + (a - 1.0) * k_a)
    r_h, k_h, v_h = heads(r), heads(k), heads(v)
    a_h = heads(a).astype(f32)
    b = kk * a_h
    xs = tuple(t.astype(f32).transpose(1, 0, 2, 3)
               for t in (r_h, heads(decay), k_h, v_h, -kk, b))

    def step(state, inp):
        rt, wt, kt, vt, at, bt = inp
        sa = jnp.einsum('bhij,bhj->bhi', state, at)
        state = (state * wt[:, :, None, :] + sa[..., None] * bt[:, :, None, :]
                 + vt[..., :, None] * kt[..., None, :])
        y = jnp.einsum('bhij,bhj->bhi', state, rt)
        return state, y

    s0 = jnp.zeros((B, H, N, N), f32)
    _, y = lax.scan(step, s0, xs)
    y = y.transpose(1, 0, 2, 3)
    mu = jnp.mean(y, axis=-1, keepdims=True)
    var = jnp.mean(jnp.square(y - mu), axis=-1, keepdims=True)
    y = ((y - mu) * lax.rsqrt(var + RWKV_GN_EPS)).reshape(B, S, W)
    y = y * ln_g.astype(f32) + ln_b.astype(f32)
    bonus = (jnp.sum(r_h * k_h * r_k, axis=-1, keepdims=True) * v_h).reshape(B, S, W)
    return ((y + bonus.astype(f32)) * g.astype(f32)).astype(p.dtype)


def peer_ffn(x, wq, sub_keys, u, v):
    B, S, D = x.shape
    T = B * S
    H, K, C = PEER_HEADS, PEER_TOPK, PEER_TOK_CHUNK
    xt = x.reshape(T, D)
    q = (xt @ wq).reshape(T, H, 2, PEER_HALF)
    s = jnp.einsum('thpc,hpnc->thpn', q, sub_keys).astype(jnp.float32)
    sv, si = lax.top_k(s, K)
    cand = (sv[:, :, 0, :, None] + sv[:, :, 1, None, :]).reshape(T, H, K * K)
    best, bi = lax.top_k(cand, K)
    i1 = jnp.take_along_axis(si[:, :, 0], bi // K, axis=-1)
    i2 = jnp.take_along_axis(si[:, :, 1], bi % K, axis=-1)
    expert = (i1 * PEER_NKEYS + i2).astype(jnp.int32)
    gates = jax.nn.softmax(best, axis=-1)
    nc = T // C

    def chunk(args):
        xc, ec, gc = args
        hpre = jnp.einsum('cd,chkd->chk', xc, u[ec]).astype(jnp.float32)
        act = (jax.nn.gelu(hpre, approximate=False) * gc).astype(x.dtype)
        return jnp.einsum('chk,chkd->cd', act, v[ec])

    y = lax.map(chunk, (xt.reshape(nc, C, D), expert.reshape(nc, C, H, K),
                        gates.reshape(nc, C, H, K)))
    return y.reshape(B, S, D)


def setup_inputs(seed: int = 0) -> dict:
    key = jax.random.key(seed)
    ks = jax.random.split(key, 22)
    f32 = jnp.float32
    nrm = lambda kk, shape, sc: jax.random.normal(kk, shape, f32) * sc
    L = DEPTH
    return {
        'x': nrm(ks[0], (BATCH, SEQ, D_MODEL), 1.0),
        'norm1_g': 1.0 + nrm(ks[1], (L, D_MODEL), 0.02),
        'w_in': nrm(ks[2], (L, D_MODEL, IN_COLS), D_MODEL ** -0.5),
        'rwkv_mu': jax.random.uniform(ks[3], (L, RWKV_COLS), f32),
        'rwkv_w0': jnp.tile(jnp.linspace(-6.0, -1.0, RWKV_HEAD_DIM, dtype=f32), RWKV_HEADS)[None]
                   + nrm(ks[4], (L, RWKV_WIDTH), 0.1),
        'rwkv_w2': nrm(ks[5], (L, DECAY_LORA, RWKV_WIDTH), 0.1 * DECAY_LORA ** -0.5),
        'rwkv_a0': nrm(ks[6], (L, RWKV_WIDTH), 0.5),
        'rwkv_a2': nrm(ks[7], (L, AAA_LORA, RWKV_WIDTH), AAA_LORA ** -0.5),
        'rwkv_g2': nrm(ks[8], (L, GATE_LORA, RWKV_WIDTH), GATE_LORA ** -0.5),
        'rwkv_k_k': 0.85 + nrm(ks[9], (L, RWKV_WIDTH), 0.02),
        'rwkv_k_a': 1.0 + nrm(ks[10], (L, RWKV_WIDTH), 0.02),
        'rwkv_r_k': nrm(ks[11], (L, RWKV_HEADS, RWKV_HEAD_DIM), 0.1),
        'rwkv_ln_g': 1.0 + nrm(ks[12], (L, RWKV_WIDTH), 0.02),
        'rwkv_ln_b': nrm(ks[13], (L, RWKV_WIDTH), 0.02),
        'w_out': nrm(ks[14], (L, ATTN_WIDTH + RWKV_WIDTH, D_MODEL), D_MODEL ** -0.5),
        'norm2_g': 1.0 + nrm(ks[15], (L, D_MODEL), 0.02),
        'peer_wq': nrm(ks[16], (L, D_MODEL, PEER_HEADS * PEER_DKEY), D_MODEL ** -0.5),
        'peer_sub_keys': nrm(ks[17], (L, PEER_HEADS, 2, PEER_NKEYS, PEER_HALF), PEER_HALF ** -0.5),
        'peer_u': nrm(ks[18], (L, PEER_EXPERTS, D_MODEL), D_MODEL ** -0.5),
        'peer_v': nrm(ks[19], (L, PEER_EXPERTS, D_MODEL), PEER_HEADS ** -0.5),
        'final_g': 1.0 + nrm(ks[20], (D_MODEL,), 0.02),
    }


def reference(x, norm1_g, w_in, rwkv_mu, rwkv_w0, rwkv_w2, rwkv_a0, rwkv_a2, rwkv_g2,
              rwkv_k_k, rwkv_k_a, rwkv_r_k, rwkv_ln_g, rwkv_ln_b, w_out, norm2_g,
              peer_wq, peer_sub_keys, peer_u, peer_v, final_g):
    A = ATTN_WIDTH
    h = x
    for l in range(DEPTH):
        xn = rmsnorm(h, norm1_g[l])
        proj = xn @ w_in[l]
        attn = moba_attention(proj[..., :A], proj[..., A:2 * A], proj[..., 2 * A:3 * A])
        rw_in = token_shift_lerp(proj[..., 3 * A:], rwkv_mu[l])
        rw = rwkv7_time_mix(rw_in, rwkv_w0[l], rwkv_w2[l], rwkv_a0[l], rwkv_a2[l], rwkv_g2[l],
                            rwkv_k_k[l], rwkv_k_a[l], rwkv_r_k[l], rwkv_ln_g[l], rwkv_ln_b[l])
        h = h + jnp.concatenate([attn, rw], axis=-1) @ w_out[l]
        h = h + peer_ffn(rmsnorm(h, norm2_g[l]), peer_wq[l], peer_sub_keys[l], peer_u[l], peer_v[l])
    return rmsnorm(h, final_g)
```

```python
import math

import jax
import jax.numpy as jnp
from jax import lax
from jax.experimental import pallas as pl
from jax.experimental.pallas import tpu as pltpu

F32 = jnp.float32
BF16 = jnp.bfloat16

D_MODEL = 2048
ATTN_HEADS = 8
ATTN_HEAD_DIM = 128
ATTN_WIDTH = ATTN_HEADS * ATTN_HEAD_DIM
MOBA_BLOCK = 256
MOBA_TOPK = 3

RWKV_HEAD_DIM = 64
RWKV_WIDTH = D_MODEL - ATTN_WIDTH
RWKV_HEADS = RWKV_WIDTH // RWKV_HEAD_DIM
DECAY_LORA = 96
AAA_LORA = 96
GATE_LORA = 256
LORA_COLS = DECAY_LORA + AAA_LORA + GATE_LORA
LORA_PAD = 512
RWKV_GN_EPS = RWKV_HEAD_DIM * 1e-5
IN_COLS = 3 * ATTN_WIDTH + 3 * RWKV_WIDTH + LORA_COLS
IN_COLS_PAD = 3 * ATTN_WIDTH + 3 * RWKV_WIDTH + LORA_PAD

PEER_HEADS = 8
PEER_NKEYS = 128
PEER_HALF = 128
PEER_TOPK = 16

RMS_EPS = 1e-6
NEG = -1e30

LANES = 128
SCAN_CHUNK = 64
SCAN_INV_BLOCK = 16
VMEM_LIMIT = 56 * 1024 * 1024

_NT = (((1,), (1,)), ((), ()))
_TN = (((0,), (0,)), ((), ()))


def _params(sem):
    return pltpu.CompilerParams(dimension_semantics=sem, vmem_limit_bytes=VMEM_LIMIT)


def _dot(a, b):
    return jnp.dot(a.astype(BF16), b.astype(BF16), preferred_element_type=F32)


def _dot_nt(a, b):
    return lax.dot_general(a.astype(BF16), b.astype(BF16), _NT, preferred_element_type=F32)


def _dot_tn(a, b):
    return lax.dot_general(a.astype(BF16), b.astype(BF16), _TN, preferred_element_type=F32)


def _split3(x):
    hi = x.astype(BF16)
    r1 = x - hi.astype(F32)
    mid = r1.astype(BF16)
    lo = (r1 - mid.astype(F32)).astype(BF16)
    return hi, mid, lo


def _dot_exact_rhs(x, ones_bf16):
    hi, mid, lo = _split3(x)
    d = lambda p: jnp.dot(p, ones_bf16, preferred_element_type=F32)
    return d(hi) + d(mid) + d(lo)


def _dot_exact_lhs(ones_bf16, x):
    hi, mid, lo = _split3(x)
    d = lambda p: jnp.dot(ones_bf16, p, preferred_element_type=F32)
    return d(hi) + d(mid) + d(lo)


def _norm_matmul_kernel(x_ref, g_ref, w_ref, o_ref, xn_ref):
    @pl.when(pl.program_id(1) == 0)
    def _():
        x = x_ref[...]
        inv = lax.rsqrt(jnp.mean(x * x, axis=-1, keepdims=True) + RMS_EPS)
        xn_ref[...] = (x * inv * g_ref[...]).astype(BF16)

    o_ref[...] = jnp.dot(xn_ref[...], w_ref[...], preferred_element_type=F32)


def _norm_matmul(x, g, w, tm, tn):
    m, k = x.shape
    n = w.shape[1]
    return pl.pallas_call(
        _norm_matmul_kernel,
        grid=(m // tm, n // tn),
        in_specs=[pl.BlockSpec((tm, k), lambda i, j: (i, 0)),
                  pl.BlockSpec((1, k), lambda i, j: (0, 0)),
                  pl.BlockSpec((k, tn), lambda i, j: (0, j))],
        out_specs=pl.BlockSpec((tm, tn), lambda i, j: (i, j)),
        out_shape=jax.ShapeDtypeStruct((m, n), F32),
        scratch_shapes=[pltpu.VMEM((tm, k), BF16)],
        compiler_params=_params(("parallel", "arbitrary")),
        name="in_proj",
    )(x, g, w)


def _moba_kernel(q_ref, k_ref, v_ref, o_ref, kmean_ref, *, nb):
    blk_len = MOBA_BLOCK
    h = pl.program_id(1)
    qi = pl.program_id(2)

    @pl.when(qi == 0)
    def _():
        for j in range(nb):
            kmean_ref[j:j + 1, :] = jnp.mean(k_ref[0, j * blk_len:(j + 1) * blk_len, :],
                                             axis=0, keepdims=True)

    q = q_ref[0]
    gate = lax.dot_general(q, kmean_ref[...], _NT, precision=lax.Precision.HIGHEST,
                           preferred_element_type=F32)
    blk = lax.broadcasted_iota(jnp.int32, gate.shape, 1)
    past = blk < qi
    g = jnp.where(past, gate, NEG)
    selw = jnp.zeros(gate.shape, F32)
    for _ in range(min(MOBA_TOPK, nb)):
        m = jnp.max(g, axis=1, keepdims=True)
        first = jnp.min(jnp.where(g == m, blk, nb), axis=1, keepdims=True)
        pick = blk == first
        selw = jnp.where(pick, 1.0, selw)
        g = jnp.where(pick, -jnp.inf, g)
    selw = jnp.where(past, selw, 0.0)

    scale = ATTN_HEAD_DIM ** -0.5
    rc = (lax.broadcasted_iota(jnp.int32, (blk_len, blk_len), 0)
          - lax.broadcasted_iota(jnp.int32, (blk_len, blk_len), 1))
    rcf = rc.astype(F32)
    slope = jnp.exp2(jnp.zeros((1, 1), F32) - (h + 1).astype(F32))
    qb = q.astype(BF16)

    def scores(kb):
        return lax.dot_general(qb, kb.astype(BF16), _NT, preferred_element_type=F32) * scale

    row0 = pl.multiple_of(qi * blk_len, blk_len)
    s = scores(k_ref[0, pl.ds(row0, blk_len), :]) - slope * rcf
    s = jnp.where(rc >= 0, s, NEG)
    m = jnp.max(s, axis=1, keepdims=True)
    p = jnp.exp(s - m)
    l = jnp.sum(p, axis=1, keepdims=True)
    acc = _dot(p, v_ref[0, pl.ds(row0, blk_len), :])

    def body(j, carry):
        m, l, acc = carry
        r0 = pl.multiple_of(j * blk_len, blk_len)
        selj = jnp.max(jnp.where(blk == j, selw, 0.0), axis=1, keepdims=True)
        off = ((qi - j) * blk_len).astype(F32)
        s = scores(k_ref[0, pl.ds(r0, blk_len), :]) - slope * (rcf + off)
        s = jnp.where(selj > 0.0, s, NEG)
        m_new = jnp.maximum(m, jnp.max(s, axis=1, keepdims=True))
        alpha = jnp.exp(m - m_new)
        p = jnp.exp(s - m_new)
        l = alpha * l + jnp.sum(p, axis=1, keepdims=True)
        acc = alpha * acc + _dot(p, v_ref[0, pl.ds(r0, blk_len), :])
        return m_new, l, acc

    m, l, acc = lax.fori_loop(0, qi, body, (m, l, acc))
    o_ref[0] = acc / l


def _moba(proj, batch, seq):
    nb = seq // MOBA_BLOCK
    hd = ATTN_HEAD_DIM
    kern = lambda *refs: _moba_kernel(*refs, nb=nb)
    return pl.pallas_call(
        kern,
        grid=(batch, ATTN_HEADS, nb),
        in_specs=[pl.BlockSpec((1, MOBA_BLOCK, hd), lambda b, h, i: (b, i, h)),
                  pl.BlockSpec((1, seq, hd), lambda b, h, i: (b, 0, ATTN_HEADS + h)),
                  pl.BlockSpec((1, seq, hd), lambda b, h, i: (b, 0, 2 * ATTN_HEADS + h))],
        out_specs=pl.BlockSpec((1, MOBA_BLOCK, hd), lambda b, h, i: (b, i, h)),
        out_shape=jax.ShapeDtypeStruct((batch, seq, ATTN_WIDTH), F32),
        scratch_shapes=[pltpu.VMEM((nb, hd), F32)],
        compiler_params=_params(("parallel", "parallel", "arbitrary")),
        name="moba",
    )(proj, proj, proj)


def _rwkv_prep_kernel(r_ref, k_ref, v_ref, l_ref, rp_ref, kp_ref, vp_ref, lp_ref,
                      mu_r, mu_k, mu_v, mu_l, w0, w2p, a0, a2p, g2p, kk_w, ka_w, rk_w, bd_ref,
                      r_o, lw_o, k_o, v_o, kk_o, b_o, g_o, bonus_o):
    first = pl.program_id(1) == 0

    def shift(cur_ref, prev_ref, mu_ref):
        cur = cur_ref[0]
        prev_last = jnp.where(first, 0.0, prev_ref[0, 7:8, :])
        rows = lax.broadcasted_iota(jnp.int32, cur.shape, 0)
        prev = jnp.where(rows == 0, prev_last, pltpu.roll(cur, 1, axis=0))
        return cur + (prev - cur) * mu_ref[...]

    r = shift(r_ref, rp_ref, mu_r)
    k = shift(k_ref, kp_ref, mu_k)
    v = shift(v_ref, vp_ref, mu_v)
    lo = shift(l_ref, lp_ref, mu_l)

    bd = bd_ref[...]
    z = -(w0[...] + _dot(jnp.tanh(lo), w2p[...]))
    softplus = jnp.maximum(z, 0.0) + jnp.log1p(jnp.exp(-jnp.abs(z)))
    lw = -jnp.exp(-softplus - 0.5)
    a = jax.nn.sigmoid(a0[...] + _dot(lo, a2p[...]))
    g = _dot(jax.nn.sigmoid(lo), g2p[...])
    kk = k * kk_w[...]
    kk = kk / jnp.maximum(jnp.sqrt(_dot_exact_rhs(kk * kk, bd)), 1e-12)
    k2 = k * (1.0 + (a - 1.0) * ka_w[...])
    r_o[0] = r
    lw_o[0] = lw
    k_o[0] = k2
    v_o[0] = v
    kk_o[0] = kk
    b_o[0] = kk * a
    g_o[0] = g
    bonus_o[0] = _dot_exact_rhs(r * k2 * rk_w[...], bd) * v


def _rwkv_prep(proj, batch, seq, weights, tt):
    w = RWKV_WIDTH
    c0 = 3 * ATTN_WIDTH // w
    lc = (3 * ATTN_WIDTH + 3 * w) // LORA_PAD
    pb = tt // 8
    cur = lambda width, col: pl.BlockSpec((1, tt, width), lambda b, i: (b, i, col))
    prev = lambda width, col: pl.BlockSpec(
        (1, 8, width), lambda b, i: (b, jnp.maximum(i * pb - 1, 0), col))
    full = lambda a: pl.BlockSpec(a.shape, lambda b, i: (0,) * a.ndim)
    out = pl.BlockSpec((1, tt, w), lambda b, i: (b, i, 0))
    return pl.pallas_call(
        _rwkv_prep_kernel,
        grid=(batch, seq // tt),
        in_specs=[cur(w, c0), cur(w, c0 + 1), cur(w, c0 + 2), cur(LORA_PAD, lc),
                  prev(w, c0), prev(w, c0 + 1), prev(w, c0 + 2), prev(LORA_PAD, lc)]
                 + [full(a) for a in weights],
        out_specs=[out] * 8,
        out_shape=[jax.ShapeDtypeStruct((batch, seq, w), F32)] * 8,
        compiler_params=_params(("parallel", "arbitrary")),
        name="rwkv_prep",
    )(proj, proj, proj, proj, proj, proj, proj, proj, *weights)


def _scan_kernel(r_ref, lw_ref, k_ref, v_ref, kk_ref, b_ref, y_ref, s_ref, *, tb):
    c = SCAN_CHUNK
    c2 = 2 * c

    @pl.when(pl.program_id(2) == 0)
    def _():
        s_ref[...] = jnp.zeros_like(s_ref)

    head0 = lax.broadcasted_iota(jnp.int32, (c, LANES), 1) < RWKV_HEAD_DIM
    row = lax.broadcasted_iota(jnp.int32, (c2, c2), 0)
    col = lax.broadcasted_iota(jnp.int32, (c2, c2), 1)
    same_head = (row // c) == (col // c)
    strict = jnp.logical_and(same_head, col < row)
    incl = jnp.logical_and(same_head, col <= row)
    inv_blk = (row // SCAN_INV_BLOCK) == (col // SCAN_INV_BLOCK)
    eye = jnp.where(row == col, 1.0, 0.0).astype(F32)
    tri = jnp.where(lax.broadcasted_iota(jnp.int32, (c, c), 0)
                    >= lax.broadcasted_iota(jnp.int32, (c, c), 1), 1.0, 0.0).astype(BF16)

    def stack(x):
        return jnp.concatenate([jnp.where(head0, x, 0.0), jnp.where(head0, 0.0, x)], axis=0)

    state = s_ref[...]
    for ci in range(tb // c):
        sl = slice(ci * c, (ci + 1) * c)
        lw = lw_ref[0, sl, :]
        r = r_ref[0, sl, :]
        k = k_ref[0, sl, :]
        v = v_ref[0, sl, :]
        kk = kk_ref[0, sl, :]
        b = b_ref[0, sl, :]

        cum = _dot_exact_lhs(tri, lw)
        cum_end = cum[c - 1:c, :]
        e_neg = jnp.exp(-cum)
        e_end = jnp.exp(cum_end - cum)
        a_s = stack(-kk * jnp.exp(cum - lw))
        r_s = stack(r * jnp.exp(cum))
        b_s = stack(b * e_neg)
        k_s = stack(k * e_neg)
        bend_s = stack(b * e_end)
        kend_s = stack(k * e_end)
        v_s = stack(v)

        gram = _dot_nt(jnp.concatenate([a_s, r_s], axis=0),
                       jnp.concatenate([b_s, k_s], axis=0))
        l_ab = jnp.where(strict, gram[:c2, :c2], 0.0)
        l_ak = jnp.where(strict, gram[:c2, c2:], 0.0)
        l_rb = jnp.where(incl, gram[c2:, :c2], 0.0)
        l_rk = jnp.where(incl, gram[c2:, c2:], 0.0)

        x = jnp.where(inv_blk, l_ab, 0.0)
        off = jnp.where(inv_blk, 0.0, l_ab)
        dinv = eye + x
        for _ in range(int(math.log2(SCAN_INV_BLOCK)) - 1):
            x = _dot(x, x)
            dinv = dinv + _dot(dinv, x)
        f = _dot(dinv, off)
        gm = eye + f
        gm = gm + _dot(gm, _dot(f, f))
        t_inv = _dot(gm, dinv)

        w1 = _dot(l_ak, v_s)
        ta = _dot(t_inv, jnp.concatenate([a_s, w1], axis=1))
        a_hat = ta[:, :LANES]
        v_hat = ta[:, LANES:]
        y1 = _dot(l_rk, v_s)
        kv = _dot_tn(v_s, kend_s)

        m1 = _dot_nt(jnp.concatenate([a_hat, r_s], axis=0), state)
        u = m1[:c2] + v_hat
        y = m1[c2:] + y1 + _dot(l_rb, u)
        state = state * jnp.exp(cum_end) + kv + _dot_tn(u, bend_s)
        y_ref[0, sl, :] = y[:c] + y[c:]
    s_ref[...] = state


def _rwkv_scan(r, lw, k, v, kk, b, tb):
    batch, seq, w = r.shape
    spec = pl.BlockSpec((1, tb, LANES), lambda bi, p, t: (bi, t, p))
    kern = lambda *refs: _scan_kernel(*refs, tb=tb)
    return pl.pallas_call(
        kern,
        grid=(batch, w // LANES, seq // tb),
        in_specs=[spec] * 6,
        out_specs=spec,
        out_shape=jax.ShapeDtypeStruct((batch, seq, w), F32),
        scratch_shapes=[pltpu.VMEM((LANES, LANES), F32)],
        compiler_params=_params(("parallel", "parallel", "arbitrary")),
        name="rwkv_scan",
    )(r, lw, k, v, kk, b)


def _out_proj_kernel(x_ref, attn_ref, y_ref, bonus_ref, g_ref, lng, lnb, bd_ref, w_ref, n2g,
                     h_ref, hn_ref):
    bd = bd_ref[...]
    y = y_ref[...]
    inv_n = 1.0 / RWKV_HEAD_DIM
    d = y - _dot_exact_rhs(y, bd) * inv_n
    var = _dot_exact_rhs(d * d, bd) * inv_n
    yn = d * lax.rsqrt(var + RWKV_GN_EPS) * lng[...] + lnb[...]
    rw = (yn + bonus_ref[...]) * g_ref[...]
    h = (x_ref[...] + _dot(attn_ref[...], w_ref[:ATTN_WIDTH, :])
         + _dot(rw, w_ref[ATTN_WIDTH:, :]))
    h_ref[...] = h
    inv = lax.rsqrt(jnp.mean(h * h, axis=-1, keepdims=True) + RMS_EPS)
    hn_ref[...] = (h * inv * n2g[...]).astype(BF16)


def _out_proj(x, attn, y, bonus, g, lng, lnb, bd, w_out, n2g, tm):
    m = x.shape[0]
    row = lambda width: pl.BlockSpec((tm, width), lambda i: (i, 0))
    full = lambda a: pl.BlockSpec(a.shape, lambda i: (0,) * a.ndim)
    return pl.pallas_call(
        _out_proj_kernel,
        grid=(m // tm,),
        in_specs=[row(D_MODEL), row(ATTN_WIDTH), row(RWKV_WIDTH), row(RWKV_WIDTH), row(RWKV_WIDTH),
                  full(lng), full(lnb), full(bd), full(w_out), full(n2g)],
        out_specs=[row(D_MODEL), row(D_MODEL)],
        out_shape=[jax.ShapeDtypeStruct((m, D_MODEL), F32),
                   jax.ShapeDtypeStruct((m, D_MODEL), BF16)],
        compiler_params=_params(("parallel",)),
        name="out_proj",
    )(x, attn, y, bonus, g, lng, lnb, bd, w_out, n2g)


def _peer_q_kernel(hn_ref, wq_ref, keys_ref, st_ref):
    q = jnp.dot(hn_ref[...], wq_ref[...], preferred_element_type=F32)
    for hp in range(2 * PEER_HEADS):
        qs = q[:, hp * PEER_HALF:(hp + 1) * PEER_HALF]
        st_ref[hp] = _dot_nt(keys_ref[hp], qs)


def _peer_q(hn, wq, keys, tm):
    m = hn.shape[0]
    return pl.pallas_call(
        _peer_q_kernel,
        grid=(m // tm,),
        in_specs=[pl.BlockSpec((tm, D_MODEL), lambda i: (i, 0)),
                  pl.BlockSpec(wq.shape, lambda i: (0, 0)),
                  pl.BlockSpec(keys.shape, lambda i: (0, 0, 0))],
        out_specs=pl.BlockSpec((2 * PEER_HEADS, PEER_NKEYS, tm), lambda i: (0, 0, i)),
        out_shape=jax.ShapeDtypeStruct((2 * PEER_HEADS, PEER_NKEYS, m), F32),
        compiler_params=_params(("parallel",)),
        name="peer_q",
    )(hn, wq, keys)


def _top_desc(x, k):
    n = x.shape[0]
    rows = lax.broadcasted_iota(jnp.int32, x.shape, 0)
    outs = []
    for _ in range(k):
        m = jnp.max(x, axis=0, keepdims=True)
        outs.append(m)
        idx = jnp.min(jnp.where(x == m, rows, n), axis=0, keepdims=True)
        x = jnp.where(rows == idx, -jnp.inf, x)
    return jnp.concatenate(outs, axis=0)


def _peer_topk_kernel(s_ref, e1_ref, e2_ref, thr_ref):
    s1 = s_ref[0]
    s2 = s_ref[1]
    sv1 = _top_desc(s1, PEER_TOPK)
    sv2 = _top_desc(s2, PEER_TOPK)
    cand = jnp.concatenate([sv1[a:a + 1] + sv2 for a in range(PEER_TOPK)], axis=0)
    best = _top_desc(cand, PEER_TOPK)
    z = jnp.sum(jnp.exp(best - best[0:1]), axis=0, keepdims=True)
    e1_ref[0] = jnp.exp(s1 - sv1[0:1]) / z
    e2_ref[0] = jnp.exp(s2 - sv2[0:1])
    thr_ref[0] = best[PEER_TOPK - 1:PEER_TOPK]


def _peer_topk(st, tl):
    m = st.shape[2]
    e_spec = pl.BlockSpec((1, PEER_NKEYS, tl), lambda h, i: (h, 0, i))
    e_shape = jax.ShapeDtypeStruct((PEER_HEADS, PEER_NKEYS, m), F32)
    return pl.pallas_call(
        _peer_topk_kernel,
        grid=(PEER_HEADS, m // tl),
        in_specs=[pl.BlockSpec((2, PEER_NKEYS, tl), lambda h, i: (h, 0, i))],
        out_specs=[e_spec, e_spec, pl.BlockSpec((1, 1, tl), lambda h, i: (h, 0, i))],
        out_shape=[e_shape, e_shape, jax.ShapeDtypeStruct((PEER_HEADS, 1, m), F32)],
        compiler_params=_params(("parallel", "parallel")),
        name="peer_topk",
    )(st)


def _peer_dense_kernel(u_ref, hnt_ref, vt_ref, s_ref, e1_ref, e2_ref, thr_ref, yt_ref, *, n1):
    e = pl.program_id(1)

    @pl.when(e == 0)
    def _():
        yt_ref[...] = jnp.zeros_like(yt_ref)

    hpre = jnp.dot(u_ref[...], hnt_ref[...], preferred_element_type=F32)
    acts = []
    for a in range(n1):
        c = e * n1 + a
        gates = jnp.zeros((PEER_NKEYS, hpre.shape[1]), F32)
        for h in range(PEER_HEADS):
            s1row = s_ref[2 * h, pl.ds(c, 1), :]
            e1row = e1_ref[h, pl.ds(c, 1), :]
            hit = (s1row + s_ref[2 * h + 1]) >= thr_ref[h]
            gates = gates + jnp.where(hit, e2_ref[h], 0.0) * e1row
        x = hpre[a * PEER_NKEYS:(a + 1) * PEER_NKEYS]
        gelu = 0.5 * x * (1.0 + lax.erf(x * (0.5 ** 0.5)))
        acts.append((gelu * gates).astype(BF16))
    act = jnp.concatenate(acts, axis=0)
    yt_ref[...] += jnp.dot(vt_ref[...], act, preferred_element_type=F32)


def _peer_dense(u, hnt, vt, st, e1, e2, thr, tt, n1):
    m = hnt.shape[1]
    et = n1 * PEER_NKEYS
    tok3 = lambda a: pl.BlockSpec((a.shape[0], a.shape[1], tt), lambda i, e: (0, 0, i))
    kern = lambda *refs: _peer_dense_kernel(*refs, n1=n1)
    return pl.pallas_call(
        kern,
        grid=(m // tt, PEER_NKEYS // n1),
        in_specs=[pl.BlockSpec((et, D_MODEL), lambda i, e: (e, 0)),
                  pl.BlockSpec((D_MODEL, tt), lambda i, e: (0, i)),
                  pl.BlockSpec((D_MODEL, et), lambda i, e: (0, e)),
                  tok3(st), tok3(e1), tok3(e2), tok3(thr)],
        out_specs=pl.BlockSpec((D_MODEL, tt), lambda i, e: (0, i)),
        out_shape=jax.ShapeDtypeStruct((D_MODEL, m), F32),
        compiler_params=_params(("parallel", "arbitrary")),
        name="peer_dense",
    )(u, hnt, vt, st, e1, e2, thr)


def _final_kernel(h_ref, y_ref, g_ref, o_ref):
    h = h_ref[...] + y_ref[...]
    inv = lax.rsqrt(jnp.mean(h * h, axis=-1, keepdims=True) + RMS_EPS)
    o_ref[...] = h * inv * g_ref[...]


def _final(h, y, g, tm):
    m = h.shape[0]
    row = pl.BlockSpec((tm, D_MODEL), lambda i: (i, 0))
    return pl.pallas_call(
        _final_kernel,
        grid=(m // tm,),
        in_specs=[row, row, pl.BlockSpec((1, D_MODEL), lambda i: (0, 0))],
        out_specs=row,
        out_shape=jax.ShapeDtypeStruct((m, D_MODEL), F32),
        compiler_params=_params(("parallel",)),
        name="final_norm",
    )(h, y, g)


def _row(a):
    return a.reshape(1, -1).astype(F32)


def _pad_rows(w, start, total):
    return jnp.zeros((total, w.shape[1]), BF16).at[start:start + w.shape[0]].set(w.astype(BF16))


def kernel(x, norm1_g, w_in, rwkv_mu, rwkv_w0, rwkv_w2, rwkv_a0, rwkv_a2, rwkv_g2, rwkv_k_k, rwkv_k_a, rwkv_r_k, rwkv_ln_g, rwkv_ln_b, w_out, norm2_g, peer_wq, peer_sub_keys, peer_u, peer_v, final_g):
    batch, seq, d = x.shape
    tokens = batch * seq
    w = RWKV_WIDTH
    xt = x.reshape(tokens, d)

    w_in_p = jnp.pad(w_in[0].astype(BF16), ((0, 0), (0, IN_COLS_PAD - IN_COLS)))
    proj = _norm_matmul(xt, _row(norm1_g[0]), w_in_p, tm=1024, tn=512)
    proj = proj.reshape(batch, seq, IN_COLS_PAD)

    attn = _moba(proj, batch, seq)

    mu = rwkv_mu[0]
    mu_l = jnp.pad(mu[3 * w:], (0, LORA_PAD - LORA_COLS))
    hid = lax.broadcasted_iota(jnp.int32, (w, w), 0) // RWKV_HEAD_DIM
    bd = (hid == hid.T).astype(BF16)
    prep_w = [_row(mu[:w]), _row(mu[w:2 * w]), _row(mu[2 * w:3 * w]), _row(mu_l),
              _row(rwkv_w0[0]), _pad_rows(rwkv_w2[0], 0, LORA_PAD),
              _row(rwkv_a0[0]), _pad_rows(rwkv_a2[0], DECAY_LORA, LORA_PAD),
              _pad_rows(rwkv_g2[0], DECAY_LORA + AAA_LORA, LORA_PAD),
              _row(rwkv_k_k[0]), _row(rwkv_k_a[0]), _row(rwkv_r_k[0]), bd]
    r, lw, k2, v, kk, b, g, bonus = _rwkv_prep(proj, batch, seq, prep_w, tt=256)
    y = _rwkv_scan(r, lw, k2, v, kk, b, tb=256)

    flat = lambda a: a.reshape(tokens, a.shape[-1])
    h, hn = _out_proj(xt, flat(attn), flat(y), flat(bonus), flat(g), _row(rwkv_ln_g[0]),
                      _row(rwkv_ln_b[0]), bd, w_out[0].astype(BF16), _row(norm2_g[0]), tm=256)

    keys = peer_sub_keys[0].reshape(2 * PEER_HEADS, PEER_NKEYS, PEER_HALF).astype(BF16)
    st = _peer_q(hn, peer_wq[0].astype(BF16), keys, tm=512)
    e1, e2, thr = _peer_topk(st, tl=512)
    yt = _peer_dense(peer_u[0].astype(BF16), hn.T, peer_v[0].T.astype(BF16), st, e1, e2, thr,
                     tt=512, n1=2)
    out = _final(h, yt.T, final_g.reshape(1, d), tm=512)
    return out.reshape(batch, seq, d)
```

```python
import itertools
import math

import jax
import jax.numpy as jnp
from jax import lax
from jax.experimental import pallas as pl
from jax.experimental.pallas import tpu as pltpu

F32 = jnp.float32
BF16 = jnp.bfloat16

D_MODEL = 2048
ATTN_HEADS = 8
ATTN_HEAD_DIM = 128
ATTN_WIDTH = ATTN_HEADS * ATTN_HEAD_DIM
MOBA_BLOCK = 256
MOBA_TOPK = 3

RWKV_HEAD_DIM = 64
RWKV_WIDTH = D_MODEL - ATTN_WIDTH
RWKV_HEADS = RWKV_WIDTH // RWKV_HEAD_DIM
DECAY_LORA = 96
AAA_LORA = 96
GATE_LORA = 256
LORA_COLS = DECAY_LORA + AAA_LORA + GATE_LORA
LORA_PAD = 512
RWKV_GN_EPS = RWKV_HEAD_DIM * 1e-5
IN_COLS = 3 * ATTN_WIDTH + 3 * RWKV_WIDTH + LORA_COLS
IN_COLS_PAD = 3 * ATTN_WIDTH + 3 * RWKV_WIDTH + LORA_PAD

PEER_HEADS = 8
PEER_NKEYS = 128
PEER_HALF = 128
PEER_TOPK = 16

RMS_EPS = 1e-6
NEG = -1e30

LANES = 128
SCAN_CHUNK = 64
SCAN_INV_BLOCK = 16
DENSE_ROWS = 64
VMEM_LIMIT = 56 * 1024 * 1024

_NT = (((1,), (1,)), ((), ()))
_TN = (((0,), (0,)), ((), ()))


def _params(sem, flags=None):
    return pltpu.CompilerParams(dimension_semantics=sem, vmem_limit_bytes=VMEM_LIMIT, flags=flags)


def _dot(a, b):
    return jnp.dot(a.astype(BF16), b.astype(BF16), preferred_element_type=F32)


def _dot_nt(a, b):
    return lax.dot_general(a.astype(BF16), b.astype(BF16), _NT, preferred_element_type=F32)


def _dot_tn(a, b):
    return lax.dot_general(a.astype(BF16), b.astype(BF16), _TN, preferred_element_type=F32)


def _split3(x):
    hi = x.astype(BF16)
    r1 = x - hi.astype(F32)
    mid = r1.astype(BF16)
    lo = (r1 - mid.astype(F32)).astype(BF16)
    return hi, mid, lo


def _dot_exact_rhs(x, ones_bf16):
    hi, mid, lo = _split3(x)
    d = lambda p: jnp.dot(p, ones_bf16, preferred_element_type=F32)
    return d(hi) + d(mid) + d(lo)


def _dot_exact_lhs(ones_bf16, x):
    hi, mid, lo = _split3(x)
    d = lambda p: jnp.dot(ones_bf16, p, preferred_element_type=F32)
    return d(hi) + d(mid) + d(lo)


def _norm_matmul_kernel(x_ref, g_ref, w_ref, o_ref, xn_ref):
    @pl.when(pl.program_id(1) == 0)
    def _():
        x = x_ref[...]
        inv = lax.rsqrt(jnp.mean(x * x, axis=-1, keepdims=True) + RMS_EPS)
        xn_ref[...] = (x * inv * g_ref[...]).astype(BF16)

    o_ref[...] = jnp.dot(xn_ref[...], w_ref[...], preferred_element_type=F32)


def _norm_matmul(x, g, w, tm, tn):
    m, k = x.shape
    n = w.shape[1]
    return pl.pallas_call(
        _norm_matmul_kernel,
        grid=(m // tm, n // tn),
        in_specs=[pl.BlockSpec((tm, k), lambda i, j: (i, 0)),
                  pl.BlockSpec((1, k), lambda i, j: (0, 0)),
                  pl.BlockSpec((k, tn), lambda i, j: (0, j))],
        out_specs=pl.BlockSpec((tm, tn), lambda i, j: (i, j)),
        out_shape=jax.ShapeDtypeStruct((m, n), F32),
        scratch_shapes=[pltpu.VMEM((tm, k), BF16)],
        compiler_params=_params(("parallel", "arbitrary")),
        name="in_proj",
    )(x, g, w)


def _moba_kernel(q_ref, k_ref, v_ref, o_ref, kmean_ref, *, nb):
    blk_len = MOBA_BLOCK
    h = pl.program_id(1)
    qi = pl.program_id(2)

    @pl.when(qi == 0)
    def _():
        for j in range(nb):
            kmean_ref[j:j + 1, :] = jnp.mean(k_ref[0, j * blk_len:(j + 1) * blk_len, :],
                                             axis=0, keepdims=True)

    q = q_ref[0]
    gate = lax.dot_general(q, kmean_ref[...], _NT, precision=lax.Precision.HIGHEST,
                           preferred_element_type=F32)
    blk = lax.broadcasted_iota(jnp.int32, gate.shape, 1)
    past = blk < qi
    g = jnp.where(past, gate, NEG)
    selw = jnp.zeros(gate.shape, F32)
    for _ in range(min(MOBA_TOPK, nb)):
        m = jnp.max(g, axis=1, keepdims=True)
        first = jnp.min(jnp.where(g == m, blk, nb), axis=1, keepdims=True)
        pick = blk == first
        selw = jnp.where(pick, 1.0, selw)
        g = jnp.where(pick, -jnp.inf, g)
    selw = jnp.where(past, selw, 0.0)

    scale = ATTN_HEAD_DIM ** -0.5
    rc = (lax.broadcasted_iota(jnp.int32, (blk_len, blk_len), 0)
          - lax.broadcasted_iota(jnp.int32, (blk_len, blk_len), 1))
    rcf = rc.astype(F32)
    slope = jnp.exp2(jnp.zeros((1, 1), F32) - (h + 1).astype(F32))
    qb = q.astype(BF16)

    def scores(kb):
        return lax.dot_general(qb, kb.astype(BF16), _NT, preferred_element_type=F32) * scale

    row0 = pl.multiple_of(qi * blk_len, blk_len)
    s = scores(k_ref[0, pl.ds(row0, blk_len), :]) - slope * rcf
    s = jnp.where(rc >= 0, s, NEG)
    m = jnp.max(s, axis=1, keepdims=True)
    p = jnp.exp(s - m)
    l = jnp.sum(p, axis=1, keepdims=True)
    acc = _dot(p, v_ref[0, pl.ds(row0, blk_len), :])

    def body(j, carry):
        m, l, acc = carry
        r0 = pl.multiple_of(j * blk_len, blk_len)
        selj = jnp.max(jnp.where(blk == j, selw, 0.0), axis=1, keepdims=True)
        off = ((qi - j) * blk_len).astype(F32)
        s = scores(k_ref[0, pl.ds(r0, blk_len), :]) - slope * (rcf + off)
        s = jnp.where(selj > 0.0, s, NEG)
        m_new = jnp.maximum(m, jnp.max(s, axis=1, keepdims=True))
        alpha = jnp.exp(m - m_new)
        p = jnp.exp(s - m_new)
        l = alpha * l + jnp.sum(p, axis=1, keepdims=True)
        acc = alpha * acc + _dot(p, v_ref[0, pl.ds(r0, blk_len), :])
        return m_new, l, acc

    m, l, acc = lax.fori_loop(0, qi, body, (m, l, acc))
    o_ref[0] = acc / l


def _moba(proj, batch, seq):
    nb = seq // MOBA_BLOCK
    hd = ATTN_HEAD_DIM
    kern = lambda *refs: _moba_kernel(*refs, nb=nb)
    return pl.pallas_call(
        kern,
        grid=(batch, ATTN_HEADS, nb),
        in_specs=[pl.BlockSpec((1, MOBA_BLOCK, hd), lambda b, h, i: (b, i, h)),
                  pl.BlockSpec((1, seq, hd), lambda b, h, i: (b, 0, ATTN_HEADS + h)),
                  pl.BlockSpec((1, seq, hd), lambda b, h, i: (b, 0, 2 * ATTN_HEADS + h))],
        out_specs=pl.BlockSpec((1, MOBA_BLOCK, hd), lambda b, h, i: (b, i, h)),
        out_shape=jax.ShapeDtypeStruct((batch, seq, ATTN_WIDTH), F32),
        scratch_shapes=[pltpu.VMEM((nb, hd), F32)],
        compiler_params=_params(("parallel", "parallel", "arbitrary")),
        name="moba",
    )(proj, proj, proj)


def _rwkv_prep_kernel(r_ref, k_ref, v_ref, l_ref, rp_ref, kp_ref, vp_ref, lp_ref,
                      mu_r, mu_k, mu_v, mu_l, w0, w2p, a0, a2p, g2p, kk_w, ka_w, rk_w, bd_ref,
                      r_o, lw_o, k_o, v_o, kk_o, b_o, g_o, bonus_o):
    first = pl.program_id(1) == 0

    def shift(cur_ref, prev_ref, mu_ref):
        cur = cur_ref[0]
        prev_last = jnp.where(first, 0.0, prev_ref[0, 7:8, :])
        rows = lax.broadcasted_iota(jnp.int32, cur.shape, 0)
        prev = jnp.where(rows == 0, prev_last, pltpu.roll(cur, 1, axis=0))
        return cur + (prev - cur) * mu_ref[...]

    r = shift(r_ref, rp_ref, mu_r)
    k = shift(k_ref, kp_ref, mu_k)
    v = shift(v_ref, vp_ref, mu_v)
    lo = shift(l_ref, lp_ref, mu_l)

    bd = bd_ref[...]
    z = -(w0[...] + _dot(jnp.tanh(lo), w2p[...]))
    softplus = jnp.maximum(z, 0.0) + jnp.log1p(jnp.exp(-jnp.abs(z)))
    lw = -jnp.exp(-softplus - 0.5)
    a = jax.nn.sigmoid(a0[...] + _dot(lo, a2p[...]))
    g = _dot(jax.nn.sigmoid(lo), g2p[...])
    kk = k * kk_w[...]
    kk = kk / jnp.maximum(jnp.sqrt(_dot_exact_rhs(kk * kk, bd)), 1e-12)
    k2 = k * (1.0 + (a - 1.0) * ka_w[...])
    r_o[0] = r
    lw_o[0] = lw
    k_o[0] = k2
    v_o[0] = v
    kk_o[0] = kk
    b_o[0] = kk * a
    g_o[0] = g
    bonus_o[0] = _dot_exact_rhs(r * k2 * rk_w[...], bd) * v


def _rwkv_prep(proj, batch, seq, weights, tt):
    w = RWKV_WIDTH
    c0 = 3 * ATTN_WIDTH // w
    lc = (3 * ATTN_WIDTH + 3 * w) // LORA_PAD
    pb = tt // 8
    cur = lambda width, col: pl.BlockSpec((1, tt, width), lambda b, i: (b, i, col))
    prev = lambda width, col: pl.BlockSpec(
        (1, 8, width), lambda b, i: (b, jnp.maximum(i * pb - 1, 0), col))
    full = lambda a: pl.BlockSpec(a.shape, lambda b, i: (0,) * a.ndim)
    out = pl.BlockSpec((1, tt, w), lambda b, i: (b, i, 0))
    return pl.pallas_call(
        _rwkv_prep_kernel,
        grid=(batch, seq // tt),
        in_specs=[cur(w, c0), cur(w, c0 + 1), cur(w, c0 + 2), cur(LORA_PAD, lc),
                  prev(w, c0), prev(w, c0 + 1), prev(w, c0 + 2), prev(LORA_PAD, lc)]
                 + [full(a) for a in weights],
        out_specs=[out] * 8,
        out_shape=[jax.ShapeDtypeStruct((batch, seq, w), F32)] * 8,
        compiler_params=_params(("parallel", "arbitrary")),
        name="rwkv_prep",
    )(proj, proj, proj, proj, proj, proj, proj, proj, *weights)


def _scan_kernel(r_ref, lw_ref, k_ref, v_ref, kk_ref, b_ref, y_ref, s_ref, *, tb, pairs):
    c = SCAN_CHUNK
    c2 = 2 * c

    @pl.when(pl.program_id(1) == 0)
    def _():
        s_ref[...] = jnp.zeros_like(s_ref)

    head0 = lax.broadcasted_iota(jnp.int32, (c, LANES), 1) < RWKV_HEAD_DIM
    row = lax.broadcasted_iota(jnp.int32, (c2, c2), 0)
    col = lax.broadcasted_iota(jnp.int32, (c2, c2), 1)
    same_head = (row // c) == (col // c)
    strict = jnp.logical_and(same_head, col < row)
    incl = jnp.logical_and(same_head, col <= row)
    inv_blk = (row // SCAN_INV_BLOCK) == (col // SCAN_INV_BLOCK)
    eye = jnp.where(row == col, 1.0, 0.0).astype(F32)
    tri = jnp.where(lax.broadcasted_iota(jnp.int32, (c, c), 0)
                    >= lax.broadcasted_iota(jnp.int32, (c, c), 1), 1.0, 0.0).astype(BF16)

    def stack(x):
        return jnp.concatenate([jnp.where(head0, x, 0.0), jnp.where(head0, 0.0, x)], axis=0)

    def pair_chunk(t0, p):
        lanes = slice(p * LANES, (p + 1) * LANES)
        rows = pl.ds(t0, c)
        lw = lw_ref[0, rows, lanes]
        cum = _dot_exact_lhs(tri, lw)
        yield
        r = r_ref[0, rows, lanes]
        k = k_ref[0, rows, lanes]
        kk = kk_ref[0, rows, lanes]
        b = b_ref[0, rows, lanes]
        cum_end = cum[c - 1:c, :]
        e_neg = jnp.exp(-cum)
        e_end = jnp.exp(cum_end - cum)
        a_s = stack(-kk * jnp.exp(cum - lw))
        r_s = stack(r * jnp.exp(cum))
        b_s = stack(b * e_neg)
        k_s = stack(k * e_neg)
        bend_s = stack(b * e_end)
        kend_s = stack(k * e_end)
        v_s = stack(v_ref[0, rows, lanes])
        gram = _dot_nt(jnp.concatenate([a_s, r_s], axis=0),
                       jnp.concatenate([b_s, k_s], axis=0))
        kv = _dot_tn(v_s, kend_s)
        yield
        l_ab = jnp.where(strict, gram[:c2, :c2], 0.0)
        l_ak = jnp.where(strict, gram[:c2, c2:], 0.0)
        l_rb = jnp.where(incl, gram[c2:, :c2], 0.0)
        l_rk = jnp.where(incl, gram[c2:, c2:], 0.0)

        x = jnp.where(inv_blk, l_ab, 0.0)
        off = jnp.where(inv_blk, 0.0, l_ab)
        dinv = eye + x
        w1 = _dot(l_ak, v_s)
        y1 = _dot(l_rk, v_s)
        for _ in range(int(math.log2(SCAN_INV_BLOCK)) - 1):
            x = _dot(x, x)
            yield
            dinv = dinv + _dot(dinv, x)
            yield
        f = _dot(dinv, off)
        yield
        f2 = _dot(f, f)
        yield
        gm = eye + f
        gm = gm + _dot(gm, f2)
        yield
        t_inv = _dot(gm, dinv)
        yield
        ta = _dot(t_inv, jnp.concatenate([a_s, w1], axis=1))
        yield
        a_hat = ta[:, :LANES]
        v_hat = ta[:, LANES:]
        state = s_ref[p]
        m1 = _dot_nt(jnp.concatenate([a_hat, r_s], axis=0), state)
        yield
        u = m1[:c2] + v_hat
        y = m1[c2:] + y1 + _dot(l_rb, u)
        s_ref[p] = state * jnp.exp(cum_end) + kv + _dot_tn(u, bend_s)
        y_ref[0, rows, lanes] = y[:c] + y[c:]

    def chunk(ci, carry):
        t0 = pl.multiple_of(ci * c, c)
        for _ in itertools.zip_longest(*[pair_chunk(t0, p) for p in range(pairs)]):
            pass
        return carry

    lax.fori_loop(0, tb // c, chunk, 0)


def _rwkv_scan(r, lw, k, v, kk, b, tb):
    batch, seq, w = r.shape
    pairs = w // LANES
    spec = pl.BlockSpec((1, tb, w), lambda bi, t: (bi, t, 0))
    kern = lambda *refs: _scan_kernel(*refs, tb=tb, pairs=pairs)
    return pl.pallas_call(
        kern,
        grid=(batch, seq // tb),
        in_specs=[spec] * 6,
        out_specs=spec,
        out_shape=jax.ShapeDtypeStruct((batch, seq, w), F32),
        scratch_shapes=[pltpu.VMEM((pairs, LANES, LANES), F32)],
        compiler_params=_params(("parallel", "arbitrary")),
        name="rwkv_scan",
    )(r, lw, k, v, kk, b)


def _out_proj_kernel(x_ref, attn_ref, y_ref, bonus_ref, g_ref, lng, lnb, bd_ref, w_ref, n2g,
                     h_ref, hn_ref):
    bd = bd_ref[...]
    y = y_ref[...]
    inv_n = 1.0 / RWKV_HEAD_DIM
    d = y - _dot_exact_rhs(y, bd) * inv_n
    var = _dot_exact_rhs(d * d, bd) * inv_n
    yn = d * lax.rsqrt(var + RWKV_GN_EPS) * lng[...] + lnb[...]
    rw = (yn + bonus_ref[...]) * g_ref[...]
    h = (x_ref[...] + _dot(attn_ref[...], w_ref[:ATTN_WIDTH, :])
         + _dot(rw, w_ref[ATTN_WIDTH:, :]))
    h_ref[...] = h
    inv = lax.rsqrt(jnp.mean(h * h, axis=-1, keepdims=True) + RMS_EPS)
    hn_ref[...] = (h * inv * n2g[...]).astype(BF16)


def _out_proj(x, attn, y, bonus, g, lng, lnb, bd, w_out, n2g, tm):
    m = x.shape[0]
    row = lambda width: pl.BlockSpec((tm, width), lambda i: (i, 0))
    full = lambda a: pl.BlockSpec(a.shape, lambda i: (0,) * a.ndim)
    return pl.pallas_call(
        _out_proj_kernel,
        grid=(m // tm,),
        in_specs=[row(D_MODEL), row(ATTN_WIDTH), row(RWKV_WIDTH), row(RWKV_WIDTH), row(RWKV_WIDTH),
                  full(lng), full(lnb), full(bd), full(w_out), full(n2g)],
        out_specs=[row(D_MODEL), row(D_MODEL)],
        out_shape=[jax.ShapeDtypeStruct((m, D_MODEL), F32),
                   jax.ShapeDtypeStruct((m, D_MODEL), BF16)],
        compiler_params=_params(("parallel",)),
        name="out_proj",
    )(x, attn, y, bonus, g, lng, lnb, bd, w_out, n2g)


def _peer_q_kernel(hn_ref, wq_ref, keys_ref, st_ref):
    q = jnp.dot(hn_ref[...], wq_ref[...], preferred_element_type=F32)
    for hp in range(2 * PEER_HEADS):
        qs = q[:, hp * PEER_HALF:(hp + 1) * PEER_HALF]
        st_ref[hp] = _dot_nt(keys_ref[hp], qs)


def _peer_q(hn, wq, keys, tm):
    m = hn.shape[0]
    return pl.pallas_call(
        _peer_q_kernel,
        grid=(m // tm,),
        in_specs=[pl.BlockSpec((tm, D_MODEL), lambda i: (i, 0)),
                  pl.BlockSpec(wq.shape, lambda i: (0, 0)),
                  pl.BlockSpec(keys.shape, lambda i: (0, 0, 0))],
        out_specs=pl.BlockSpec((2 * PEER_HEADS, PEER_NKEYS, tm), lambda i: (0, 0, i)),
        out_shape=jax.ShapeDtypeStruct((2 * PEER_HEADS, PEER_NKEYS, m), F32),
        compiler_params=_params(("parallel",)),
        name="peer_q",
    )(hn, wq, keys)


def _top_desc(x, k):
    n = x.shape[0]
    rows = lax.broadcasted_iota(jnp.int32, x.shape, 0)
    outs = []
    for _ in range(k):
        m = jnp.max(x, axis=0, keepdims=True)
        outs.append(m)
        idx = jnp.min(jnp.where(x == m, rows, n), axis=0, keepdims=True)
        x = jnp.where(rows == idx, -jnp.inf, x)
    return jnp.concatenate(outs, axis=0)


def _peer_topk_kernel(s_ref, e1_ref, e2_ref, thr_ref):
    s1 = s_ref[0]
    s2 = s_ref[1]
    sv1 = _top_desc(s1, PEER_TOPK)
    sv2 = _top_desc(s2, PEER_TOPK)
    cand = jnp.concatenate([sv1[a:a + 1] + sv2 for a in range(PEER_TOPK)], axis=0)
    best = _top_desc(cand, PEER_TOPK)
    z = jnp.sum(jnp.exp(best - best[0:1]), axis=0, keepdims=True)
    e1_ref[0] = jnp.exp(s1 - sv1[0:1]) / z
    e2_ref[0] = jnp.exp(s2 - sv2[0:1])
    thr_ref[0] = best[PEER_TOPK - 1:PEER_TOPK]


def _peer_topk(st, tl):
    m = st.shape[2]
    e_spec = pl.BlockSpec((1, PEER_NKEYS, tl), lambda h, i: (h, 0, i))
    e_shape = jax.ShapeDtypeStruct((PEER_HEADS, PEER_NKEYS, m), F32)
    return pl.pallas_call(
        _peer_topk_kernel,
        grid=(PEER_HEADS, m // tl),
        in_specs=[pl.BlockSpec((2, PEER_NKEYS, tl), lambda h, i: (h, 0, i))],
        out_specs=[e_spec, e_spec, pl.BlockSpec((1, 1, tl), lambda h, i: (h, 0, i))],
        out_shape=[e_shape, e_shape, jax.ShapeDtypeStruct((PEER_HEADS, 1, m), F32)],
        compiler_params=_params(("parallel", "parallel")),
        name="peer_topk",
    )(st)


def _peer_dense_kernel(u_ref, hnt_ref, vt_ref, s_ref, e2_ref, thr_ref, s1a_ref, s1b_ref,
                       e1a_ref, e1b_ref, yt_ref, act_ref, hpre_ref, *, n1):
    step = pl.program_id(1)
    et = hpre_ref.shape[1]
    row_refs = ((s1a_ref, e1a_ref), (s1b_ref, e1b_ref))

    @pl.when(step == 0)
    def _():
        yt_ref[...] = jnp.zeros_like(yt_ref)
        hpre_ref[1] = jnp.zeros(hpre_ref.shape[1:], hpre_ref.dtype)
        act_ref[0] = jnp.zeros(act_ref.shape[1:], act_ref.dtype)

    for half in range(2):
        cur, prev = half, 1 - half
        hs = slice(half * et, (half + 1) * et)
        hpre_ref[cur] = jnp.dot(u_ref[hs, :], hnt_ref[...], preferred_element_type=F32)
        yt_ref[...] += jnp.dot(vt_ref[:, hs], act_ref[cur].astype(BF16),
                               preferred_element_type=F32)

        s1r_ref, e1r_ref = row_refs[half]
        for jt in range(hpre_ref.shape[2] // LANES):
            ls = slice(jt * LANES, (jt + 1) * LANES)
            for ib in range(PEER_NKEYS // DENSE_ROWS):
                rs = slice(ib * DENSE_ROWS, (ib + 1) * DENSE_ROWS)
                gates = [jnp.zeros((DENSE_ROWS, LANES), F32) for _ in range(n1)]
                for h in range(PEER_HEADS):
                    s2 = s_ref[2 * h + 1, rs, ls]
                    e2 = e2_ref[h, rs, ls]
                    thr = thr_ref[h, :, ls]
                    for a in range(n1):
                        s1row = s1r_ref[0, h * n1 + a:h * n1 + a + 1, ls]
                        e1row = e1r_ref[0, h * n1 + a:h * n1 + a + 1, ls]
                        gates[a] = gates[a] + jnp.where((s1row + s2) >= thr, e2, 0.0) * e1row
                for a in range(n1):
                    ers = slice(a * PEER_NKEYS + ib * DENSE_ROWS,
                                a * PEER_NKEYS + (ib + 1) * DENSE_ROWS)
                    x = hpre_ref[prev, ers, ls]
                    gelu = 0.5 * x * (1.0 + lax.erf(x * (0.5 ** 0.5)))
                    act_ref[prev, ers, ls] = gelu * gates[a]


def _peer_dense(u, hnt, vt, st, e1, e2, thr, tt, n1):
    m = hnt.shape[1]
    et = n1 * PEER_NKEYS
    n_steps = PEER_NKEYS // (2 * n1)
    n_tiles = 2 * n_steps
    tok3 = lambda a: pl.BlockSpec((a.shape[0], a.shape[1], tt), lambda i, e: (0, 0, i))
    by_tile = lambda a: a.reshape(PEER_HEADS, n_tiles, n1, m).transpose(1, 0, 2, 3).reshape(
        n_tiles, PEER_HEADS * n1, m)
    s1r = by_tile(st.reshape(PEER_HEADS, 2, PEER_NKEYS, m)[:, 0])
    e1r = by_tile(e1)
    rows = lambda half: pl.BlockSpec(
        (1, PEER_HEADS * n1, tt), lambda i, e: (jnp.clip(2 * e + half - 1, 0, n_tiles - 1), 0, i))
    kern = lambda *refs: _peer_dense_kernel(*refs, n1=n1)
    return pl.pallas_call(
        kern,
        grid=(m // tt, n_steps + 1),
        in_specs=[pl.BlockSpec((2 * et, D_MODEL), lambda i, e: (jnp.minimum(e, n_steps - 1), 0)),
                  pl.BlockSpec((D_MODEL, tt), lambda i, e: (0, i)),
                  pl.BlockSpec((D_MODEL, 2 * et), lambda i, e: (0, jnp.maximum(e - 1, 0))),
                  tok3(st), tok3(e2), tok3(thr), rows(0), rows(1), rows(0), rows(1)],
        out_specs=pl.BlockSpec((D_MODEL, tt), lambda i, e: (0, i)),
        out_shape=jax.ShapeDtypeStruct((D_MODEL, m), F32),
        scratch_shapes=[pltpu.VMEM((2, et, tt), F32), pltpu.VMEM((2, et, tt), F32)],
        compiler_params=_params(("parallel", "arbitrary")),
        name="peer_dense",
    )(u, hnt, vt, st, e2, thr, s1r, s1r, e1r, e1r)


def _final_kernel(h_ref, y_ref, g_ref, o_ref):
    h = h_ref[...] + y_ref[...]
    inv = lax.rsqrt(jnp.mean(h * h, axis=-1, keepdims=True) + RMS_EPS)
    o_ref[...] = h * inv * g_ref[...]


def _final(h, y, g, tm):
    m = h.shape[0]
    row = pl.BlockSpec((tm, D_MODEL), lambda i: (i, 0))
    return pl.pallas_call(
        _final_kernel,
        grid=(m // tm,),
        in_specs=[row, row, pl.BlockSpec((1, D_MODEL), lambda i: (0, 0))],
        out_specs=row,
        out_shape=jax.ShapeDtypeStruct((m, D_MODEL), F32),
        compiler_params=_params(("parallel",)),
        name="final_norm",
    )(h, y, g)


def _row(a):
    return a.reshape(1, -1).astype(F32)


def _pad_rows(w, start, total):
    return jnp.zeros((total, w.shape[1]), BF16).at[start:start + w.shape[0]].set(w.astype(BF16))


def kernel(x, norm1_g, w_in, rwkv_mu, rwkv_w0, rwkv_w2, rwkv_a0, rwkv_a2, rwkv_g2, rwkv_k_k, rwkv_k_a, rwkv_r_k, rwkv_ln_g, rwkv_ln_b, w_out, norm2_g, peer_wq, peer_sub_keys, peer_u, peer_v, final_g):
    batch, seq, d = x.shape
    tokens = batch * seq
    w = RWKV_WIDTH
    xt = x.reshape(tokens, d)

    w_in_p = jnp.pad(w_in[0].astype(BF16), ((0, 0), (0, IN_COLS_PAD - IN_COLS)))
    proj = _norm_matmul(xt, _row(norm1_g[0]), w_in_p, tm=1024, tn=512)
    proj = proj.reshape(batch, seq, IN_COLS_PAD)

    attn = _moba(proj, batch, seq)

    mu = rwkv_mu[0]
    mu_l = jnp.pad(mu[3 * w:], (0, LORA_PAD - LORA_COLS))
    hid = lax.broadcasted_iota(jnp.int32, (w, w), 0) // RWKV_HEAD_DIM
    bd = (hid == hid.T).astype(BF16)
    prep_w = [_row(mu[:w]), _row(mu[w:2 * w]), _row(mu[2 * w:3 * w]), _row(mu_l),
              _row(rwkv_w0[0]), _pad_rows(rwkv_w2[0], 0, LORA_PAD),
              _row(rwkv_a0[0]), _pad_rows(rwkv_a2[0], DECAY_LORA, LORA_PAD),
              _pad_rows(rwkv_g2[0], DECAY_LORA + AAA_LORA, LORA_PAD),
              _row(rwkv_k_k[0]), _row(rwkv_k_a[0]), _row(rwkv_r_k[0]), bd]
    r, lw, k2, v, kk, b, g, bonus = _rwkv_prep(proj, batch, seq, prep_w, tt=256)
    y = _rwkv_scan(r, lw, k2, v, kk, b, tb=256)

    flat = lambda a: a.reshape(tokens, a.shape[-1])
    h, hn = _out_proj(xt, flat(attn), flat(y), flat(bonus), flat(g), _row(rwkv_ln_g[0]),
                      _row(rwkv_ln_b[0]), bd, w_out[0].astype(BF16), _row(norm2_g[0]), tm=256)

    keys = peer_sub_keys[0].reshape(2 * PEER_HEADS, PEER_NKEYS, PEER_HALF).astype(BF16)
    st = _peer_q(hn, peer_wq[0].astype(BF16), keys, tm=512)
    e1, e2, thr = _peer_topk(st, tl=512)
    yt = _peer_dense(peer_u[0].astype(BF16), hn.T, peer_v[0].T.astype(BF16), st, e1, e2, thr,
                     tt=512, n1=4)
    out = _final(h, yt.T, final_g.reshape(1, d), tm=512)
    return out.reshape(batch, seq, d)
```

```python
import itertools
import math

import jax
import jax.numpy as jnp
from jax import lax
from jax.experimental import pallas as pl
from jax.experimental.pallas import tpu as pltpu

F32 = jnp.float32
BF16 = jnp.bfloat16

D_MODEL = 2048
ATTN_HEADS = 8
ATTN_HEAD_DIM = 128
ATTN_WIDTH = ATTN_HEADS * ATTN_HEAD_DIM
MOBA_BLOCK = 256
MOBA_TOPK = 3

RWKV_HEAD_DIM = 64
RWKV_WIDTH = D_MODEL - ATTN_WIDTH
RWKV_HEADS = RWKV_WIDTH // RWKV_HEAD_DIM
DECAY_LORA = 96
AAA_LORA = 96
GATE_LORA = 256
LORA_COLS = DECAY_LORA + AAA_LORA + GATE_LORA
LORA_PAD = 512
RWKV_GN_EPS = RWKV_HEAD_DIM * 1e-5
IN_COLS = 3 * ATTN_WIDTH + 3 * RWKV_WIDTH + LORA_COLS
IN_COLS_PAD = 3 * ATTN_WIDTH + 3 * RWKV_WIDTH + LORA_PAD

PEER_HEADS = 8
PEER_NKEYS = 128
PEER_HALF = 128
PEER_TOPK = 16

RMS_EPS = 1e-6
NEG = -1e30

LANES = 128
SCAN_CHUNK = 64
SCAN_INV_BLOCK = 16
DENSE_ROWS = 64
VMEM_LIMIT = 56 * 1024 * 1024

_NT = (((1,), (1,)), ((), ()))
_TN = (((0,), (0,)), ((), ()))


def _params(sem, flags=None):
    return pltpu.CompilerParams(dimension_semantics=sem, vmem_limit_bytes=VMEM_LIMIT, flags=flags)


def _dot(a, b):
    return jnp.dot(a.astype(BF16), b.astype(BF16), preferred_element_type=F32)


def _dot_nt(a, b):
    return lax.dot_general(a.astype(BF16), b.astype(BF16), _NT, preferred_element_type=F32)


def _dot_tn(a, b):
    return lax.dot_general(a.astype(BF16), b.astype(BF16), _TN, preferred_element_type=F32)


def _split3(x):
    hi = x.astype(BF16)
    r1 = x - hi.astype(F32)
    mid = r1.astype(BF16)
    lo = (r1 - mid.astype(F32)).astype(BF16)
    return hi, mid, lo


def _dot_exact_rhs(x, ones_bf16):
    hi, mid, lo = _split3(x)
    d = lambda p: jnp.dot(p, ones_bf16, preferred_element_type=F32)
    return d(hi) + d(mid) + d(lo)


def _dot_exact_lhs(ones_bf16, x):
    hi, mid, lo = _split3(x)
    d = lambda p: jnp.dot(ones_bf16, p, preferred_element_type=F32)
    return d(hi) + d(mid) + d(lo)


def _norm_matmul_kernel(x_ref, g_ref, w_ref, o_ref, xn_ref):
    @pl.when(pl.program_id(1) == 0)
    def _():
        x = x_ref[...]
        inv = lax.rsqrt(jnp.mean(x * x, axis=-1, keepdims=True) + RMS_EPS)
        xn_ref[...] = (x * inv * g_ref[...]).astype(BF16)

    o_ref[...] = jnp.dot(xn_ref[...], w_ref[...], preferred_element_type=F32)


def _norm_matmul(x, g, w, tm, tn):
    m, k = x.shape
    n = w.shape[1]
    return pl.pallas_call(
        _norm_matmul_kernel,
        grid=(m // tm, n // tn),
        in_specs=[pl.BlockSpec((tm, k), lambda i, j: (i, 0)),
                  pl.BlockSpec((1, k), lambda i, j: (0, 0)),
                  pl.BlockSpec((k, tn), lambda i, j: (0, j))],
        out_specs=pl.BlockSpec((tm, tn), lambda i, j: (i, j)),
        out_shape=jax.ShapeDtypeStruct((m, n), F32),
        scratch_shapes=[pltpu.VMEM((tm, k), BF16)],
        compiler_params=_params(("parallel", "arbitrary")),
        name="in_proj",
    )(x, g, w)


def _moba_kernel(q_ref, k_ref, v_ref, o_ref, kmean_ref, kb_ref, vt_ref, *, nb, heads):
    blk_len = MOBA_BLOCK
    hd = ATTN_HEAD_DIM
    hg = pl.program_id(1)
    qi = pl.program_id(2)
    hrange = range(heads)

    @pl.when(qi == 0)
    def _():
        for hh in hrange:
            hl = slice(hh * hd, (hh + 1) * hd)
            for j in range(nb):
                rows = slice(j * blk_len, (j + 1) * blk_len)
                kj = k_ref[0, rows, hl]
                kmean_ref[hh, j:j + 1, :] = jnp.mean(kj, axis=0, keepdims=True)
                kb_ref[hh, rows, :] = kj.astype(BF16)
                vt_ref[hh, :, rows] = v_ref[0, rows, hl].T.astype(BF16)

    log2e = math.log2(math.e)
    scale2 = (hd ** -0.5) * log2e
    kq = (lax.broadcasted_iota(jnp.int32, (blk_len, blk_len), 1)
          - lax.broadcasted_iota(jnp.int32, (blk_len, blk_len), 0))
    kqf = kq.astype(F32)
    blk = lax.broadcasted_iota(jnp.int32, (nb, blk_len), 0)
    past = blk < qi

    def select(hh):
        q = q_ref[0, :, hh * hd:(hh + 1) * hd]
        gate = lax.dot_general(kmean_ref[hh], q, _NT, precision=lax.Precision.HIGHEST,
                               preferred_element_type=F32)
        g = jnp.where(past, gate, NEG)
        selw = jnp.zeros(gate.shape, F32)
        for _ in range(min(MOBA_TOPK, nb)):
            m = jnp.max(g, axis=0, keepdims=True)
            first = jnp.min(jnp.where(g == m, blk, nb), axis=0, keepdims=True)
            pick = blk == first
            selw = jnp.where(pick, 1.0, selw)
            g = jnp.where(pick, -jnp.inf, g)
        return jnp.where(past, selw, 0.0), q.astype(BF16)

    sel_q = [select(hh) for hh in hrange]
    selw = [x[0] for x in sel_q]
    qb = [x[1] for x in sel_q]
    slope2 = [jnp.exp2(jnp.zeros((1, 1), F32) - (hg * heads + hh + 1).astype(F32)) * log2e
              for hh in hrange]
    bias0 = [slope2[hh] * kqf for hh in hrange]

    def scores(j):
        r0 = pl.multiple_of(j * blk_len, blk_len)
        return [lax.dot_general(kb_ref[hh, pl.ds(r0, blk_len), :], qb[hh], _NT,
                                preferred_element_type=F32) * scale2 - bias0[hh] for hh in hrange]

    def weighted_values(j, p):
        r0 = pl.multiple_of(j * blk_len, blk_len)
        return [jnp.dot(vt_ref[hh, :, pl.ds(r0, blk_len)], p[hh].astype(BF16),
                        preferred_element_type=F32) for hh in hrange]

    s = [jnp.where(kq >= 0, x, NEG) for x in scores(qi)]
    m = [jnp.max(x, axis=0, keepdims=True) for x in s]
    p = [jnp.exp2(s[hh] - m[hh]) for hh in hrange]
    l = [jnp.sum(x, axis=0, keepdims=True) for x in p]
    acc = weighted_values(qi, p)

    def body(j, carry):
        m, l, acc = carry
        off = ((qi - j) * blk_len).astype(F32)
        s = scores(j)
        selj = [jnp.max(jnp.where(blk == j, selw[hh], 0.0), axis=0, keepdims=True) for hh in hrange]
        s = [jnp.where(selj[hh] > 0.0, s[hh] - slope2[hh] * off, NEG) for hh in hrange]
        m_new = [jnp.maximum(m[hh], jnp.max(s[hh], axis=0, keepdims=True)) for hh in hrange]
        alpha = [jnp.exp2(m[hh] - m_new[hh]) for hh in hrange]
        p = [jnp.exp2(s[hh] - m_new[hh]) for hh in hrange]
        l = [alpha[hh] * l[hh] + jnp.sum(p[hh], axis=0, keepdims=True) for hh in hrange]
        pv = weighted_values(j, p)
        acc = [alpha[hh] * acc[hh] + pv[hh] for hh in hrange]
        return m_new, l, acc

    m, l, acc = lax.fori_loop(0, qi, body, (m, l, acc))
    for hh in hrange:
        o_ref[0, :, hh * hd:(hh + 1) * hd] = (acc[hh] / l[hh]).T


def _moba(proj, batch, seq, heads):
    nb = seq // MOBA_BLOCK
    hd = ATTN_HEAD_DIM
    hw = heads * hd
    groups = ATTN_HEADS // heads
    kern = lambda *refs: _moba_kernel(*refs, nb=nb, heads=heads)
    return pl.pallas_call(
        kern,
        grid=(batch, groups, nb),
        in_specs=[pl.BlockSpec((1, MOBA_BLOCK, hw), lambda b, h, i: (b, i, h)),
                  pl.BlockSpec((1, seq, hw), lambda b, h, i: (b, 0, groups + h)),
                  pl.BlockSpec((1, seq, hw), lambda b, h, i: (b, 0, 2 * groups + h))],
        out_specs=pl.BlockSpec((1, MOBA_BLOCK, hw), lambda b, h, i: (b, i, h)),
        out_shape=jax.ShapeDtypeStruct((batch, seq, ATTN_WIDTH), F32),
        scratch_shapes=[pltpu.VMEM((heads, nb, hd), F32),
                        pltpu.VMEM((heads, seq, hd), BF16),
                        pltpu.VMEM((heads, hd, seq), BF16)],
        compiler_params=_params(("parallel", "parallel", "arbitrary")),
        name="moba",
    )(proj, proj, proj)


def _rwkv_prep_kernel(r_ref, k_ref, v_ref, l_ref, rp_ref, kp_ref, vp_ref, lp_ref,
                      mu_r, mu_k, mu_v, mu_l, w0, w2p, a0, a2p, g2p, kk_w, ka_w, rk_w, bd_ref,
                      r_o, lw_o, k_o, v_o, kk_o, b_o, g_o, bonus_o):
    first = pl.program_id(1) == 0

    def shift(cur_ref, prev_ref, mu_ref):
        cur = cur_ref[0]
        prev_last = jnp.where(first, 0.0, prev_ref[0, 7:8, :])
        rows = lax.broadcasted_iota(jnp.int32, cur.shape, 0)
        prev = jnp.where(rows == 0, prev_last, pltpu.roll(cur, 1, axis=0))
        return cur + (prev - cur) * mu_ref[...]

    r = shift(r_ref, rp_ref, mu_r)
    k = shift(k_ref, kp_ref, mu_k)
    v = shift(v_ref, vp_ref, mu_v)
    lo = shift(l_ref, lp_ref, mu_l)

    bd = bd_ref[...]
    z = -(w0[...] + _dot(jnp.tanh(lo), w2p[...]))
    softplus = jnp.maximum(z, 0.0) + jnp.log1p(jnp.exp(-jnp.abs(z)))
    lw = -jnp.exp(-softplus - 0.5)
    a = jax.nn.sigmoid(a0[...] + _dot(lo, a2p[...]))
    g = _dot(jax.nn.sigmoid(lo), g2p[...])
    kk = k * kk_w[...]
    kk = kk / jnp.maximum(jnp.sqrt(_dot_exact_rhs(kk * kk, bd)), 1e-12)
    k2 = k * (1.0 + (a - 1.0) * ka_w[...])
    r_o[0] = r
    lw_o[0] = lw
    k_o[0] = k2
    v_o[0] = v
    kk_o[0] = kk
    b_o[0] = kk * a
    g_o[0] = g
    bonus_o[0] = _dot_exact_rhs(r * k2 * rk_w[...], bd) * v


def _rwkv_prep(proj, batch, seq, weights, tt):
    w = RWKV_WIDTH
    c0 = 3 * ATTN_WIDTH // w
    lc = (3 * ATTN_WIDTH + 3 * w) // LORA_PAD
    pb = tt // 8
    cur = lambda width, col: pl.BlockSpec((1, tt, width), lambda b, i: (b, i, col))
    prev = lambda width, col: pl.BlockSpec(
        (1, 8, width), lambda b, i: (b, jnp.maximum(i * pb - 1, 0), col))
    full = lambda a: pl.BlockSpec(a.shape, lambda b, i: (0,) * a.ndim)
    out = pl.BlockSpec((1, tt, w), lambda b, i: (b, i, 0))
    return pl.pallas_call(
        _rwkv_prep_kernel,
        grid=(batch, seq // tt),
        in_specs=[cur(w, c0), cur(w, c0 + 1), cur(w, c0 + 2), cur(LORA_PAD, lc),
                  prev(w, c0), prev(w, c0 + 1), prev(w, c0 + 2), prev(LORA_PAD, lc)]
                 + [full(a) for a in weights],
        out_specs=[out] * 8,
        out_shape=[jax.ShapeDtypeStruct((batch, seq, w), F32)] * 8,
        compiler_params=_params(("parallel", "arbitrary")),
        name="rwkv_prep",
    )(proj, proj, proj, proj, proj, proj, proj, proj, *weights)


def _scan_kernel(r_ref, lw_ref, k_ref, v_ref, kk_ref, b_ref, y_ref, s_ref, *, tb, pairs):
    c = SCAN_CHUNK
    c2 = 2 * c

    @pl.when(pl.program_id(1) == 0)
    def _():
        s_ref[...] = jnp.zeros_like(s_ref)

    head0 = lax.broadcasted_iota(jnp.int32, (c, LANES), 1) < RWKV_HEAD_DIM
    row = lax.broadcasted_iota(jnp.int32, (c2, c2), 0)
    col = lax.broadcasted_iota(jnp.int32, (c2, c2), 1)
    same_head = (row // c) == (col // c)
    strict = jnp.logical_and(same_head, col < row)
    incl = jnp.logical_and(same_head, col <= row)
    inv_blk = (row // SCAN_INV_BLOCK) == (col // SCAN_INV_BLOCK)
    eye = jnp.where(row == col, 1.0, 0.0).astype(F32)
    tri = jnp.where(lax.broadcasted_iota(jnp.int32, (c, c), 0)
                    >= lax.broadcasted_iota(jnp.int32, (c, c), 1), 1.0, 0.0).astype(BF16)

    def stack(x):
        return jnp.concatenate([jnp.where(head0, x, 0.0), jnp.where(head0, 0.0, x)], axis=0)

    def pair_chunk(t0, p):
        lanes = slice(p * LANES, (p + 1) * LANES)
        rows = pl.ds(t0, c)
        lw = lw_ref[0, rows, lanes]
        cum = _dot_exact_lhs(tri, lw)
        yield
        r = r_ref[0, rows, lanes]
        k = k_ref[0, rows, lanes]
        kk = kk_ref[0, rows, lanes]
        b = b_ref[0, rows, lanes]
        cum_end = cum[c - 1:c, :]
        e_neg = jnp.exp(-cum)
        e_end = jnp.exp(cum_end - cum)
        a_s = stack(-kk * jnp.exp(cum - lw))
        r_s = stack(r * jnp.exp(cum))
        b_s = stack(b * e_neg)
        k_s = stack(k * e_neg)
        bend_s = stack(b * e_end)
        kend_s = stack(k * e_end)
        v_s = stack(v_ref[0, rows, lanes])
        gram = _dot_nt(jnp.concatenate([a_s, r_s], axis=0),
                       jnp.concatenate([b_s, k_s], axis=0))
        kv = _dot_tn(v_s, kend_s)
        yield
        l_ab = jnp.where(strict, gram[:c2, :c2], 0.0)
        l_ak = jnp.where(strict, gram[:c2, c2:], 0.0)
        l_rb = jnp.where(incl, gram[c2:, :c2], 0.0)
        l_rk = jnp.where(incl, gram[c2:, c2:], 0.0)

        x = jnp.where(inv_blk, l_ab, 0.0)
        off = jnp.where(inv_blk, 0.0, l_ab)
        dinv = eye + x
        w1 = _dot(l_ak, v_s)
        y1 = _dot(l_rk, v_s)
        for _ in range(int(math.log2(SCAN_INV_BLOCK)) - 1):
            x = _dot(x, x)
            yield
            dinv = dinv + _dot(dinv, x)
            yield
        f = _dot(dinv, off)
        yield
        f2 = _dot(f, f)
        yield
        gm = eye + f
        gm = gm + _dot(gm, f2)
        yield
        t_inv = _dot(gm, dinv)
        yield
        ta = _dot(t_inv, jnp.concatenate([a_s, w1], axis=1))
        yield
        a_hat = ta[:, :LANES]
        v_hat = ta[:, LANES:]
        state = s_ref[p]
        m1 = _dot_nt(jnp.concatenate([a_hat, r_s], axis=0), state)
        yield
        u = m1[:c2] + v_hat
        y = m1[c2:] + y1 + _dot(l_rb, u)
        s_ref[p] = state * jnp.exp(cum_end) + kv + _dot_tn(u, bend_s)
        y_ref[0, rows, lanes] = y[:c] + y[c:]

    def chunk(ci, carry):
        t0 = pl.multiple_of(ci * c, c)
        for _ in itertools.zip_longest(*[pair_chunk(t0, p) for p in range(pairs)]):
            pass
        return carry

    lax.fori_loop(0, tb // c, chunk, 0)


def _rwkv_scan(r, lw, k, v, kk, b, tb):
    batch, seq, w = r.shape
    pairs = w // LANES
    spec = pl.BlockSpec((1, tb, w), lambda bi, t: (bi, t, 0))
    kern = lambda *refs: _scan_kernel(*refs, tb=tb, pairs=pairs)
    return pl.pallas_call(
        kern,
        grid=(batch, seq // tb),
        in_specs=[spec] * 6,
        out_specs=spec,
        out_shape=jax.ShapeDtypeStruct((batch, seq, w), F32),
        scratch_shapes=[pltpu.VMEM((pairs, LANES, LANES), F32)],
        compiler_params=_params(("parallel", "arbitrary")),
        name="rwkv_scan",
    )(r, lw, k, v, kk, b)


def _out_proj_kernel(x_ref, attn_ref, y_ref, bonus_ref, g_ref, lng, lnb, bd_ref, w_ref, n2g,
                     h_ref, hn_ref):
    bd = bd_ref[...]
    y = y_ref[...]
    inv_n = 1.0 / RWKV_HEAD_DIM
    d = y - _dot_exact_rhs(y, bd) * inv_n
    var = _dot_exact_rhs(d * d, bd) * inv_n
    yn = d * lax.rsqrt(var + RWKV_GN_EPS) * lng[...] + lnb[...]
    rw = (yn + bonus_ref[...]) * g_ref[...]
    h = (x_ref[...] + _dot(attn_ref[...], w_ref[:ATTN_WIDTH, :])
         + _dot(rw, w_ref[ATTN_WIDTH:, :]))
    h_ref[...] = h
    inv = lax.rsqrt(jnp.mean(h * h, axis=-1, keepdims=True) + RMS_EPS)
    hn_ref[...] = (h * inv * n2g[...]).astype(BF16)


def _out_proj(x, attn, y, bonus, g, lng, lnb, bd, w_out, n2g, tm):
    m = x.shape[0]
    row = lambda width: pl.BlockSpec((tm, width), lambda i: (i, 0))
    full = lambda a: pl.BlockSpec(a.shape, lambda i: (0,) * a.ndim)
    return pl.pallas_call(
        _out_proj_kernel,
        grid=(m // tm,),
        in_specs=[row(D_MODEL), row(ATTN_WIDTH), row(RWKV_WIDTH), row(RWKV_WIDTH), row(RWKV_WIDTH),
                  full(lng), full(lnb), full(bd), full(w_out), full(n2g)],
        out_specs=[row(D_MODEL), row(D_MODEL)],
        out_shape=[jax.ShapeDtypeStruct((m, D_MODEL), F32),
                   jax.ShapeDtypeStruct((m, D_MODEL), BF16)],
        compiler_params=_params(("parallel",)),
        name="out_proj",
    )(x, attn, y, bonus, g, lng, lnb, bd, w_out, n2g)


def _peer_q_kernel(hn_ref, wq_ref, keys_ref, st_ref):
    q = jnp.dot(hn_ref[...], wq_ref[...], preferred_element_type=F32)
    for hp in range(2 * PEER_HEADS):
        qs = q[:, hp * PEER_HALF:(hp + 1) * PEER_HALF]
        st_ref[hp] = _dot_nt(keys_ref[hp], qs)


def _peer_q(hn, wq, keys, tm):
    m = hn.shape[0]
    return pl.pallas_call(
        _peer_q_kernel,
        grid=(m // tm,),
        in_specs=[pl.BlockSpec((tm, D_MODEL), lambda i: (i, 0)),
                  pl.BlockSpec(wq.shape, lambda i: (0, 0)),
                  pl.BlockSpec(keys.shape, lambda i: (0, 0, 0))],
        out_specs=pl.BlockSpec((2 * PEER_HEADS, PEER_NKEYS, tm), lambda i: (0, 0, i)),
        out_shape=jax.ShapeDtypeStruct((2 * PEER_HEADS, PEER_NKEYS, m), F32),
        compiler_params=_params(("parallel",)),
        name="peer_q",
    )(hn, wq, keys)


def _top_desc(x, k):
    n = x.shape[0]
    rows = lax.broadcasted_iota(jnp.int32, x.shape, 0)
    outs = []
    for _ in range(k):
        m = jnp.max(x, axis=0, keepdims=True)
        outs.append(m)
        idx = jnp.min(jnp.where(x == m, rows, n), axis=0, keepdims=True)
        x = jnp.where(rows == idx, -jnp.inf, x)
    return jnp.concatenate(outs, axis=0)


def _peer_topk_kernel(s_ref, e1_ref, e2_ref, thr_ref):
    s1 = s_ref[0]
    s2 = s_ref[1]
    sv1 = _top_desc(s1, PEER_TOPK)
    sv2 = _top_desc(s2, PEER_TOPK)
    cand = jnp.concatenate([sv1[a:a + 1] + sv2 for a in range(PEER_TOPK)], axis=0)
    best = _top_desc(cand, PEER_TOPK)
    z = jnp.sum(jnp.exp(best - best[0:1]), axis=0, keepdims=True)
    e1_ref[0] = jnp.exp(s1 - sv1[0:1]) / z
    e2_ref[0] = jnp.exp(s2 - sv2[0:1])
    thr_ref[0] = best[PEER_TOPK - 1:PEER_TOPK]


def _peer_topk(st, tl):
    m = st.shape[2]
    e_spec = pl.BlockSpec((1, PEER_NKEYS, tl), lambda h, i: (h, 0, i))
    e_shape = jax.ShapeDtypeStruct((PEER_HEADS, PEER_NKEYS, m), F32)
    return pl.pallas_call(
        _peer_topk_kernel,
        grid=(PEER_HEADS, m // tl),
        in_specs=[pl.BlockSpec((2, PEER_NKEYS, tl), lambda h, i: (h, 0, i))],
        out_specs=[e_spec, e_spec, pl.BlockSpec((1, 1, tl), lambda h, i: (h, 0, i))],
        out_shape=[e_shape, e_shape, jax.ShapeDtypeStruct((PEER_HEADS, 1, m), F32)],
        compiler_params=_params(("parallel", "parallel")),
        name="peer_topk",
    )(st)


def _peer_dense_kernel(u_ref, hnt_ref, vt_ref, s_ref, e2_ref, thr_ref, s1a_ref, s1b_ref,
                       e1a_ref, e1b_ref, yt_ref, act_ref, hpre_ref, *, n1):
    step = pl.program_id(1)
    et = hpre_ref.shape[1]
    row_refs = ((s1a_ref, e1a_ref), (s1b_ref, e1b_ref))

    @pl.when(step == 0)
    def _():
        yt_ref[...] = jnp.zeros_like(yt_ref)
        hpre_ref[1] = jnp.zeros(hpre_ref.shape[1:], hpre_ref.dtype)
        act_ref[0] = jnp.zeros(act_ref.shape[1:], act_ref.dtype)

    for half in range(2):
        cur, prev = half, 1 - half
        hs = slice(half * et, (half + 1) * et)
        hpre_ref[cur] = jnp.dot(u_ref[hs, :], hnt_ref[...], preferred_element_type=F32)
        yt_ref[...] += jnp.dot(vt_ref[:, hs], act_ref[cur].astype(BF16),
                               preferred_element_type=F32)

        s1r_ref, e1r_ref = row_refs[half]
        for jt in range(hpre_ref.shape[2] // LANES):
            ls = slice(jt * LANES, (jt + 1) * LANES)
            for ib in range(PEER_NKEYS // DENSE_ROWS):
                rs = slice(ib * DENSE_ROWS, (ib + 1) * DENSE_ROWS)
                gates = [jnp.zeros((DENSE_ROWS, LANES), F32) for _ in range(n1)]
                for h in range(PEER_HEADS):
                    s2 = s_ref[2 * h + 1, rs, ls]
                    e2 = e2_ref[h, rs, ls]
                    thr = thr_ref[h, :, ls]
                    for a in range(n1):
                        s1row = s1r_ref[0, h * n1 + a:h * n1 + a + 1, ls]
                        e1row = e1r_ref[0, h * n1 + a:h * n1 + a + 1, ls]
                        gates[a] = gates[a] + jnp.where((s1row + s2) >= thr, e2, 0.0) * e1row
                for a in range(n1):
                    ers = slice(a * PEER_NKEYS + ib * DENSE_ROWS,
                                a * PEER_NKEYS + (ib + 1) * DENSE_ROWS)
                    x = hpre_ref[prev, ers, ls]
                    gelu = 0.5 * x * (1.0 + lax.erf(x * (0.5 ** 0.5)))
                    act_ref[prev, ers, ls] = gelu * gates[a]


def _peer_dense(u, hnt, vt, st, e1, e2, thr, tt, n1):
    m = hnt.shape[1]
    et = n1 * PEER_NKEYS
    n_steps = PEER_NKEYS // (2 * n1)
    n_tiles = 2 * n_steps
    tok3 = lambda a: pl.BlockSpec((a.shape[0], a.shape[1], tt), lambda i, e: (0, 0, i))
    by_tile = lambda a: a.reshape(PEER_HEADS, n_tiles, n1, m).transpose(1, 0, 2, 3).reshape(
        n_tiles, PEER_HEADS * n1, m)
    s1r = by_tile(st.reshape(PEER_HEADS, 2, PEER_NKEYS, m)[:, 0])
    e1r = by_tile(e1)
    rows = lambda half: pl.BlockSpec(
        (1, PEER_HEADS * n1, tt), lambda i, e: (jnp.clip(2 * e + half - 1, 0, n_tiles - 1), 0, i))
    kern = lambda *refs: _peer_dense_kernel(*refs, n1=n1)
    return pl.pallas_call(
        kern,
        grid=(m // tt, n_steps + 1),
        in_specs=[pl.BlockSpec((2 * et, D_MODEL), lambda i, e: (jnp.minimum(e, n_steps - 1), 0)),
                  pl.BlockSpec((D_MODEL, tt), lambda i, e: (0, i)),
                  pl.BlockSpec((D_MODEL, 2 * et), lambda i, e: (0, jnp.maximum(e - 1, 0))),
                  tok3(st), tok3(e2), tok3(thr), rows(0), rows(1), rows(0), rows(1)],
        out_specs=pl.BlockSpec((D_MODEL, tt), lambda i, e: (0, i)),
        out_shape=jax.ShapeDtypeStruct((D_MODEL, m), F32),
        scratch_shapes=[pltpu.VMEM((2, et, tt), F32), pltpu.VMEM((2, et, tt), F32)],
        compiler_params=_params(("parallel", "arbitrary")),
        name="peer_dense",
    )(u, hnt, vt, st, e2, thr, s1r, s1r, e1r, e1r)


def _final_kernel(h_ref, y_ref, g_ref, o_ref):
    h = h_ref[...] + y_ref[...]
    inv = lax.rsqrt(jnp.mean(h * h, axis=-1, keepdims=True) + RMS_EPS)
    o_ref[...] = h * inv * g_ref[...]


def _final(h, y, g, tm):
    m = h.shape[0]
    row = pl.BlockSpec((tm, D_MODEL), lambda i: (i, 0))
    return pl.pallas_call(
        _final_kernel,
        grid=(m // tm,),
        in_specs=[row, row, pl.BlockSpec((1, D_MODEL), lambda i: (0, 0))],
        out_specs=row,
        out_shape=jax.ShapeDtypeStruct((m, D_MODEL), F32),
        compiler_params=_params(("parallel",)),
        name="final_norm",
    )(h, y, g)


def _row(a):
    return a.reshape(1, -1).astype(F32)


def _pad_rows(w, start, total):
    return jnp.zeros((total, w.shape[1]), BF16).at[start:start + w.shape[0]].set(w.astype(BF16))


def kernel(x, norm1_g, w_in, rwkv_mu, rwkv_w0, rwkv_w2, rwkv_a0, rwkv_a2, rwkv_g2, rwkv_k_k, rwkv_k_a, rwkv_r_k, rwkv_ln_g, rwkv_ln_b, w_out, norm2_g, peer_wq, peer_sub_keys, peer_u, peer_v, final_g):
    batch, seq, d = x.shape
    tokens = batch * seq
    w = RWKV_WIDTH
    xt = x.reshape(tokens, d)

    w_in_p = jnp.pad(w_in[0].astype(BF16), ((0, 0), (0, IN_COLS_PAD - IN_COLS)))
    proj = _norm_matmul(xt, _row(norm1_g[0]), w_in_p, tm=1024, tn=512)
    proj = proj.reshape(batch, seq, IN_COLS_PAD)

    attn = _moba(proj, batch, seq, heads=4)

    mu = rwkv_mu[0]
    mu_l = jnp.pad(mu[3 * w:], (0, LORA_PAD - LORA_COLS))
    hid = lax.broadcasted_iota(jnp.int32, (w, w), 0) // RWKV_HEAD_DIM
    bd = (hid == hid.T).astype(BF16)
    prep_w = [_row(mu[:w]), _row(mu[w:2 * w]), _row(mu[2 * w:3 * w]), _row(mu_l),
              _row(rwkv_w0[0]), _pad_rows(rwkv_w2[0], 0, LORA_PAD),
              _row(rwkv_a0[0]), _pad_rows(rwkv_a2[0], DECAY_LORA, LORA_PAD),
              _pad_rows(rwkv_g2[0], DECAY_LORA + AAA_LORA, LORA_PAD),
              _row(rwkv_k_k[0]), _row(rwkv_k_a[0]), _row(rwkv_r_k[0]), bd]
    r, lw, k2, v, kk, b, g, bonus = _rwkv_prep(proj, batch, seq, prep_w, tt=256)
    y = _rwkv_scan(r, lw, k2, v, kk, b, tb=256)

    flat = lambda a: a.reshape(tokens, a.shape[-1])
    h, hn = _out_proj(xt, flat(attn), flat(y), flat(bonus), flat(g), _row(rwkv_ln_g[0]),
                      _row(rwkv_ln_b[0]), bd, w_out[0].astype(BF16), _row(norm2_g[0]), tm=256)

    keys = peer_sub_keys[0].reshape(2 * PEER_HEADS, PEER_NKEYS, PEER_HALF).astype(BF16)
    st = _peer_q(hn, peer_wq[0].astype(BF16), keys, tm=512)
    e1, e2, thr = _peer_topk(st, tl=512)
    yt = _peer_dense(peer_u[0].astype(BF16), hn.T, peer_v[0].T.astype(BF16), st, e1, e2, thr,
                     tt=512, n1=4)
    out = _final(h, yt.T, final_g.reshape(1, d), tm=512)
    return out.reshape(batch, seq, d)
```

```python
import itertools
import math

import jax
import jax.numpy as jnp
from jax import lax
from jax.experimental import pallas as pl
from jax.experimental.pallas import tpu as pltpu

F32 = jnp.float32
BF16 = jnp.bfloat16

D_MODEL = 2048
ATTN_HEADS = 8
ATTN_HEAD_DIM = 128
ATTN_WIDTH = ATTN_HEADS * ATTN_HEAD_DIM
MOBA_BLOCK = 256
MOBA_TOPK = 3

RWKV_HEAD_DIM = 64
RWKV_WIDTH = D_MODEL - ATTN_WIDTH
RWKV_HEADS = RWKV_WIDTH // RWKV_HEAD_DIM
DECAY_LORA = 96
AAA_LORA = 96
GATE_LORA = 256
LORA_COLS = DECAY_LORA + AAA_LORA + GATE_LORA
LORA_PAD = 512
RWKV_GN_EPS = RWKV_HEAD_DIM * 1e-5
IN_COLS = 3 * ATTN_WIDTH + 3 * RWKV_WIDTH + LORA_COLS
IN_COLS_PAD = 3 * ATTN_WIDTH + 3 * RWKV_WIDTH + LORA_PAD

PEER_HEADS = 8
PEER_NKEYS = 128
PEER_HALF = 128
PEER_TOPK = 16

RMS_EPS = 1e-6
NEG = -1e30

LANES = 128
SCAN_CHUNK = 64
SCAN_INV_BLOCK = 16
DOWN_ROWS = 2048
MXU_COLS = 256
DENSE_ROWS = 32
VMEM_LIMIT = 56 * 1024 * 1024

_NT = (((1,), (1,)), ((), ()))
_TN = (((0,), (0,)), ((), ()))


def _params(sem, flags=None):
    return pltpu.CompilerParams(dimension_semantics=sem, vmem_limit_bytes=VMEM_LIMIT, flags=flags)


def _dot(a, b):
    return jnp.dot(a.astype(BF16), b.astype(BF16), preferred_element_type=F32)


def _dot_nt(a, b):
    return lax.dot_general(a.astype(BF16), b.astype(BF16), _NT, preferred_element_type=F32)


def _dot_tn(a, b):
    return lax.dot_general(a.astype(BF16), b.astype(BF16), _TN, preferred_element_type=F32)


def _split3(x):
    hi = x.astype(BF16)
    r1 = x - hi.astype(F32)
    mid = r1.astype(BF16)
    lo = (r1 - mid.astype(F32)).astype(BF16)
    return hi, mid, lo


def _dot_exact_rhs(x, ones_bf16):
    hi, mid, lo = _split3(x)
    d = lambda p: jnp.dot(p, ones_bf16, preferred_element_type=F32)
    return d(hi) + d(mid) + d(lo)


def _dot_exact_lhs(ones_bf16, x):
    hi, mid, lo = _split3(x)
    d = lambda p: jnp.dot(ones_bf16, p, preferred_element_type=F32)
    return d(hi) + d(mid) + d(lo)


def _norm_matmul_kernel(x_ref, g_ref, w_ref, o_ref, xn_ref):
    @pl.when(pl.program_id(1) == 0)
    def _():
        x = x_ref[...]
        inv = lax.rsqrt(jnp.mean(x * x, axis=-1, keepdims=True) + RMS_EPS)
        xn_ref[...] = (x * inv * g_ref[...]).astype(BF16)

    o_ref[...] = jnp.dot(xn_ref[...], w_ref[...], preferred_element_type=F32)


def _norm_matmul(x, g, w, tm, tn):
    m, k = x.shape
    n = w.shape[1]
    return pl.pallas_call(
        _norm_matmul_kernel,
        grid=(m // tm, n // tn),
        in_specs=[pl.BlockSpec((tm, k), lambda i, j: (i, 0)),
                  pl.BlockSpec((1, k), lambda i, j: (0, 0)),
                  pl.BlockSpec((k, tn), lambda i, j: (0, j))],
        out_specs=pl.BlockSpec((tm, tn), lambda i, j: (i, j)),
        out_shape=jax.ShapeDtypeStruct((m, n), F32),
        scratch_shapes=[pltpu.VMEM((tm, k), BF16)],
        compiler_params=_params(("parallel", "arbitrary")),
        name="in_proj",
    )(x, g, w)


def _moba_kernel(q_ref, k_ref, v_ref, o_ref, kmean_ref, kb_ref, vt_ref, *, nb, heads):
    blk_len = MOBA_BLOCK
    hd = ATTN_HEAD_DIM
    hg = pl.program_id(1)
    qi = pl.program_id(2)
    hrange = range(heads)

    @pl.when(qi == 0)
    def _():
        for hh in hrange:
            hl = slice(hh * hd, (hh + 1) * hd)
            for j in range(nb):
                rows = slice(j * blk_len, (j + 1) * blk_len)
                kj = k_ref[0, rows, hl]
                kmean_ref[hh, j:j + 1, :] = jnp.mean(kj, axis=0, keepdims=True)
                kb_ref[hh, rows, :] = kj.astype(BF16)
                vt_ref[hh, :, rows] = v_ref[0, rows, hl].T.astype(BF16)

    log2e = math.log2(math.e)
    scale2 = (hd ** -0.5) * log2e
    kq = (lax.broadcasted_iota(jnp.int32, (blk_len, blk_len), 1)
          - lax.broadcasted_iota(jnp.int32, (blk_len, blk_len), 0))
    kqf = kq.astype(F32)
    blk = lax.broadcasted_iota(jnp.int32, (nb, blk_len), 0)
    past = blk < qi

    def select(hh):
        q = q_ref[0, :, hh * hd:(hh + 1) * hd]
        gate = lax.dot_general(kmean_ref[hh], q, _NT, precision=lax.Precision.HIGHEST,
                               preferred_element_type=F32)
        g = jnp.where(past, gate, NEG)
        selw = jnp.zeros(gate.shape, F32)
        for _ in range(min(MOBA_TOPK, nb)):
            m = jnp.max(g, axis=0, keepdims=True)
            first = jnp.min(jnp.where(g == m, blk, nb), axis=0, keepdims=True)
            pick = blk == first
            selw = jnp.where(pick, 1.0, selw)
            g = jnp.where(pick, -jnp.inf, g)
        return jnp.where(past, selw, 0.0), q.astype(BF16)

    sel_q = [select(hh) for hh in hrange]
    selw = [x[0] for x in sel_q]
    qb = [x[1] for x in sel_q]
    slope2 = [jnp.exp2(jnp.zeros((1, 1), F32) - (hg * heads + hh + 1).astype(F32)) * log2e
              for hh in hrange]
    bias0 = [slope2[hh] * kqf for hh in hrange]

    def scores(j):
        r0 = pl.multiple_of(j * blk_len, blk_len)
        return [lax.dot_general(kb_ref[hh, pl.ds(r0, blk_len), :], qb[hh], _NT,
                                preferred_element_type=F32) * scale2 - bias0[hh] for hh in hrange]

    def weighted_values(j, p):
        r0 = pl.multiple_of(j * blk_len, blk_len)
        return [jnp.dot(vt_ref[hh, :, pl.ds(r0, blk_len)], p[hh].astype(BF16),
                        preferred_element_type=F32) for hh in hrange]

    s = [jnp.where(kq >= 0, x, NEG) for x in scores(qi)]
    m = [jnp.max(x, axis=0, keepdims=True) for x in s]
    p = [jnp.exp2(s[hh] - m[hh]) for hh in hrange]
    l = [jnp.sum(x, axis=0, keepdims=True) for x in p]
    acc = weighted_values(qi, p)

    def body(j, carry):
        m, l, acc = carry
        off = ((qi - j) * blk_len).astype(F32)
        s = scores(j)
        selj = [jnp.max(jnp.where(blk == j, selw[hh], 0.0), axis=0, keepdims=True) for hh in hrange]
        s = [jnp.where(selj[hh] > 0.0, s[hh] - slope2[hh] * off, NEG) for hh in hrange]
        m_new = [jnp.maximum(m[hh], jnp.max(s[hh], axis=0, keepdims=True)) for hh in hrange]
        alpha = [jnp.exp2(m[hh] - m_new[hh]) for hh in hrange]
        p = [jnp.exp2(s[hh] - m_new[hh]) for hh in hrange]
        l = [alpha[hh] * l[hh] + jnp.sum(p[hh], axis=0, keepdims=True) for hh in hrange]
        pv = weighted_values(j, p)
        acc = [alpha[hh] * acc[hh] + pv[hh] for hh in hrange]
        return m_new, l, acc

    m, l, acc = lax.fori_loop(0, qi, body, (m, l, acc))
    for hh in hrange:
        o_ref[0, :, hh * hd:(hh + 1) * hd] = (acc[hh] / l[hh]).T


def _moba(proj, batch, seq, heads):
    nb = seq // MOBA_BLOCK
    hd = ATTN_HEAD_DIM
    hw = heads * hd
    groups = ATTN_HEADS // heads
    kern = lambda *refs: _moba_kernel(*refs, nb=nb, heads=heads)
    return pl.pallas_call(
        kern,
        grid=(batch, groups, nb),
        in_specs=[pl.BlockSpec((1, MOBA_BLOCK, hw), lambda b, h, i: (b, i, h)),
                  pl.BlockSpec((1, seq, hw), lambda b, h, i: (b, 0, groups + h)),
                  pl.BlockSpec((1, seq, hw), lambda b, h, i: (b, 0, 2 * groups + h))],
        out_specs=pl.BlockSpec((1, MOBA_BLOCK, hw), lambda b, h, i: (b, i, h)),
        out_shape=jax.ShapeDtypeStruct((batch, seq, ATTN_WIDTH), F32),
        scratch_shapes=[pltpu.VMEM((heads, nb, hd), F32),
                        pltpu.VMEM((heads, seq, hd), BF16),
                        pltpu.VMEM((heads, hd, seq), BF16)],
        compiler_params=_params(("parallel", "parallel", "arbitrary")),
        name="moba",
    )(proj, proj, proj)


def _rwkv_prep_kernel(r_ref, k_ref, v_ref, l_ref, rp_ref, kp_ref, vp_ref, lp_ref,
                      mu_r, mu_k, mu_v, mu_l, w0, w2p, a0, a2p, g2p, kk_w, ka_w, rk_w, bd_ref,
                      r_o, lw_o, k_o, v_o, kk_o, b_o, g_o, bonus_o):
    first = pl.program_id(1) == 0

    def shift(cur_ref, prev_ref, mu_ref):
        cur = cur_ref[0]
        prev_last = jnp.where(first, 0.0, prev_ref[0, 7:8, :])
        rows = lax.broadcasted_iota(jnp.int32, cur.shape, 0)
        prev = jnp.where(rows == 0, prev_last, pltpu.roll(cur, 1, axis=0))
        return cur + (prev - cur) * mu_ref[...]

    r = shift(r_ref, rp_ref, mu_r)
    k = shift(k_ref, kp_ref, mu_k)
    v = shift(v_ref, vp_ref, mu_v)
    lo = shift(l_ref, lp_ref, mu_l)

    bd = bd_ref[...]
    z = -(w0[...] + _dot(jnp.tanh(lo), w2p[...]))
    softplus = jnp.maximum(z, 0.0) + jnp.log1p(jnp.exp(-jnp.abs(z)))
    lw = -jnp.exp(-softplus - 0.5)
    a = jax.nn.sigmoid(a0[...] + _dot(lo, a2p[...]))
    g = _dot(jax.nn.sigmoid(lo), g2p[...])
    kk = k * kk_w[...]
    kk = kk / jnp.maximum(jnp.sqrt(_dot_exact_rhs(kk * kk, bd)), 1e-12)
    k2 = k * (1.0 + (a - 1.0) * ka_w[...])
    r_o[0] = r
    lw_o[0] = lw
    k_o[0] = k2
    v_o[0] = v
    kk_o[0] = kk
    b_o[0] = kk * a
    g_o[0] = g
    bonus_o[0] = _dot_exact_rhs(r * k2 * rk_w[...], bd) * v


def _rwkv_prep(proj, batch, seq, weights, tt):
    w = RWKV_WIDTH
    c0 = 3 * ATTN_WIDTH // w
    lc = (3 * ATTN_WIDTH + 3 * w) // LORA_PAD
    pb = tt // 8
    cur = lambda width, col: pl.BlockSpec((1, tt, width), lambda b, i: (b, i, col))
    prev = lambda width, col: pl.BlockSpec(
        (1, 8, width), lambda b, i: (b, jnp.maximum(i * pb - 1, 0), col))
    full = lambda a: pl.BlockSpec(a.shape, lambda b, i: (0,) * a.ndim)
    out = pl.BlockSpec((1, tt, w), lambda b, i: (b, i, 0))
    return pl.pallas_call(
        _rwkv_prep_kernel,
        grid=(batch, seq // tt),
        in_specs=[cur(w, c0), cur(w, c0 + 1), cur(w, c0 + 2), cur(LORA_PAD, lc),
                  prev(w, c0), prev(w, c0 + 1), prev(w, c0 + 2), prev(LORA_PAD, lc)]
                 + [full(a) for a in weights],
        out_specs=[out] * 8,
        out_shape=[jax.ShapeDtypeStruct((batch, seq, w), F32)] * 8,
        compiler_params=_params(("parallel", "arbitrary")),
        name="rwkv_prep",
    )(proj, proj, proj, proj, proj, proj, proj, proj, *weights)


def _scan_kernel(r_ref, lw_ref, k_ref, v_ref, kk_ref, b_ref, y_ref, s_ref, *, tb, pairs):
    c = SCAN_CHUNK
    c2 = 2 * c

    @pl.when(pl.program_id(1) == 0)
    def _():
        s_ref[...] = jnp.zeros_like(s_ref)

    head0 = lax.broadcasted_iota(jnp.int32, (c, LANES), 1) < RWKV_HEAD_DIM
    row = lax.broadcasted_iota(jnp.int32, (c2, c2), 0)
    col = lax.broadcasted_iota(jnp.int32, (c2, c2), 1)
    same_head = (row // c) == (col // c)
    strict = jnp.logical_and(same_head, col < row)
    incl = jnp.logical_and(same_head, col <= row)
    inv_blk = (row // SCAN_INV_BLOCK) == (col // SCAN_INV_BLOCK)
    eye = jnp.where(row == col, 1.0, 0.0).astype(F32)
    tri = jnp.where(lax.broadcasted_iota(jnp.int32, (c, c), 0)
                    >= lax.broadcasted_iota(jnp.int32, (c, c), 1), 1.0, 0.0).astype(BF16)

    def stack(x):
        return jnp.concatenate([jnp.where(head0, x, 0.0), jnp.where(head0, 0.0, x)], axis=0)

    def pair_chunk(t0, p):
        lanes = slice(p * LANES, (p + 1) * LANES)
        rows = pl.ds(t0, c)
        lw = lw_ref[0, rows, lanes]
        cum = _dot_exact_lhs(tri, lw)
        yield
        r = r_ref[0, rows, lanes]
        k = k_ref[0, rows, lanes]
        kk = kk_ref[0, rows, lanes]
        b = b_ref[0, rows, lanes]
        cum_end = cum[c - 1:c, :]
        e_neg = jnp.exp(-cum)
        e_end = jnp.exp(cum_end - cum)
        a_s = stack(-kk * jnp.exp(cum - lw))
        r_s = stack(r * jnp.exp(cum))
        b_s = stack(b * e_neg)
        k_s = stack(k * e_neg)
        bend_s = stack(b * e_end)
        kend_s = stack(k * e_end)
        v_s = stack(v_ref[0, rows, lanes])
        gram = _dot_nt(jnp.concatenate([a_s, r_s], axis=0),
                       jnp.concatenate([b_s, k_s], axis=0))
        kv = _dot_tn(v_s, kend_s)
        yield
        l_ab = jnp.where(strict, gram[:c2, :c2], 0.0)
        l_ak = jnp.where(strict, gram[:c2, c2:], 0.0)
        l_rb = jnp.where(incl, gram[c2:, :c2], 0.0)
        l_rk = jnp.where(incl, gram[c2:, c2:], 0.0)

        x = jnp.where(inv_blk, l_ab, 0.0)
        off = jnp.where(inv_blk, 0.0, l_ab)
        dinv = eye + x
        w1 = _dot(l_ak, v_s)
        y1 = _dot(l_rk, v_s)
        for _ in range(int(math.log2(SCAN_INV_BLOCK)) - 1):
            x = _dot(x, x)
            yield
            dinv = dinv + _dot(dinv, x)
            yield
        f = _dot(dinv, off)
        yield
        f2 = _dot(f, f)
        yield
        gm = eye + f
        gm = gm + _dot(gm, f2)
        yield
        t_inv = _dot(gm, dinv)
        yield
        ta = _dot(t_inv, jnp.concatenate([a_s, w1], axis=1))
        yield
        a_hat = ta[:, :LANES]
        v_hat = ta[:, LANES:]
        state = s_ref[p]
        m1 = _dot_nt(jnp.concatenate([a_hat, r_s], axis=0), state)
        yield
        u = m1[:c2] + v_hat
        y = m1[c2:] + y1 + _dot(l_rb, u)
        s_ref[p] = state * jnp.exp(cum_end) + kv + _dot_tn(u, bend_s)
        y_ref[0, rows, lanes] = y[:c] + y[c:]

    def chunk(ci, carry):
        t0 = pl.multiple_of(ci * c, c)
        for _ in itertools.zip_longest(*[pair_chunk(t0, p) for p in range(pairs)]):
            pass
        return carry

    lax.fori_loop(0, tb // c, chunk, 0)


def _rwkv_scan(r, lw, k, v, kk, b, tb):
    batch, seq, w = r.shape
    pairs = w // LANES
    spec = pl.BlockSpec((1, tb, w), lambda bi, t: (bi, t, 0))
    kern = lambda *refs: _scan_kernel(*refs, tb=tb, pairs=pairs)
    return pl.pallas_call(
        kern,
        grid=(batch, seq // tb),
        in_specs=[spec] * 6,
        out_specs=spec,
        out_shape=jax.ShapeDtypeStruct((batch, seq, w), F32),
        scratch_shapes=[pltpu.VMEM((pairs, LANES, LANES), F32)],
        compiler_params=_params(("parallel", "arbitrary")),
        name="rwkv_scan",
    )(r, lw, k, v, kk, b)


def _out_proj_kernel(x_ref, attn_ref, y_ref, bonus_ref, g_ref, lng, lnb, bd_ref, w_ref, n2g,
                     h_ref, hn_ref):
    bd = bd_ref[...]
    y = y_ref[...]
    inv_n = 1.0 / RWKV_HEAD_DIM
    d = y - _dot_exact_rhs(y, bd) * inv_n
    var = _dot_exact_rhs(d * d, bd) * inv_n
    yn = d * lax.rsqrt(var + RWKV_GN_EPS) * lng[...] + lnb[...]
    rw = (yn + bonus_ref[...]) * g_ref[...]
    h = (x_ref[...] + _dot(attn_ref[...], w_ref[:ATTN_WIDTH, :])
         + _dot(rw, w_ref[ATTN_WIDTH:, :]))
    h_ref[...] = h
    inv = lax.rsqrt(jnp.mean(h * h, axis=-1, keepdims=True) + RMS_EPS)
    hn_ref[...] = (h * inv * n2g[...]).astype(BF16)


def _out_proj(x, attn, y, bonus, g, lng, lnb, bd, w_out, n2g, tm):
    m = x.shape[0]
    row = lambda width: pl.BlockSpec((tm, width), lambda i: (i, 0))
    full = lambda a: pl.BlockSpec(a.shape, lambda i: (0,) * a.ndim)
    return pl.pallas_call(
        _out_proj_kernel,
        grid=(m // tm,),
        in_specs=[row(D_MODEL), row(ATTN_WIDTH), row(RWKV_WIDTH), row(RWKV_WIDTH), row(RWKV_WIDTH),
                  full(lng), full(lnb), full(bd), full(w_out), full(n2g)],
        out_specs=[row(D_MODEL), row(D_MODEL)],
        out_shape=[jax.ShapeDtypeStruct((m, D_MODEL), F32),
                   jax.ShapeDtypeStruct((m, D_MODEL), BF16)],
        compiler_params=_params(("parallel",)),
        name="out_proj",
    )(x, attn, y, bonus, g, lng, lnb, bd, w_out, n2g)


def _peer_q_kernel(hn_ref, wq_ref, keys_ref, st_ref):
    q = jnp.dot(hn_ref[...], wq_ref[...], preferred_element_type=F32)
    for hp in range(2 * PEER_HEADS):
        qs = q[:, hp * PEER_HALF:(hp + 1) * PEER_HALF]
        st_ref[hp] = _dot_nt(keys_ref[hp], qs)


def _peer_q(hn, wq, keys, tm):
    m = hn.shape[0]
    return pl.pallas_call(
        _peer_q_kernel,
        grid=(m // tm,),
        in_specs=[pl.BlockSpec((tm, D_MODEL), lambda i: (i, 0)),
                  pl.BlockSpec(wq.shape, lambda i: (0, 0)),
                  pl.BlockSpec(keys.shape, lambda i: (0, 0, 0))],
        out_specs=pl.BlockSpec((2 * PEER_HEADS, PEER_NKEYS, tm), lambda i: (0, 0, i)),
        out_shape=jax.ShapeDtypeStruct((2 * PEER_HEADS, PEER_NKEYS, m), F32),
        compiler_params=_params(("parallel",)),
        name="peer_q",
    )(hn, wq, keys)


def _top_desc(x, k):
    n = x.shape[0]
    rows = lax.broadcasted_iota(jnp.int32, x.shape, 0)
    rank = jnp.full(x.shape, float(k), F32)
    outs = []
    for i in range(k):
        m = jnp.max(x, axis=0, keepdims=True)
        outs.append(m)
        pick = rows == jnp.min(jnp.where(x == m, rows, n), axis=0, keepdims=True)
        x = jnp.where(pick, -jnp.inf, x)
        rank = jnp.where(pick, float(i), rank)
    return jnp.concatenate(outs, axis=0), rank


def _peer_topk_kernel(s_ref, cnt1_ref, e1_ref, rank2_ref, e2_ref):
    k = PEER_TOPK
    s1 = s_ref[0]
    s2 = s_ref[1]
    sv1, rank1 = _top_desc(s1, k)
    sv2, rank2 = _top_desc(s2, k)
    widths = [k // (a + 1) for a in range(k)]
    pad = -sum(widths) % 8
    cand = jnp.concatenate([sv1[a:a + 1] + sv2[:widths[a]] for a in range(k)]
                           + [jnp.full((pad, s1.shape[1]), -jnp.inf, F32)], axis=0)
    best, pos = _top_desc(cand, k)
    chosen = jnp.where(pos < float(k), 1.0, 0.0)
    z = jnp.sum(jnp.exp(best - best[0:1]), axis=0, keepdims=True)
    cnt1 = jnp.zeros(s1.shape, F32)
    row = 0
    for a in range(k):
        cnt_a = jnp.sum(chosen[row:row + widths[a]], axis=0, keepdims=True)
        cnt1 = jnp.where(rank1 == float(a), cnt_a, cnt1)
        row += widths[a]
    cnt1_ref[0] = cnt1
    e1_ref[0] = jnp.exp(s1 - sv1[0:1]) / z
    rank2_ref[0] = rank2
    e2_ref[0] = jnp.exp(s2 - sv2[0:1])


def _peer_topk(st, tl):
    m = st.shape[2]
    spec = pl.BlockSpec((1, PEER_NKEYS, tl), lambda h, i: (h, 0, i))
    shape = lambda dt: jax.ShapeDtypeStruct((PEER_HEADS, PEER_NKEYS, m), dt)
    return pl.pallas_call(
        _peer_topk_kernel,
        grid=(PEER_HEADS, m // tl),
        in_specs=[pl.BlockSpec((2, PEER_NKEYS, tl), lambda h, i: (h, 0, i))],
        out_specs=[spec] * 4,
        out_shape=[shape(F32)] * 4,
        compiler_params=_params(("parallel", "parallel")),
        name="peer_topk",
    )(st)


def _peer_dense_kernel(u_ref, hnt_ref, vt_ref, rank2_ref, e2_ref, cnta_ref, cntb_ref,
                       e1a_ref, e1b_ref, yt_ref, act_ref, hpre_ref, *, n1):
    step = pl.program_id(1)
    et = hpre_ref.shape[1]
    row_refs = ((cnta_ref, e1a_ref), (cntb_ref, e1b_ref))

    @pl.when(step == 0)
    def _():
        yt_ref[...] = jnp.zeros_like(yt_ref)
        hpre_ref[1] = jnp.zeros(hpre_ref.shape[1:], hpre_ref.dtype)
        act_ref[0] = jnp.zeros(act_ref.shape[1:], act_ref.dtype)

    def gate_tile(half, jt, ib):
        prev = 1 - half
        cnt_ref, e1r_ref = row_refs[half]
        ls = slice(jt * LANES, (jt + 1) * LANES)
        rs = slice(ib * DENSE_ROWS, (ib + 1) * DENSE_ROWS)
        gates = [jnp.zeros((DENSE_ROWS, LANES), F32) for _ in range(n1)]
        for h in range(PEER_HEADS):
            r2 = rank2_ref[h, rs, ls]
            e2 = e2_ref[h, rs, ls]
            for a in range(n1):
                r = h * n1 + a
                hit = r2 < cnt_ref[0, r:r + 1, ls]
                gates[a] = gates[a] + jnp.where(hit, e2, 0.0) * e1r_ref[0, r:r + 1, ls]
        for a in range(n1):
            ers = slice(a * PEER_NKEYS + ib * DENSE_ROWS, a * PEER_NKEYS + (ib + 1) * DENSE_ROWS)
            x = hpre_ref[prev, ers, ls]
            gelu = 0.5 * x * (1.0 + lax.erf(x * (0.5 ** 0.5)))
            act_ref[prev, ers, ls] = gelu * gates[a]

    tt = hpre_ref.shape[2]
    n_ib = PEER_NKEYS // DENSE_ROWS
    for half in range(2):
        cur = half
        hs = slice(half * et, (half + 1) * et)
        for piece in range(tt // MXU_COLS):
            cs = slice(piece * MXU_COLS, (piece + 1) * MXU_COLS)
            tiles = [(jt, ib) for jt in range(piece * MXU_COLS // LANES, (piece + 1) * MXU_COLS // LANES)
                     for ib in range(n_ib)]
            hpre_ref[cur, :, cs] = jnp.dot(u_ref[hs, :], hnt_ref[:, cs],
                                           preferred_element_type=F32)
            per_down = len(tiles) * DOWN_ROWS // D_MODEL
            for i, tile in enumerate(tiles):
                if i % per_down == per_down // 2:
                    ms = slice(i // per_down * DOWN_ROWS, (i // per_down + 1) * DOWN_ROWS)
                    yt_ref[ms, cs] += jnp.dot(vt_ref[ms, hs], act_ref[cur, :, cs].astype(BF16),
                                              preferred_element_type=F32)
                gate_tile(half, *tile)


def _peer_dense(u, hnt, vt, cnt1, e1, rank2, e2, tt, n1):
    m = hnt.shape[1]
    et = n1 * PEER_NKEYS
    n_steps = PEER_NKEYS // (2 * n1)
    n_tiles = 2 * n_steps
    tok3 = lambda a: pl.BlockSpec((a.shape[0], a.shape[1], tt), lambda i, e: (0, 0, i))
    by_tile = lambda a: a.reshape(PEER_HEADS, n_tiles, n1, m).transpose(1, 0, 2, 3).reshape(
        n_tiles, PEER_HEADS * n1, m)
    cntr = by_tile(cnt1)
    e1r = by_tile(e1)
    rows = lambda half: pl.BlockSpec(
        (1, PEER_HEADS * n1, tt), lambda i, e: (jnp.clip(2 * e + half - 1, 0, n_tiles - 1), 0, i))
    kern = lambda *refs: _peer_dense_kernel(*refs, n1=n1)
    return pl.pallas_call(
        kern,
        grid=(m // tt, n_steps + 1),
        in_specs=[pl.BlockSpec((2 * et, D_MODEL), lambda i, e: (jnp.minimum(e, n_steps - 1), 0)),
                  pl.BlockSpec((D_MODEL, tt), lambda i, e: (0, i)),
                  pl.BlockSpec((D_MODEL, 2 * et), lambda i, e: (0, jnp.maximum(e - 1, 0))),
                  tok3(rank2), tok3(e2), rows(0), rows(1), rows(0), rows(1)],
        out_specs=pl.BlockSpec((D_MODEL, tt), lambda i, e: (0, i)),
        out_shape=jax.ShapeDtypeStruct((D_MODEL, m), F32),
        scratch_shapes=[pltpu.VMEM((2, et, tt), F32), pltpu.VMEM((2, et, tt), F32)],
        compiler_params=_params(("parallel", "arbitrary")),
        name="peer_dense",
    )(u, hnt, vt, rank2, e2, cntr, cntr, e1r, e1r)


def _final_kernel(h_ref, y_ref, g_ref, o_ref):
    h = h_ref[...] + y_ref[...]
    inv = lax.rsqrt(jnp.mean(h * h, axis=-1, keepdims=True) + RMS_EPS)
    o_ref[...] = h * inv * g_ref[...]


def _final(h, y, g, tm):
    m = h.shape[0]
    row = pl.BlockSpec((tm, D_MODEL), lambda i: (i, 0))
    return pl.pallas_call(
        _final_kernel,
        grid=(m // tm,),
        in_specs=[row, row, pl.BlockSpec((1, D_MODEL), lambda i: (0, 0))],
        out_specs=row,
        out_shape=jax.ShapeDtypeStruct((m, D_MODEL), F32),
        compiler_params=_params(("parallel",)),
        name="final_norm",
    )(h, y, g)


def _row(a):
    return a.reshape(1, -1).astype(F32)


def _pad_rows(w, start, total):
    return jnp.zeros((total, w.shape[1]), BF16).at[start:start + w.shape[0]].set(w.astype(BF16))


def kernel(x, norm1_g, w_in, rwkv_mu, rwkv_w0, rwkv_w2, rwkv_a0, rwkv_a2, rwkv_g2, rwkv_k_k, rwkv_k_a, rwkv_r_k, rwkv_ln_g, rwkv_ln_b, w_out, norm2_g, peer_wq, peer_sub_keys, peer_u, peer_v, final_g):
    batch, seq, d = x.shape
    tokens = batch * seq
    w = RWKV_WIDTH
    xt = x.reshape(tokens, d)

    w_in_p = jnp.pad(w_in[0].astype(BF16), ((0, 0), (0, IN_COLS_PAD - IN_COLS)))
    proj = _norm_matmul(xt, _row(norm1_g[0]), w_in_p, tm=1024, tn=512)
    proj = proj.reshape(batch, seq, IN_COLS_PAD)

    attn = _moba(proj, batch, seq, heads=4)

    mu = rwkv_mu[0]
    mu_l = jnp.pad(mu[3 * w:], (0, LORA_PAD - LORA_COLS))
    hid = lax.broadcasted_iota(jnp.int32, (w, w), 0) // RWKV_HEAD_DIM
    bd = (hid == hid.T).astype(BF16)
    prep_w = [_row(mu[:w]), _row(mu[w:2 * w]), _row(mu[2 * w:3 * w]), _row(mu_l),
              _row(rwkv_w0[0]), _pad_rows(rwkv_w2[0], 0, LORA_PAD),
              _row(rwkv_a0[0]), _pad_rows(rwkv_a2[0], DECAY_LORA, LORA_PAD),
              _pad_rows(rwkv_g2[0], DECAY_LORA + AAA_LORA, LORA_PAD),
              _row(rwkv_k_k[0]), _row(rwkv_k_a[0]), _row(rwkv_r_k[0]), bd]
    r, lw, k2, v, kk, b, g, bonus = _rwkv_prep(proj, batch, seq, prep_w, tt=256)
    y = _rwkv_scan(r, lw, k2, v, kk, b, tb=256)

    flat = lambda a: a.reshape(tokens, a.shape[-1])
    h, hn = _out_proj(xt, flat(attn), flat(y), flat(bonus), flat(g), _row(rwkv_ln_g[0]),
                      _row(rwkv_ln_b[0]), bd, w_out[0].astype(BF16), _row(norm2_g[0]), tm=256)

    keys = peer_sub_keys[0].reshape(2 * PEER_HEADS, PEER_NKEYS, PEER_HALF).astype(BF16)
    st = _peer_q(hn, peer_wq[0].astype(BF16), keys, tm=512)
    cnt1, e1, rank2, e2 = _peer_topk(st, tl=512)
    yt = _peer_dense(peer_u[0].astype(BF16), hn.T, peer_v[0].T.astype(BF16), cnt1, e1, rank2, e2,
                     tt=512, n1=4)
    out = _final(h, yt.T, final_g.reshape(1, d), tm=512)
    return out.reshape(batch, seq, d)
```

```python
import itertools
import math

import jax
import jax.numpy as jnp
from jax import lax
from jax.experimental import pallas as pl
from jax.experimental.pallas import tpu as pltpu

F32 = jnp.float32
BF16 = jnp.bfloat16

D_MODEL = 2048
ATTN_HEADS = 8
ATTN_HEAD_DIM = 128
ATTN_WIDTH = ATTN_HEADS * ATTN_HEAD_DIM
MOBA_BLOCK = 256
MOBA_TOPK = 3

RWKV_HEAD_DIM = 64
RWKV_WIDTH = D_MODEL - ATTN_WIDTH
RWKV_HEADS = RWKV_WIDTH // RWKV_HEAD_DIM
DECAY_LORA = 96
AAA_LORA = 96
GATE_LORA = 256
LORA_COLS = DECAY_LORA + AAA_LORA + GATE_LORA
LORA_PAD = 512
RWKV_GN_EPS = RWKV_HEAD_DIM * 1e-5
IN_COLS = 3 * ATTN_WIDTH + 3 * RWKV_WIDTH + LORA_COLS
IN_COLS_PAD = 3 * ATTN_WIDTH + 3 * RWKV_WIDTH + LORA_PAD

PEER_HEADS = 8
PEER_NKEYS = 128
PEER_HALF = 128
PEER_TOPK = 16

RMS_EPS = 1e-6
NEG = -1e30

LANES = 128
SCAN_CHUNK = 64
SCAN_INV_BLOCK = 16
DOWN_ROWS = 2048
MXU_COLS = 256
DENSE_ROWS = 32
VMEM_LIMIT = 56 * 1024 * 1024

_NT = (((1,), (1,)), ((), ()))
_TN = (((0,), (0,)), ((), ()))


def _params(sem, flags=None):
    return pltpu.CompilerParams(dimension_semantics=sem, vmem_limit_bytes=VMEM_LIMIT, flags=flags)


def _dot(a, b):
    return jnp.dot(a.astype(BF16), b.astype(BF16), preferred_element_type=F32)


def _dot_nt(a, b):
    return lax.dot_general(a.astype(BF16), b.astype(BF16), _NT, preferred_element_type=F32)


def _dot_tn(a, b):
    return lax.dot_general(a.astype(BF16), b.astype(BF16), _TN, preferred_element_type=F32)


def _split3(x):
    hi = x.astype(BF16)
    r1 = x - hi.astype(F32)
    mid = r1.astype(BF16)
    lo = (r1 - mid.astype(F32)).astype(BF16)
    return hi, mid, lo


def _dot_exact_rhs(x, ones_bf16):
    hi, mid, lo = _split3(x)
    d = lambda p: jnp.dot(p, ones_bf16, preferred_element_type=F32)
    return d(hi) + d(mid) + d(lo)


def _dot_exact_lhs(ones_bf16, x):
    hi, mid, lo = _split3(x)
    d = lambda p: jnp.dot(ones_bf16, p, preferred_element_type=F32)
    return d(hi) + d(mid) + d(lo)


def _norm_matmul_kernel(x_ref, g_ref, w_ref, o_ref, xn_ref):
    @pl.when(pl.program_id(1) == 0)
    def _():
        x = x_ref[...]
        inv = lax.rsqrt(jnp.mean(x * x, axis=-1, keepdims=True) + RMS_EPS)
        xn_ref[...] = (x * inv * g_ref[...]).astype(BF16)

    o_ref[...] = jnp.dot(xn_ref[...], w_ref[...], preferred_element_type=F32)


def _norm_matmul(x, g, w, tm, tn):
    m, k = x.shape
    n = w.shape[1]
    return pl.pallas_call(
        _norm_matmul_kernel,
        grid=(m // tm, n // tn),
        in_specs=[pl.BlockSpec((tm, k), lambda i, j: (i, 0)),
                  pl.BlockSpec((1, k), lambda i, j: (0, 0)),
                  pl.BlockSpec((k, tn), lambda i, j: (0, j))],
        out_specs=pl.BlockSpec((tm, tn), lambda i, j: (i, j)),
        out_shape=jax.ShapeDtypeStruct((m, n), F32),
        scratch_shapes=[pltpu.VMEM((tm, k), BF16)],
        compiler_params=_params(("parallel", "arbitrary")),
        name="in_proj",
    )(x, g, w)


def _moba_kernel(q_ref, k_ref, v_ref, o_ref, kmean_ref, kb_ref, vt_ref, *, nb, heads):
    blk_len = MOBA_BLOCK
    hd = ATTN_HEAD_DIM
    hg = pl.program_id(1)
    qi = pl.program_id(2)
    hrange = range(heads)

    @pl.when(qi == 0)
    def _():
        for hh in hrange:
            hl = slice(hh * hd, (hh + 1) * hd)
            for j in range(nb):
                rows = slice(j * blk_len, (j + 1) * blk_len)
                kj = k_ref[0, rows, hl]
                kmean_ref[hh, j:j + 1, :] = jnp.mean(kj, axis=0, keepdims=True)
                kb_ref[hh, rows, :] = kj.astype(BF16)
                vt_ref[hh, :, rows] = v_ref[0, rows, hl].T.astype(BF16)

    log2e = math.log2(math.e)
    scale2 = (hd ** -0.5) * log2e
    kq = (lax.broadcasted_iota(jnp.int32, (blk_len, blk_len), 1)
          - lax.broadcasted_iota(jnp.int32, (blk_len, blk_len), 0))
    kqf = kq.astype(F32)
    blk = lax.broadcasted_iota(jnp.int32, (nb, blk_len), 0)
    past = blk < qi

    def select(hh):
        q = q_ref[0, :, hh * hd:(hh + 1) * hd]
        gate = lax.dot_general(kmean_ref[hh], q, _NT, precision=lax.Precision.HIGHEST,
                               preferred_element_type=F32)
        g = jnp.where(past, gate, NEG)
        selw = jnp.zeros(gate.shape, F32)
        for _ in range(min(MOBA_TOPK, nb)):
            m = jnp.max(g, axis=0, keepdims=True)
            first = jnp.min(jnp.where(g == m, blk, nb), axis=0, keepdims=True)
            pick = blk == first
            selw = jnp.where(pick, 1.0, selw)
            g = jnp.where(pick, -jnp.inf, g)
        return jnp.where(past, selw, 0.0), q.astype(BF16)

    sel_q = [select(hh) for hh in hrange]
    selw = [x[0] for x in sel_q]
    qb = [x[1] for x in sel_q]
    slope2 = [jnp.exp2(jnp.zeros((1, 1), F32) - (hg * heads + hh + 1).astype(F32)) * log2e
              for hh in hrange]
    bias0 = [slope2[hh] * kqf for hh in hrange]

    def scores(j):
        r0 = pl.multiple_of(j * blk_len, blk_len)
        return [lax.dot_general(kb_ref[hh, pl.ds(r0, blk_len), :], qb[hh], _NT,
                                preferred_element_type=F32) * scale2 - bias0[hh] for hh in hrange]

    def weighted_values(j, p):
        r0 = pl.multiple_of(j * blk_len, blk_len)
        return [jnp.dot(vt_ref[hh, :, pl.ds(r0, blk_len)], p[hh].astype(BF16),
                        preferred_element_type=F32) for hh in hrange]

    s = [jnp.where(kq >= 0, x, NEG) for x in scores(qi)]
    m = [jnp.max(x, axis=0, keepdims=True) for x in s]
    p = [jnp.exp2(s[hh] - m[hh]) for hh in hrange]
    l = [jnp.sum(x, axis=0, keepdims=True) for x in p]
    acc = weighted_values(qi, p)

    def body(j, carry):
        m, l, acc = carry
        off = ((qi - j) * blk_len).astype(F32)
        s = scores(j)
        selj = [jnp.max(jnp.where(blk == j, selw[hh], 0.0), axis=0, keepdims=True) for hh in hrange]
        s = [jnp.where(selj[hh] > 0.0, s[hh] - slope2[hh] * off, NEG) for hh in hrange]
        m_new = [jnp.maximum(m[hh], jnp.max(s[hh], axis=0, keepdims=True)) for hh in hrange]
        alpha = [jnp.exp2(m[hh] - m_new[hh]) for hh in hrange]
        p = [jnp.exp2(s[hh] - m_new[hh]) for hh in hrange]
        l = [alpha[hh] * l[hh] + jnp.sum(p[hh], axis=0, keepdims=True) for hh in hrange]
        pv = weighted_values(j, p)
        acc = [alpha[hh] * acc[hh] + pv[hh] for hh in hrange]
        return m_new, l, acc

    m, l, acc = lax.fori_loop(0, qi, body, (m, l, acc))
    for hh in hrange:
        o_ref[0, :, hh * hd:(hh + 1) * hd] = (acc[hh] / l[hh]).T


def _moba(proj, batch, seq, heads):
    nb = seq // MOBA_BLOCK
    hd = ATTN_HEAD_DIM
    hw = heads * hd
    groups = ATTN_HEADS // heads
    kern = lambda *refs: _moba_kernel(*refs, nb=nb, heads=heads)
    return pl.pallas_call(
        kern,
        grid=(batch, groups, nb),
        in_specs=[pl.BlockSpec((1, MOBA_BLOCK, hw), lambda b, h, i: (b, i, h)),
                  pl.BlockSpec((1, seq, hw), lambda b, h, i: (b, 0, groups + h)),
                  pl.BlockSpec((1, seq, hw), lambda b, h, i: (b, 0, 2 * groups + h))],
        out_specs=pl.BlockSpec((1, MOBA_BLOCK, hw), lambda b, h, i: (b, i, h)),
        out_shape=jax.ShapeDtypeStruct((batch, seq, ATTN_WIDTH), F32),
        scratch_shapes=[pltpu.VMEM((heads, nb, hd), F32),
                        pltpu.VMEM((heads, seq, hd), BF16),
                        pltpu.VMEM((heads, hd, seq), BF16)],
        compiler_params=_params(("parallel", "parallel", "arbitrary")),
        name="moba",
    )(proj, proj, proj)


def _rwkv_prep_kernel(r_ref, k_ref, v_ref, l_ref, rp_ref, kp_ref, vp_ref, lp_ref,
                      mu_r, mu_k, mu_v, mu_l, w0, w2p, a0, a2p, g2p, kk_w, ka_w, rk_w, bd_ref,
                      r_o, lw_o, k_o, v_o, kk_o, b_o, g_o, bonus_o):
    first = pl.program_id(1) == 0

    def shift(cur_ref, prev_ref, mu_ref):
        cur = cur_ref[0]
        prev_last = jnp.where(first, 0.0, prev_ref[0, 7:8, :])
        rows = lax.broadcasted_iota(jnp.int32, cur.shape, 0)
        prev = jnp.where(rows == 0, prev_last, pltpu.roll(cur, 1, axis=0))
        return cur + (prev - cur) * mu_ref[...]

    r = shift(r_ref, rp_ref, mu_r)
    k = shift(k_ref, kp_ref, mu_k)
    v = shift(v_ref, vp_ref, mu_v)
    lo = shift(l_ref, lp_ref, mu_l)

    bd = bd_ref[...]
    z = -(w0[...] + _dot(jnp.tanh(lo), w2p[...]))
    softplus = jnp.maximum(z, 0.0) + jnp.log1p(jnp.exp(-jnp.abs(z)))
    lw = -jnp.exp(-softplus - 0.5)
    a = jax.nn.sigmoid(a0[...] + _dot(lo, a2p[...]))
    g = _dot(jax.nn.sigmoid(lo), g2p[...])
    kk = k * kk_w[...]
    kk = kk / jnp.maximum(jnp.sqrt(_dot_exact_rhs(kk * kk, bd)), 1e-12)
    k2 = k * (1.0 + (a - 1.0) * ka_w[...])
    r_o[0] = r
    lw_o[0] = lw
    k_o[0] = k2
    v_o[0] = v
    kk_o[0] = kk
    b_o[0] = kk * a
    g_o[0] = g
    bonus_o[0] = _dot_exact_rhs(r * k2 * rk_w[...], bd) * v


def _rwkv_prep(proj, batch, seq, weights, tt):
    w = RWKV_WIDTH
    c0 = 3 * ATTN_WIDTH // w
    lc = (3 * ATTN_WIDTH + 3 * w) // LORA_PAD
    pb = tt // 8
    cur = lambda width, col: pl.BlockSpec((1, tt, width), lambda b, i: (b, i, col))
    prev = lambda width, col: pl.BlockSpec(
        (1, 8, width), lambda b, i: (b, jnp.maximum(i * pb - 1, 0), col))
    full = lambda a: pl.BlockSpec(a.shape, lambda b, i: (0,) * a.ndim)
    out = pl.BlockSpec((1, tt, w), lambda b, i: (b, i, 0))
    return pl.pallas_call(
        _rwkv_prep_kernel,
        grid=(batch, seq // tt),
        in_specs=[cur(w, c0), cur(w, c0 + 1), cur(w, c0 + 2), cur(LORA_PAD, lc),
                  prev(w, c0), prev(w, c0 + 1), prev(w, c0 + 2), prev(LORA_PAD, lc)]
                 + [full(a) for a in weights],
        out_specs=[out] * 8,
        out_shape=[jax.ShapeDtypeStruct((batch, seq, w), F32)] * 8,
        compiler_params=_params(("parallel", "arbitrary")),
        name="rwkv_prep",
    )(proj, proj, proj, proj, proj, proj, proj, proj, *weights)


def _scan_kernel(r_ref, lw_ref, k_ref, v_ref, kk_ref, b_ref, y_ref, s_ref, *, tb, pairs):
    c = SCAN_CHUNK
    c2 = 2 * c

    @pl.when(pl.program_id(1) == 0)
    def _():
        s_ref[...] = jnp.zeros_like(s_ref)

    head0 = lax.broadcasted_iota(jnp.int32, (c, LANES), 1) < RWKV_HEAD_DIM
    row = lax.broadcasted_iota(jnp.int32, (c2, c2), 0)
    col = lax.broadcasted_iota(jnp.int32, (c2, c2), 1)
    same_head = (row // c) == (col // c)
    strict = jnp.logical_and(same_head, col < row)
    incl = jnp.logical_and(same_head, col <= row)
    inv_blk = (row // SCAN_INV_BLOCK) == (col // SCAN_INV_BLOCK)
    eye = jnp.where(row == col, 1.0, 0.0).astype(F32)
    tri = jnp.where(lax.broadcasted_iota(jnp.int32, (c, c), 0)
                    >= lax.broadcasted_iota(jnp.int32, (c, c), 1), 1.0, 0.0).astype(BF16)

    def stack(x):
        return jnp.concatenate([jnp.where(head0, x, 0.0), jnp.where(head0, 0.0, x)], axis=0)

    def pair_chunk(t0, p):
        lanes = slice(p * LANES, (p + 1) * LANES)
        rows = pl.ds(t0, c)
        lw = lw_ref[0, rows, lanes]
        cum = _dot_exact_lhs(tri, lw)
        yield
        r = r_ref[0, rows, lanes]
        k = k_ref[0, rows, lanes]
        kk = kk_ref[0, rows, lanes]
        b = b_ref[0, rows, lanes]
        cum_end = cum[c - 1:c, :]
        e_neg = jnp.exp(-cum)
        e_end = jnp.exp(cum_end - cum)
        a_s = stack(-kk * jnp.exp(cum - lw))
        r_s = stack(r * jnp.exp(cum))
        b_s = stack(b * e_neg)
        k_s = stack(k * e_neg)
        bend_s = stack(b * e_end)
        kend_s = stack(k * e_end)
        v_s = stack(v_ref[0, rows, lanes])
        gram = _dot_nt(jnp.concatenate([a_s, r_s], axis=0),
                       jnp.concatenate([b_s, k_s], axis=0))
        kv = _dot_tn(v_s, kend_s)
        yield
        l_ab = jnp.where(strict, gram[:c2, :c2], 0.0)
        l_ak = jnp.where(strict, gram[:c2, c2:], 0.0)
        l_rb = jnp.where(incl, gram[c2:, :c2], 0.0)
        l_rk = jnp.where(incl, gram[c2:, c2:], 0.0)

        x = jnp.where(inv_blk, l_ab, 0.0)
        off = jnp.where(inv_blk, 0.0, l_ab)
        dinv = eye + x
        w1 = _dot(l_ak, v_s)
        y1 = _dot(l_rk, v_s)
        for _ in range(int(math.log2(SCAN_INV_BLOCK)) - 1):
            x = _dot(x, x)
            yield
            dinv = dinv + _dot(dinv, x)
            yield
        f = _dot(dinv, off)
        yield
        f2 = _dot(f, f)
        yield
        gm = eye + f
        gm = gm + _dot(gm, f2)
        yield
        t_inv = _dot(gm, dinv)
        yield
        ta = _dot(t_inv, jnp.concatenate([a_s, w1], axis=1))
        yield
        a_hat = ta[:, :LANES]
        v_hat = ta[:, LANES:]
        state = s_ref[p]
        m1 = _dot_nt(jnp.concatenate([a_hat, r_s], axis=0), state)
        yield
        u = m1[:c2] + v_hat
        y = m1[c2:] + y1 + _dot(l_rb, u)
        s_ref[p] = state * jnp.exp(cum_end) + kv + _dot_tn(u, bend_s)
        y_ref[0, rows, lanes] = y[:c] + y[c:]

    def chunk(ci, carry):
        t0 = pl.multiple_of(ci * c, c)
        for _ in itertools.zip_longest(*[pair_chunk(t0, p) for p in range(pairs)]):
            pass
        return carry

    lax.fori_loop(0, tb // c, chunk, 0)


def _rwkv_scan(r, lw, k, v, kk, b, tb):
    batch, seq, w = r.shape
    pairs = w // LANES
    spec = pl.BlockSpec((1, tb, w), lambda bi, t: (bi, t, 0))
    kern = lambda *refs: _scan_kernel(*refs, tb=tb, pairs=pairs)
    return pl.pallas_call(
        kern,
        grid=(batch, seq // tb),
        in_specs=[spec] * 6,
        out_specs=spec,
        out_shape=jax.ShapeDtypeStruct((batch, seq, w), F32),
        scratch_shapes=[pltpu.VMEM((pairs, LANES, LANES), F32)],
        compiler_params=_params(("parallel", "arbitrary")),
        name="rwkv_scan",
    )(r, lw, k, v, kk, b)


def _out_proj_kernel(x_ref, attn_ref, y_ref, bonus_ref, g_ref, lng, lnb, bd_ref, w_ref, n2g,
                     h_ref, hn_ref):
    bd = bd_ref[...]
    y = y_ref[...]
    inv_n = 1.0 / RWKV_HEAD_DIM
    d = y - _dot_exact_rhs(y, bd) * inv_n
    var = _dot_exact_rhs(d * d, bd) * inv_n
    yn = d * lax.rsqrt(var + RWKV_GN_EPS) * lng[...] + lnb[...]
    rw = (yn + bonus_ref[...]) * g_ref[...]
    h = (x_ref[...] + _dot(attn_ref[...], w_ref[:ATTN_WIDTH, :])
         + _dot(rw, w_ref[ATTN_WIDTH:, :]))
    h_ref[...] = h
    inv = lax.rsqrt(jnp.mean(h * h, axis=-1, keepdims=True) + RMS_EPS)
    hn_ref[...] = (h * inv * n2g[...]).astype(BF16)


def _out_proj(x, attn, y, bonus, g, lng, lnb, bd, w_out, n2g, tm):
    m = x.shape[0]
    row = lambda width: pl.BlockSpec((tm, width), lambda i: (i, 0))
    full = lambda a: pl.BlockSpec(a.shape, lambda i: (0,) * a.ndim)
    return pl.pallas_call(
        _out_proj_kernel,
        grid=(m // tm,),
        in_specs=[row(D_MODEL), row(ATTN_WIDTH), row(RWKV_WIDTH), row(RWKV_WIDTH), row(RWKV_WIDTH),
                  full(lng), full(lnb), full(bd), full(w_out), full(n2g)],
        out_specs=[row(D_MODEL), row(D_MODEL)],
        out_shape=[jax.ShapeDtypeStruct((m, D_MODEL), F32),
                   jax.ShapeDtypeStruct((m, D_MODEL), BF16)],
        compiler_params=_params(("parallel",)),
        name="out_proj",
    )(x, attn, y, bonus, g, lng, lnb, bd, w_out, n2g)


def _peer_q_kernel(hn_ref, wq_ref, keys_ref, st_ref):
    q = jnp.dot(hn_ref[...], wq_ref[...], preferred_element_type=F32)
    for hp in range(2 * PEER_HEADS):
        qs = q[:, hp * PEER_HALF:(hp + 1) * PEER_HALF]
        st_ref[hp] = _dot_nt(keys_ref[hp], qs)


def _peer_q(hn, wq, keys, tm):
    m = hn.shape[0]
    return pl.pallas_call(
        _peer_q_kernel,
        grid=(m // tm,),
        in_specs=[pl.BlockSpec((tm, D_MODEL), lambda i: (i, 0)),
                  pl.BlockSpec(wq.shape, lambda i: (0, 0)),
                  pl.BlockSpec(keys.shape, lambda i: (0, 0, 0))],
        out_specs=pl.BlockSpec((2 * PEER_HEADS, PEER_NKEYS, tm), lambda i: (0, 0, i)),
        out_shape=jax.ShapeDtypeStruct((2 * PEER_HEADS, PEER_NKEYS, m), F32),
        compiler_params=_params(("parallel",)),
        name="peer_q",
    )(hn, wq, keys)


def _top_desc(x, k, exact_ties):
    n = x.shape[0]
    rows = lax.broadcasted_iota(jnp.int32, x.shape, 0)
    rank = jnp.full(x.shape, float(k), F32)
    outs = []
    for i in range(k):
        m = jnp.max(x, axis=0, keepdims=True)
        outs.append(m)
        pick = x == m
        if exact_ties:
            pick = rows == jnp.min(jnp.where(pick, rows, n), axis=0, keepdims=True)
        x = jnp.where(pick, -jnp.inf, x)
        rank = jnp.where(pick, float(i), rank)
    gone = jnp.sum(jnp.where(x == -jnp.inf, 1.0, 0.0), axis=0, keepdims=True)
    return jnp.concatenate(outs, axis=0), rank, gone


def _peer_topk_kernel(s_ref, cnt1_ref, e1_ref, rank2_ref, e2_ref):
    k = PEER_TOPK
    widths = [k // (a + 1) for a in range(k)]
    pad = -sum(widths) % 8

    def run(exact_ties):
        s1 = s_ref[0]
        s2 = s_ref[1]
        sv1, rank1, gone1 = _top_desc(s1, k, exact_ties)
        sv2, rank2, gone2 = _top_desc(s2, k, exact_ties)
        cand = jnp.concatenate([sv1[a:a + 1] + sv2[:widths[a]] for a in range(k)]
                               + [jnp.full((pad, s1.shape[1]), -jnp.inf, F32)], axis=0)
        best, pos, gone3 = _top_desc(cand, k, exact_ties)
        chosen = jnp.where(pos < float(k), 1.0, 0.0)
        z = jnp.sum(jnp.exp(best - best[0:1]), axis=0, keepdims=True)
        cnt1 = jnp.zeros(s1.shape, F32)
        row = 0
        for a in range(k):
            cnt_a = jnp.sum(chosen[row:row + widths[a]], axis=0, keepdims=True)
            cnt1 = jnp.where(rank1 == float(a), cnt_a, cnt1)
            row += widths[a]
        cnt1_ref[0] = cnt1
        e1_ref[0] = jnp.exp(s1 - sv1[0:1]) / z
        rank2_ref[0] = rank2
        e2_ref[0] = jnp.exp(s2 - sv2[0:1])
        return jnp.max(jnp.maximum(jnp.maximum(gone1, gone2), gone3 - float(pad)))

    most_gone = run(exact_ties=False)

    @pl.when(most_gone > float(k))
    def _():
        run(exact_ties=True)


def _peer_topk(st, tl):
    m = st.shape[2]
    spec = pl.BlockSpec((1, PEER_NKEYS, tl), lambda h, i: (h, 0, i))
    shape = lambda dt: jax.ShapeDtypeStruct((PEER_HEADS, PEER_NKEYS, m), dt)
    return pl.pallas_call(
        _peer_topk_kernel,
        grid=(PEER_HEADS, m // tl),
        in_specs=[pl.BlockSpec((2, PEER_NKEYS, tl), lambda h, i: (h, 0, i))],
        out_specs=[spec] * 4,
        out_shape=[shape(F32)] * 4,
        compiler_params=_params(("parallel", "parallel")),
        name="peer_topk",
    )(st)


def _peer_dense_kernel(u_ref, hn_ref, vt_ref, rank2_ref, e2_ref, cnta_ref, cntb_ref,
                       e1a_ref, e1b_ref, yt_ref, act_ref, hpre_ref, *, n1):
    step = pl.program_id(1)
    et = hpre_ref.shape[1]
    row_refs = ((cnta_ref, e1a_ref), (cntb_ref, e1b_ref))

    @pl.when(step == 0)
    def _():
        yt_ref[...] = jnp.zeros_like(yt_ref)
        hpre_ref[1] = jnp.zeros(hpre_ref.shape[1:], hpre_ref.dtype)
        act_ref[0] = jnp.zeros(act_ref.shape[1:], act_ref.dtype)

    def gate_tile(half, jt, ib):
        prev = 1 - half
        cnt_ref, e1r_ref = row_refs[half]
        ls = slice(jt * LANES, (jt + 1) * LANES)
        rs = slice(ib * DENSE_ROWS, (ib + 1) * DENSE_ROWS)
        gates = [jnp.zeros((DENSE_ROWS, LANES), F32) for _ in range(n1)]
        for h in range(PEER_HEADS):
            r2 = rank2_ref[h, rs, ls]
            e2 = e2_ref[h, rs, ls]
            for a in range(n1):
                hit = r2 < cnt_ref[h, 0, a:a + 1, ls]
                gates[a] = gates[a] + jnp.where(hit, e2, 0.0) * e1r_ref[h, 0, a:a + 1, ls]
        for a in range(n1):
            ers = slice(a * PEER_NKEYS + ib * DENSE_ROWS, a * PEER_NKEYS + (ib + 1) * DENSE_ROWS)
            x = hpre_ref[prev, ers, ls]
            gelu = 0.5 * x * (1.0 + lax.erf(x * (0.5 ** 0.5)))
            act_ref[prev, ers, ls] = gelu * gates[a]

    tt = hpre_ref.shape[2]
    n_ib = PEER_NKEYS // DENSE_ROWS
    for half in range(2):
        cur = half
        hs = slice(half * et, (half + 1) * et)
        for piece in range(tt // MXU_COLS):
            cs = slice(piece * MXU_COLS, (piece + 1) * MXU_COLS)
            tiles = [(jt, ib) for jt in range(piece * MXU_COLS // LANES, (piece + 1) * MXU_COLS // LANES)
                     for ib in range(n_ib)]
            hpre_ref[cur, :, cs] = lax.dot_general(u_ref[hs, :], hn_ref[cs, :], _NT,
                                                   preferred_element_type=F32)
            per_down = len(tiles) * DOWN_ROWS // D_MODEL
            for i, tile in enumerate(tiles):
                if i % per_down == per_down // 2:
                    ms = slice(i // per_down * DOWN_ROWS, (i // per_down + 1) * DOWN_ROWS)
                    yt_ref[ms, cs] += jnp.dot(vt_ref[ms, hs], act_ref[cur, :, cs].astype(BF16),
                                              preferred_element_type=F32)
                gate_tile(half, *tile)


def _peer_dense(u, hn, vt, cnt1, e1, rank2, e2, tt, n1):
    m = hn.shape[0]
    et = n1 * PEER_NKEYS
    n_steps = PEER_NKEYS // (2 * n1)
    n_tiles = 2 * n_steps
    tok3 = lambda a: pl.BlockSpec((a.shape[0], a.shape[1], tt), lambda i, e: (0, 0, i))
    by_tile = lambda a: a.reshape(PEER_HEADS, n_tiles, n1, m)
    cntr = by_tile(cnt1)
    e1r = by_tile(e1)
    rows = lambda half: pl.BlockSpec(
        (PEER_HEADS, 1, n1, tt),
        lambda i, e: (0, jnp.clip(2 * e + half - 1, 0, n_tiles - 1), 0, i))
    kern = lambda *refs: _peer_dense_kernel(*refs, n1=n1)
    return pl.pallas_call(
        kern,
        grid=(m // tt, n_steps + 1),
        in_specs=[pl.BlockSpec((2 * et, D_MODEL), lambda i, e: (jnp.minimum(e, n_steps - 1), 0)),
                  pl.BlockSpec((tt, D_MODEL), lambda i, e: (i, 0)),
                  pl.BlockSpec((D_MODEL, 2 * et), lambda i, e: (0, jnp.maximum(e - 1, 0))),
                  tok3(rank2), tok3(e2), rows(0), rows(1), rows(0), rows(1)],
        out_specs=pl.BlockSpec((D_MODEL, tt), lambda i, e: (0, i)),
        out_shape=jax.ShapeDtypeStruct((D_MODEL, m), F32),
        scratch_shapes=[pltpu.VMEM((2, et, tt), F32), pltpu.VMEM((2, et, tt), F32)],
        compiler_params=_params(("parallel", "arbitrary")),
        name="peer_dense",
    )(u, hn, vt, rank2, e2, cntr, cntr, e1r, e1r)


def _final_kernel(h_ref, yt_ref, g_ref, o_ref):
    h = h_ref[...] + yt_ref[...].T
    inv = lax.rsqrt(jnp.mean(h * h, axis=-1, keepdims=True) + RMS_EPS)
    o_ref[...] = h * inv * g_ref[...]


def _final(h, yt, g, tm):
    m = h.shape[0]
    row = pl.BlockSpec((tm, D_MODEL), lambda i: (i, 0))
    return pl.pallas_call(
        _final_kernel,
        grid=(m // tm,),
        in_specs=[row, pl.BlockSpec((D_MODEL, tm), lambda i: (0, i)),
                  pl.BlockSpec((1, D_MODEL), lambda i: (0, 0))],
        out_specs=row,
        out_shape=jax.ShapeDtypeStruct((m, D_MODEL), F32),
        compiler_params=_params(("parallel",)),
        name="final_norm",
    )(h, yt, g)


def _row(a):
    return a.reshape(1, -1).astype(F32)


def _pad_rows(w, start, total):
    return jnp.zeros((total, w.shape[1]), BF16).at[start:start + w.shape[0]].set(w.astype(BF16))


def kernel(x, norm1_g, w_in, rwkv_mu, rwkv_w0, rwkv_w2, rwkv_a0, rwkv_a2, rwkv_g2, rwkv_k_k, rwkv_k_a, rwkv_r_k, rwkv_ln_g, rwkv_ln_b, w_out, norm2_g, peer_wq, peer_sub_keys, peer_u, peer_v, final_g):
    batch, seq, d = x.shape
    tokens = batch * seq
    w = RWKV_WIDTH
    xt = x.reshape(tokens, d)

    w_in_p = jnp.pad(w_in[0].astype(BF16), ((0, 0), (0, IN_COLS_PAD - IN_COLS)))
    proj = _norm_matmul(xt, _row(norm1_g[0]), w_in_p, tm=1024, tn=512)
    proj = proj.reshape(batch, seq, IN_COLS_PAD)

    attn = _moba(proj, batch, seq, heads=4)

    mu = rwkv_mu[0]
    mu_l = jnp.pad(mu[3 * w:], (0, LORA_PAD - LORA_COLS))
    hid = lax.broadcasted_iota(jnp.int32, (w, w), 0) // RWKV_HEAD_DIM
    bd = (hid == hid.T).astype(BF16)
    prep_w = [_row(mu[:w]), _row(mu[w:2 * w]), _row(mu[2 * w:3 * w]), _row(mu_l),
              _row(rwkv_w0[0]), _pad_rows(rwkv_w2[0], 0, LORA_PAD),
              _row(rwkv_a0[0]), _pad_rows(rwkv_a2[0], DECAY_LORA, LORA_PAD),
              _pad_rows(rwkv_g2[0], DECAY_LORA + AAA_LORA, LORA_PAD),
              _row(rwkv_k_k[0]), _row(rwkv_k_a[0]), _row(rwkv_r_k[0]), bd]
    r, lw, k2, v, kk, b, g, bonus = _rwkv_prep(proj, batch, seq, prep_w, tt=256)
    y = _rwkv_scan(r, lw, k2, v, kk, b, tb=256)

    flat = lambda a: a.reshape(tokens, a.shape[-1])
    h, hn = _out_proj(xt, flat(attn), flat(y), flat(bonus), flat(g), _row(rwkv_ln_g[0]),
                      _row(rwkv_ln_b[0]), bd, w_out[0].astype(BF16), _row(norm2_g[0]), tm=256)

    keys = peer_sub_keys[0].reshape(2 * PEER_HEADS, PEER_NKEYS, PEER_HALF).astype(BF16)
    st = _peer_q(hn, peer_wq[0].astype(BF16), keys, tm=512)
    cnt1, e1, rank2, e2 = _peer_topk(st, tl=512)
    yt = _peer_dense(peer_u[0].astype(BF16), hn, peer_v[0].T.astype(BF16), cnt1, e1, rank2, e2,
                     tt=512, n1=4)
    out = _final(h, yt, final_g.reshape(1, d), tm=512)
    return out.reshape(batch, seq, d)
```

```python
import itertools
import math

import jax
import jax.numpy as jnp
from jax import lax
from jax.experimental import pallas as pl
from jax.experimental.pallas import tpu as pltpu

F32 = jnp.float32
BF16 = jnp.bfloat16

D_MODEL = 2048
ATTN_HEADS = 8
ATTN_HEAD_DIM = 128
ATTN_WIDTH = ATTN_HEADS * ATTN_HEAD_DIM
MOBA_BLOCK = 256
MOBA_TOPK = 3

RWKV_HEAD_DIM = 64
RWKV_WIDTH = D_MODEL - ATTN_WIDTH
RWKV_HEADS = RWKV_WIDTH // RWKV_HEAD_DIM
DECAY_LORA = 96
AAA_LORA = 96
GATE_LORA = 256
LORA_COLS = DECAY_LORA + AAA_LORA + GATE_LORA
LORA_PAD = 512
RWKV_GN_EPS = RWKV_HEAD_DIM * 1e-5
IN_COLS = 3 * ATTN_WIDTH + 3 * RWKV_WIDTH + LORA_COLS
IN_COLS_PAD = 3 * ATTN_WIDTH + 3 * RWKV_WIDTH + LORA_PAD

PEER_HEADS = 8
PEER_NKEYS = 128
PEER_HALF = 128
PEER_TOPK = 16

RMS_EPS = 1e-6
NEG = -1e30

LANES = 128
SCAN_CHUNK = 64
SCAN_INV_BLOCK = 16
DOWN_ROWS = 2048
MXU_COLS = 256
DENSE_ROWS = 32
VMEM_LIMIT = 56 * 1024 * 1024

_NT = (((1,), (1,)), ((), ()))
_TN = (((0,), (0,)), ((), ()))


def _params(sem, flags=None):
    return pltpu.CompilerParams(dimension_semantics=sem, vmem_limit_bytes=VMEM_LIMIT, flags=flags)


def _dot(a, b):
    return jnp.dot(a.astype(BF16), b.astype(BF16), preferred_element_type=F32)


def _dot_nt(a, b):
    return lax.dot_general(a.astype(BF16), b.astype(BF16), _NT, preferred_element_type=F32)


def _dot_tn(a, b):
    return lax.dot_general(a.astype(BF16), b.astype(BF16), _TN, preferred_element_type=F32)


def _split3(x):
    hi = x.astype(BF16)
    r1 = x - hi.astype(F32)
    mid = r1.astype(BF16)
    lo = (r1 - mid.astype(F32)).astype(BF16)
    return hi, mid, lo


def _dot_exact_rhs(x, ones_bf16):
    hi = x.astype(BF16)
    lo = (x - hi.astype(F32)).astype(BF16)
    d = lambda p: jnp.dot(p, ones_bf16, preferred_element_type=F32)
    return d(hi) + d(lo)


def _dot_exact_lhs(ones_bf16, x):
    hi, mid, lo = _split3(x)
    d = lambda p: jnp.dot(ones_bf16, p, preferred_element_type=F32)
    return d(hi) + d(mid) + d(lo)


def _norm_matmul_kernel(x_ref, g_ref, w_ref, o_ref, xn_ref):
    @pl.when(pl.program_id(1) == 0)
    def _():
        x = x_ref[...]
        inv = lax.rsqrt(jnp.mean(x * x, axis=-1, keepdims=True) + RMS_EPS)
        xn_ref[...] = (x * inv * g_ref[...]).astype(BF16)

    o_ref[...] = jnp.dot(xn_ref[...], w_ref[...], preferred_element_type=F32)


def _norm_matmul(x, g, w, tm, tn):
    m, k = x.shape
    n = w.shape[1]
    return pl.pallas_call(
        _norm_matmul_kernel,
        grid=(m // tm, n // tn),
        in_specs=[pl.BlockSpec((tm, k), lambda i, j: (i, 0)),
                  pl.BlockSpec((1, k), lambda i, j: (0, 0)),
                  pl.BlockSpec((k, tn), lambda i, j: (0, j))],
        out_specs=pl.BlockSpec((tm, tn), lambda i, j: (i, j)),
        out_shape=jax.ShapeDtypeStruct((m, n), F32),
        scratch_shapes=[pltpu.VMEM((tm, k), BF16)],
        compiler_params=_params(("parallel", "arbitrary")),
        name="in_proj",
    )(x, g, w)


def _moba_kernel(q_ref, k_ref, v_ref, o_ref, kmean_ref, kb_ref, vt_ref, *, nb, heads):
    blk_len = MOBA_BLOCK
    hd = ATTN_HEAD_DIM
    hg = pl.program_id(1)
    qi = pl.program_id(2)
    hrange = range(heads)

    @pl.when(qi == 0)
    def _():
        for hh in hrange:
            hl = slice(hh * hd, (hh + 1) * hd)
            for j in range(nb):
                rows = slice(j * blk_len, (j + 1) * blk_len)
                kj = k_ref[0, rows, hl]
                kmean_ref[hh, j:j + 1, :] = jnp.mean(kj, axis=0, keepdims=True)
                kb_ref[hh, rows, :] = kj.astype(BF16)
                vt_ref[hh, :, rows] = v_ref[0, rows, hl].T.astype(BF16)

    log2e = math.log2(math.e)
    scale2 = (hd ** -0.5) * log2e
    kq = (lax.broadcasted_iota(jnp.int32, (blk_len, blk_len), 1)
          - lax.broadcasted_iota(jnp.int32, (blk_len, blk_len), 0))
    kqf = kq.astype(F32)
    blk = lax.broadcasted_iota(jnp.int32, (nb, blk_len), 0)
    past = blk < qi

    def select(hh):
        q = q_ref[0, :, hh * hd:(hh + 1) * hd]
        gate = lax.dot_general(kmean_ref[hh], q, _NT, precision=lax.Precision.HIGHEST,
                               preferred_element_type=F32)
        g = jnp.where(past, gate, NEG)
        selw = jnp.zeros(gate.shape, F32)
        for _ in range(min(MOBA_TOPK, nb)):
            m = jnp.max(g, axis=0, keepdims=True)
            first = jnp.min(jnp.where(g == m, blk, nb), axis=0, keepdims=True)
            pick = blk == first
            selw = jnp.where(pick, 1.0, selw)
            g = jnp.where(pick, -jnp.inf, g)
        return jnp.where(past, selw, 0.0), q.astype(BF16)

    sel_q = [select(hh) for hh in hrange]
    selw = [x[0] for x in sel_q]
    qb = [x[1] for x in sel_q]
    slope2 = [jnp.exp2(jnp.zeros((1, 1), F32) - (hg * heads + hh + 1).astype(F32)) * log2e
              for hh in hrange]
    bias0 = [slope2[hh] * kqf for hh in hrange]

    def scores(j):
        r0 = pl.multiple_of(j * blk_len, blk_len)
        return [lax.dot_general(kb_ref[hh, pl.ds(r0, blk_len), :], qb[hh], _NT,
                                preferred_element_type=F32) * scale2 - bias0[hh] for hh in hrange]

    def weighted_values(j, p):
        r0 = pl.multiple_of(j * blk_len, blk_len)
        return [jnp.dot(vt_ref[hh, :, pl.ds(r0, blk_len)], p[hh].astype(BF16),
                        preferred_element_type=F32) for hh in hrange]

    s = [jnp.where(kq >= 0, x, NEG) for x in scores(qi)]
    m = [jnp.max(x, axis=0, keepdims=True) for x in s]
    p = [jnp.exp2(s[hh] - m[hh]) for hh in hrange]
    l = [jnp.sum(x, axis=0, keepdims=True) for x in p]
    acc = weighted_values(qi, p)

    def body(j, carry):
        m, l, acc = carry
        off = ((qi - j) * blk_len).astype(F32)
        s = scores(j)
        selj = [jnp.max(jnp.where(blk == j, selw[hh], 0.0), axis=0, keepdims=True) for hh in hrange]
        s = [jnp.where(selj[hh] > 0.0, s[hh] - slope2[hh] * off, NEG) for hh in hrange]
        m_new = [jnp.maximum(m[hh], jnp.max(s[hh], axis=0, keepdims=True)) for hh in hrange]
        alpha = [jnp.exp2(m[hh] - m_new[hh]) for hh in hrange]
        p = [jnp.exp2(s[hh] - m_new[hh]) for hh in hrange]
        l = [alpha[hh] * l[hh] + jnp.sum(p[hh], axis=0, keepdims=True) for hh in hrange]
        pv = weighted_values(j, p)
        acc = [alpha[hh] * acc[hh] + pv[hh] for hh in hrange]
        return m_new, l, acc

    m, l, acc = lax.fori_loop(0, qi, body, (m, l, acc))
    for hh in hrange:
        o_ref[0, :, hh * hd:(hh + 1) * hd] = (acc[hh] / l[hh]).T


def _moba(proj, batch, seq, heads):
    nb = seq // MOBA_BLOCK
    hd = ATTN_HEAD_DIM
    hw = heads * hd
    groups = ATTN_HEADS // heads
    kern = lambda *refs: _moba_kernel(*refs, nb=nb, heads=heads)
    return pl.pallas_call(
        kern,
        grid=(batch, groups, nb),
        in_specs=[pl.BlockSpec((1, MOBA_BLOCK, hw), lambda b, h, i: (b, i, h)),
                  pl.BlockSpec((1, seq, hw), lambda b, h, i: (b, 0, groups + h)),
                  pl.BlockSpec((1, seq, hw), lambda b, h, i: (b, 0, 2 * groups + h))],
        out_specs=pl.BlockSpec((1, MOBA_BLOCK, hw), lambda b, h, i: (b, i, h)),
        out_shape=jax.ShapeDtypeStruct((batch, seq, ATTN_WIDTH), F32),
        scratch_shapes=[pltpu.VMEM((heads, nb, hd), F32),
                        pltpu.VMEM((heads, seq, hd), BF16),
                        pltpu.VMEM((heads, hd, seq), BF16)],
        compiler_params=_params(("parallel", "parallel", "arbitrary")),
        name="moba",
    )(proj, proj, proj)


def _rwkv_prep_kernel(r_ref, k_ref, v_ref, l_ref, rp_ref, kp_ref, vp_ref, lp_ref,
                      mu_r, mu_k, mu_v, mu_l, w0, w2p, a0, a2p, g2p, kk_w, ka_w, rk_w, bd_ref,
                      r_o, lw_o, k_o, v_o, kk_o, b_o, g_o, bonus_o):
    first = pl.program_id(1) == 0

    def shift(cur_ref, prev_ref, mu_ref):
        cur = cur_ref[0]
        prev_last = jnp.where(first, 0.0, prev_ref[0, 7:8, :])
        rows = lax.broadcasted_iota(jnp.int32, cur.shape, 0)
        prev = jnp.where(rows == 0, prev_last, pltpu.roll(cur, 1, axis=0))
        return cur + (prev - cur) * mu_ref[...]

    r = shift(r_ref, rp_ref, mu_r)
    k = shift(k_ref, kp_ref, mu_k)
    v = shift(v_ref, vp_ref, mu_v)
    lo = shift(l_ref, lp_ref, mu_l)

    bd = bd_ref[...]
    z = -(w0[...] + _dot(jnp.tanh(lo), w2p[...]))
    softplus = jnp.maximum(z, 0.0) + jnp.log1p(jnp.exp(-jnp.abs(z)))
    lw = -jnp.exp(-softplus - 0.5)
    a = jax.nn.sigmoid(a0[...] + _dot(lo, a2p[...]))
    g = _dot(jax.nn.sigmoid(lo), g2p[...])
    kk = k * kk_w[...]
    kk = kk / jnp.maximum(jnp.sqrt(_dot_exact_rhs(kk * kk, bd)), 1e-12)
    k2 = k * (1.0 + (a - 1.0) * ka_w[...])
    r_o[0] = r
    lw_o[0] = lw
    k_o[0] = k2
    v_o[0] = v
    kk_o[0] = kk
    b_o[0] = kk * a
    g_o[0] = g
    bonus_o[0] = _dot_exact_rhs(r * k2 * rk_w[...], bd) * v


def _rwkv_prep(proj, batch, seq, weights, tt):
    w = RWKV_WIDTH
    c0 = 3 * ATTN_WIDTH // w
    lc = (3 * ATTN_WIDTH + 3 * w) // LORA_PAD
    pb = tt // 8
    cur = lambda width, col: pl.BlockSpec((1, tt, width), lambda b, i: (b, i, col))
    prev = lambda width, col: pl.BlockSpec(
        (1, 8, width), lambda b, i: (b, jnp.maximum(i * pb - 1, 0), col))
    full = lambda a: pl.BlockSpec(a.shape, lambda b, i: (0,) * a.ndim)
    out = pl.BlockSpec((1, tt, w), lambda b, i: (b, i, 0))
    return pl.pallas_call(
        _rwkv_prep_kernel,
        grid=(batch, seq // tt),
        in_specs=[cur(w, c0), cur(w, c0 + 1), cur(w, c0 + 2), cur(LORA_PAD, lc),
                  prev(w, c0), prev(w, c0 + 1), prev(w, c0 + 2), prev(LORA_PAD, lc)]
                 + [full(a) for a in weights],
        out_specs=[out] * 8,
        out_shape=[jax.ShapeDtypeStruct((batch, seq, w), F32)] * 8,
        compiler_params=_params(("parallel", "arbitrary")),
        name="rwkv_prep",
    )(proj, proj, proj, proj, proj, proj, proj, proj, *weights)


def _scan_kernel(r_ref, lw_ref, k_ref, v_ref, kk_ref, b_ref, y_ref, s_ref, *, tb, pairs):
    c = SCAN_CHUNK
    c2 = 2 * c

    @pl.when(pl.program_id(1) == 0)
    def _():
        s_ref[...] = jnp.zeros_like(s_ref)

    head0 = lax.broadcasted_iota(jnp.int32, (c, LANES), 1) < RWKV_HEAD_DIM
    row = lax.broadcasted_iota(jnp.int32, (c2, c2), 0)
    col = lax.broadcasted_iota(jnp.int32, (c2, c2), 1)
    same_head = (row // c) == (col // c)
    strict = jnp.logical_and(same_head, col < row)
    incl = jnp.logical_and(same_head, col <= row)
    inv_blk = (row // SCAN_INV_BLOCK) == (col // SCAN_INV_BLOCK)
    eye = jnp.where(row == col, 1.0, 0.0).astype(F32)
    tri = jnp.where(lax.broadcasted_iota(jnp.int32, (c, c), 0)
                    >= lax.broadcasted_iota(jnp.int32, (c, c), 1), 1.0, 0.0).astype(BF16)

    def stack(x):
        return jnp.concatenate([jnp.where(head0, x, 0.0), jnp.where(head0, 0.0, x)], axis=0)

    def pair_chunk(t0, p):
        lanes = slice(p * LANES, (p + 1) * LANES)
        rows = pl.ds(t0, c)
        lw = lw_ref[0, rows, lanes]
        cum = _dot_exact_lhs(tri, lw)
        yield
        r = r_ref[0, rows, lanes]
        k = k_ref[0, rows, lanes]
        kk = kk_ref[0, rows, lanes]
        b = b_ref[0, rows, lanes]
        cum_end = cum[c - 1:c, :]
        e_neg = jnp.exp(-cum)
        e_end = jnp.exp(cum_end - cum)
        a_s = stack(-kk * jnp.exp(cum - lw))
        r_s = stack(r * jnp.exp(cum))
        b_s = stack(b * e_neg)
        k_s = stack(k * e_neg)
        bend_s = stack(b * e_end)
        kend_s = stack(k * e_end)
        v_s = stack(v_ref[0, rows, lanes])
        gram = _dot_nt(jnp.concatenate([a_s, r_s], axis=0),
                       jnp.concatenate([b_s, k_s], axis=0))
        kv = _dot_tn(v_s, kend_s)
        yield
        l_ab = jnp.where(strict, gram[:c2, :c2], 0.0)
        l_ak = jnp.where(strict, gram[:c2, c2:], 0.0)
        l_rb = jnp.where(incl, gram[c2:, :c2], 0.0)
        l_rk = jnp.where(incl, gram[c2:, c2:], 0.0)

        x = jnp.where(inv_blk, l_ab, 0.0)
        off = jnp.where(inv_blk, 0.0, l_ab)
        dinv = eye + x
        w1 = _dot(l_ak, v_s)
        y1 = _dot(l_rk, v_s)
        for _ in range(int(math.log2(SCAN_INV_BLOCK)) - 1):
            x = _dot(x, x)
            yield
            dinv = dinv + _dot(dinv, x)
            yield
        f = _dot(dinv, off)
        yield
        f2 = _dot(f, f)
        yield
        gm = eye + f
        gm = gm + _dot(gm, f2)
        yield
        t_inv = _dot(gm, dinv)
        yield
        ta = _dot(t_inv, jnp.concatenate([a_s, w1], axis=1))
        yield
        a_hat = ta[:, :LANES]
        v_hat = ta[:, LANES:]
        state = s_ref[p]
        m1 = _dot_nt(jnp.concatenate([a_hat, r_s], axis=0), state)
        yield
        u = m1[:c2] + v_hat
        y = m1[c2:] + y1 + _dot(l_rb, u)
        s_ref[p] = state * jnp.exp(cum_end) + kv + _dot_tn(u, bend_s)
        y_ref[0, rows, lanes] = y[:c] + y[c:]

    def chunk(ci, carry):
        t0 = pl.multiple_of(ci * c, c)
        for _ in itertools.zip_longest(*[pair_chunk(t0, p) for p in range(pairs)]):
            pass
        return carry

    lax.fori_loop(0, tb // c, chunk, 0)


def _rwkv_scan(r, lw, k, v, kk, b, tb):
    batch, seq, w = r.shape
    pairs = w // LANES
    spec = pl.BlockSpec((1, tb, w), lambda bi, t: (bi, t, 0))
    kern = lambda *refs: _scan_kernel(*refs, tb=tb, pairs=pairs)
    return pl.pallas_call(
        kern,
        grid=(batch, seq // tb),
        in_specs=[spec] * 6,
        out_specs=spec,
        out_shape=jax.ShapeDtypeStruct((batch, seq, w), F32),
        scratch_shapes=[pltpu.VMEM((pairs, LANES, LANES), F32)],
        compiler_params=_params(("parallel", "arbitrary")),
        name="rwkv_scan",
    )(r, lw, k, v, kk, b)


def _out_proj_kernel(x_ref, attn_ref, y_ref, bonus_ref, g_ref, lng, lnb, bd_ref, w_ref, n2g,
                     h_ref, hn_ref):
    bd = bd_ref[...]
    y = y_ref[...]
    inv_n = 1.0 / RWKV_HEAD_DIM
    d = y - _dot_exact_rhs(y, bd) * inv_n
    var = _dot_exact_rhs(d * d, bd) * inv_n
    yn = d * lax.rsqrt(var + RWKV_GN_EPS) * lng[...] + lnb[...]
    rw = (yn + bonus_ref[...]) * g_ref[...]
    h = (x_ref[...] + _dot(attn_ref[...], w_ref[:ATTN_WIDTH, :])
         + _dot(rw, w_ref[ATTN_WIDTH:, :]))
    h_ref[...] = h
    inv = lax.rsqrt(jnp.mean(h * h, axis=-1, keepdims=True) + RMS_EPS)
    hn_ref[...] = (h * inv * n2g[...]).astype(BF16)


def _out_proj(x, attn, y, bonus, g, lng, lnb, bd, w_out, n2g, tm):
    m = x.shape[0]
    row = lambda width: pl.BlockSpec((tm, width), lambda i: (i, 0))
    full = lambda a: pl.BlockSpec(a.shape, lambda i: (0,) * a.ndim)
    return pl.pallas_call(
        _out_proj_kernel,
        grid=(m // tm,),
        in_specs=[row(D_MODEL), row(ATTN_WIDTH), row(RWKV_WIDTH), row(RWKV_WIDTH), row(RWKV_WIDTH),
                  full(lng), full(lnb), full(bd), full(w_out), full(n2g)],
        out_specs=[row(D_MODEL), row(D_MODEL)],
        out_shape=[jax.ShapeDtypeStruct((m, D_MODEL), F32),
                   jax.ShapeDtypeStruct((m, D_MODEL), BF16)],
        compiler_params=_params(("parallel",)),
        name="out_proj",
    )(x, attn, y, bonus, g, lng, lnb, bd, w_out, n2g)


def _peer_q_kernel(hn_ref, wq_ref, keys_ref, st_ref):
    q = jnp.dot(hn_ref[...], wq_ref[...], preferred_element_type=F32)
    for hp in range(2 * PEER_HEADS):
        qs = q[:, hp * PEER_HALF:(hp + 1) * PEER_HALF]
        st_ref[hp] = _dot_nt(keys_ref[hp], qs)


def _peer_q(hn, wq, keys, tm):
    m = hn.shape[0]
    return pl.pallas_call(
        _peer_q_kernel,
        grid=(m // tm,),
        in_specs=[pl.BlockSpec((tm, D_MODEL), lambda i: (i, 0)),
                  pl.BlockSpec(wq.shape, lambda i: (0, 0)),
                  pl.BlockSpec(keys.shape, lambda i: (0, 0, 0))],
        out_specs=pl.BlockSpec((2 * PEER_HEADS, PEER_NKEYS, tm), lambda i: (0, 0, i)),
        out_shape=jax.ShapeDtypeStruct((2 * PEER_HEADS, PEER_NKEYS, m), F32),
        compiler_params=_params(("parallel",)),
        name="peer_q",
    )(hn, wq, keys)


def _top_desc(x, k, exact_ties):
    n = x.shape[0]
    rows = lax.broadcasted_iota(jnp.int32, x.shape, 0)
    rank = jnp.full(x.shape, float(k), F32)
    outs = []
    for i in range(k):
        m = jnp.max(x, axis=0, keepdims=True)
        outs.append(m)
        pick = x == m
        if exact_ties:
            pick = rows == jnp.min(jnp.where(pick, rows, n), axis=0, keepdims=True)
        x = jnp.where(pick, -jnp.inf, x)
        rank = jnp.where(pick, float(i), rank)
    gone = jnp.sum(jnp.where(x == -jnp.inf, 1.0, 0.0), axis=0, keepdims=True)
    return jnp.concatenate(outs, axis=0), rank, gone


def _peer_topk_kernel(s_ref, cnt1_ref, e1_ref, rank2_ref, e2_ref, *, n1):
    k = PEER_TOPK
    widths = [k // (a + 1) for a in range(k)]
    pad = -sum(widths) % 8

    def run(exact_ties):
        s1 = s_ref[0]
        s2 = s_ref[1]
        sv1, rank1, gone1 = _top_desc(s1, k, exact_ties)
        sv2, rank2, gone2 = _top_desc(s2, k, exact_ties)
        cand = jnp.concatenate([sv1[a:a + 1] + sv2[:widths[a]] for a in range(k)]
                               + [jnp.full((pad, s1.shape[1]), -jnp.inf, F32)], axis=0)
        best, pos, gone3 = _top_desc(cand, k, exact_ties)
        chosen = jnp.where(pos < float(k), 1.0, 0.0)
        z = jnp.sum(jnp.exp(best - best[0:1]), axis=0, keepdims=True)
        cnt1 = jnp.zeros(s1.shape, F32)
        row = 0
        for a in range(k):
            cnt_a = jnp.sum(chosen[row:row + widths[a]], axis=0, keepdims=True)
            cnt1 = jnp.where(rank1 == float(a), cnt_a, cnt1)
            row += widths[a]
        e1 = jnp.exp(s1 - sv1[0:1]) / z
        for t in range(PEER_NKEYS // n1):
            cnt1_ref[0, t] = cnt1[t * n1:(t + 1) * n1]
            e1_ref[0, t] = e1[t * n1:(t + 1) * n1]
        rank2_ref[0] = rank2
        e2_ref[0] = jnp.exp(s2 - sv2[0:1])
        return jnp.max(jnp.maximum(jnp.maximum(gone1, gone2), gone3 - float(pad)))

    most_gone = run(exact_ties=False)

    @pl.when(most_gone > float(k))
    def _():
        run(exact_ties=True)


def _peer_topk(st, tl, n1):
    m = st.shape[2]
    n_tiles = PEER_NKEYS // n1
    spec = pl.BlockSpec((1, PEER_NKEYS, tl), lambda h, i: (h, 0, i))
    shape = jax.ShapeDtypeStruct((PEER_HEADS, PEER_NKEYS, m), F32)
    tile_spec = pl.BlockSpec((1, n_tiles, n1, tl), lambda h, i: (h, 0, 0, i))
    tile_shape = jax.ShapeDtypeStruct((PEER_HEADS, n_tiles, n1, m), F32)
    kern = lambda *refs: _peer_topk_kernel(*refs, n1=n1)
    return pl.pallas_call(
        kern,
        grid=(PEER_HEADS, m // tl),
        in_specs=[pl.BlockSpec((2, PEER_NKEYS, tl), lambda h, i: (h, 0, i))],
        out_specs=[tile_spec, tile_spec, spec, spec],
        out_shape=[tile_shape, tile_shape, shape, shape],
        compiler_params=_params(("parallel", "parallel")),
        name="peer_topk",
    )(st)


def _peer_dense_kernel(u_ref, hnt_ref, vt_ref, rank2_ref, e2_ref, cnta_ref, cntb_ref,
                       e1a_ref, e1b_ref, yt_ref, act_ref, hpre_ref, *, n1):
    step = pl.program_id(1)
    et = hpre_ref.shape[1]
    row_refs = ((cnta_ref, e1a_ref), (cntb_ref, e1b_ref))

    @pl.when(step == 0)
    def _():
        yt_ref[...] = jnp.zeros_like(yt_ref)
        hpre_ref[1] = jnp.zeros(hpre_ref.shape[1:], hpre_ref.dtype)
        act_ref[0] = jnp.zeros(act_ref.shape[1:], act_ref.dtype)

    def gate_tile(half, jt, ib):
        prev = 1 - half
        cnt_ref, e1r_ref = row_refs[half]
        ls = slice(jt * LANES, (jt + 1) * LANES)
        rs = slice(ib * DENSE_ROWS, (ib + 1) * DENSE_ROWS)
        gates = [jnp.zeros((DENSE_ROWS, LANES), F32) for _ in range(n1)]
        for h in range(PEER_HEADS):
            r2 = rank2_ref[h, rs, ls]
            e2 = e2_ref[h, rs, ls]
            for a in range(n1):
                hit = r2 < cnt_ref[h, 0, a:a + 1, ls]
                gates[a] = gates[a] + jnp.where(hit, e2, 0.0) * e1r_ref[h, 0, a:a + 1, ls]
        for a in range(n1):
            ers = slice(a * PEER_NKEYS + ib * DENSE_ROWS, a * PEER_NKEYS + (ib + 1) * DENSE_ROWS)
            x = hpre_ref[prev, ers, ls]
            gelu = 0.5 * x * (1.0 + lax.erf(x * (0.5 ** 0.5)))
            act_ref[prev, ers, ls] = gelu * gates[a]

    tt = hpre_ref.shape[2]
    n_ib = PEER_NKEYS // DENSE_ROWS
    for half in range(2):
        cur = half
        hs = slice(half * et, (half + 1) * et)
        for piece in range(tt // MXU_COLS):
            cs = slice(piece * MXU_COLS, (piece + 1) * MXU_COLS)
            tiles = [(jt, ib) for jt in range(piece * MXU_COLS // LANES, (piece + 1) * MXU_COLS // LANES)
                     for ib in range(n_ib)]
            hpre_ref[cur, :, cs] = jnp.dot(u_ref[hs, :], hnt_ref[:, cs],
                                           preferred_element_type=F32)
            per_down = len(tiles) * DOWN_ROWS // D_MODEL
            for i, tile in enumerate(tiles):
                if i % per_down == per_down // 2:
                    ms = slice(i // per_down * DOWN_ROWS, (i // per_down + 1) * DOWN_ROWS)
                    yt_ref[ms, cs] += jnp.dot(vt_ref[ms, hs], act_ref[cur, :, cs].astype(BF16),
                                              preferred_element_type=F32)
                gate_tile(half, *tile)


def _peer_dense(u, hnt, vt, cntr, e1r, rank2, e2, tt, n1):
    m = hnt.shape[1]
    et = n1 * PEER_NKEYS
    n_steps = PEER_NKEYS // (2 * n1)
    n_tiles = 2 * n_steps
    tok3 = lambda a: pl.BlockSpec((a.shape[0], a.shape[1], tt), lambda i, e: (0, 0, i))
    rows = lambda half: pl.BlockSpec(
        (PEER_HEADS, 1, n1, tt),
        lambda i, e: (0, jnp.clip(2 * e + half - 1, 0, n_tiles - 1), 0, i))
    kern = lambda *refs: _peer_dense_kernel(*refs, n1=n1)
    return pl.pallas_call(
        kern,
        grid=(m // tt, n_steps + 1),
        in_specs=[pl.BlockSpec((2 * et, D_MODEL), lambda i, e: (jnp.minimum(e, n_steps - 1), 0)),
                  pl.BlockSpec((D_MODEL, tt), lambda i, e: (0, i)),
                  pl.BlockSpec((D_MODEL, 2 * et), lambda i, e: (0, jnp.maximum(e - 1, 0))),
                  tok3(rank2), tok3(e2), rows(0), rows(1), rows(0), rows(1)],
        out_specs=pl.BlockSpec((D_MODEL, tt), lambda i, e: (0, i)),
        out_shape=jax.ShapeDtypeStruct((D_MODEL, m), F32),
        scratch_shapes=[pltpu.VMEM((2, et, tt), F32), pltpu.VMEM((2, et, tt), F32)],
        compiler_params=_params(("parallel", "arbitrary")),
        name="peer_dense",
    )(u, hnt, vt, rank2, e2, cntr, cntr, e1r, e1r)


def _final_kernel(h_ref, yt_ref, g_ref, o_ref):
    h = h_ref[...] + yt_ref[...].T
    inv = lax.rsqrt(jnp.mean(h * h, axis=-1, keepdims=True) + RMS_EPS)
    o_ref[...] = h * inv * g_ref[...]


def _final(h, yt, g, tm):
    m = h.shape[0]
    row = pl.BlockSpec((tm, D_MODEL), lambda i: (i, 0))
    return pl.pallas_call(
        _final_kernel,
        grid=(m // tm,),
        in_specs=[row, pl.BlockSpec((D_MODEL, tm), lambda i: (0, i)),
                  pl.BlockSpec((1, D_MODEL), lambda i: (0, 0))],
        out_specs=row,
        out_shape=jax.ShapeDtypeStruct((m, D_MODEL), F32),
        compiler_params=_params(("parallel",)),
        name="final_norm",
    )(h, yt, g)


def _row(a):
    return a.reshape(1, -1).astype(F32)


def _pad_rows(w, start, total):
    return jnp.zeros((total, w.shape[1]), BF16).at[start:start + w.shape[0]].set(w.astype(BF16))


def kernel(x, norm1_g, w_in, rwkv_mu, rwkv_w0, rwkv_w2, rwkv_a0, rwkv_a2, rwkv_g2, rwkv_k_k, rwkv_k_a, rwkv_r_k, rwkv_ln_g, rwkv_ln_b, w_out, norm2_g, peer_wq, peer_sub_keys, peer_u, peer_v, final_g):
    batch, seq, d = x.shape
    tokens = batch * seq
    w = RWKV_WIDTH
    xt = x.reshape(tokens, d)

    w_in_p = jnp.pad(w_in[0].astype(BF16), ((0, 0), (0, IN_COLS_PAD - IN_COLS)))
    proj = _norm_matmul(xt, _row(norm1_g[0]), w_in_p, tm=1024, tn=512)
    proj = proj.reshape(batch, seq, IN_COLS_PAD)

    attn = _moba(proj, batch, seq, heads=4)

    mu = rwkv_mu[0]
    mu_l = jnp.pad(mu[3 * w:], (0, LORA_PAD - LORA_COLS))
    hid = lax.broadcasted_iota(jnp.int32, (w, w), 0) // RWKV_HEAD_DIM
    bd = (hid == hid.T).astype(BF16)
    prep_w = [_row(mu[:w]), _row(mu[w:2 * w]), _row(mu[2 * w:3 * w]), _row(mu_l),
              _row(rwkv_w0[0]), _pad_rows(rwkv_w2[0], 0, LORA_PAD),
              _row(rwkv_a0[0]), _pad_rows(rwkv_a2[0], DECAY_LORA, LORA_PAD),
              _pad_rows(rwkv_g2[0], DECAY_LORA + AAA_LORA, LORA_PAD),
              _row(rwkv_k_k[0]), _row(rwkv_k_a[0]), _row(rwkv_r_k[0]), bd]
    r, lw, k2, v, kk, b, g, bonus = _rwkv_prep(proj, batch, seq, prep_w, tt=256)
    y = _rwkv_scan(r, lw, k2, v, kk, b, tb=256)

    flat = lambda a: a.reshape(tokens, a.shape[-1])
    h, hn = _out_proj(xt, flat(attn), flat(y), flat(bonus), flat(g), _row(rwkv_ln_g[0]),
                      _row(rwkv_ln_b[0]), bd, w_out[0].astype(BF16), _row(norm2_g[0]), tm=256)

    keys = peer_sub_keys[0].reshape(2 * PEER_HEADS, PEER_NKEYS, PEER_HALF).astype(BF16)
    st = _peer_q(hn, peer_wq[0].astype(BF16), keys, tm=512)
    cnt1, e1, rank2, e2 = _peer_topk(st, tl=512, n1=4)
    yt = _peer_dense(peer_u[0].astype(BF16), hn.T, peer_v[0].astype(BF16).T, cnt1, e1, rank2, e2,
                     tt=512, n1=4)
    out = _final(h, yt, final_g.reshape(1, d), tm=512)
    return out.reshape(batch, seq, d)
```

```python
import itertools
import math

import jax
import jax.numpy as jnp
from jax import lax
from jax.experimental import pallas as pl
from jax.experimental.pallas import tpu as pltpu

F32 = jnp.float32
BF16 = jnp.bfloat16

D_MODEL = 2048
ATTN_HEADS = 8
ATTN_HEAD_DIM = 128
ATTN_WIDTH = ATTN_HEADS * ATTN_HEAD_DIM
MOBA_BLOCK = 256
MOBA_TOPK = 3

RWKV_HEAD_DIM = 64
RWKV_WIDTH = D_MODEL - ATTN_WIDTH
RWKV_HEADS = RWKV_WIDTH // RWKV_HEAD_DIM
DECAY_LORA = 96
AAA_LORA = 96
GATE_LORA = 256
LORA_COLS = DECAY_LORA + AAA_LORA + GATE_LORA
LORA_PAD = 512
RWKV_GN_EPS = RWKV_HEAD_DIM * 1e-5
IN_COLS = 3 * ATTN_WIDTH + 3 * RWKV_WIDTH + LORA_COLS
IN_COLS_PAD = 3 * ATTN_WIDTH + 3 * RWKV_WIDTH + LORA_PAD

PEER_HEADS = 8
PEER_NKEYS = 128
PEER_HALF = 128
PEER_TOPK = 16

RMS_EPS = 1e-6
NEG = -1e30

LANES = 128
SCAN_CHUNK = 64
SCAN_INV_BLOCK = 16
DOWN_ROWS = 2048
MXU_COLS = 256
DENSE_ROWS = 32
VMEM_LIMIT = 56 * 1024 * 1024

_NT = (((1,), (1,)), ((), ()))
_TN = (((0,), (0,)), ((), ()))


def _params(sem, flags=None):
    return pltpu.CompilerParams(dimension_semantics=sem, vmem_limit_bytes=VMEM_LIMIT, flags=flags)


def _dot(a, b):
    return jnp.dot(a.astype(BF16), b.astype(BF16), preferred_element_type=F32)


def _dot_nt(a, b):
    return lax.dot_general(a.astype(BF16), b.astype(BF16), _NT, preferred_element_type=F32)


def _dot_tn(a, b):
    return lax.dot_general(a.astype(BF16), b.astype(BF16), _TN, preferred_element_type=F32)


def _split3(x):
    hi = x.astype(BF16)
    r1 = x - hi.astype(F32)
    mid = r1.astype(BF16)
    lo = (r1 - mid.astype(F32)).astype(BF16)
    return hi, mid, lo


def _dot_exact_rhs(x, ones_bf16):
    hi = x.astype(BF16)
    lo = (x - hi.astype(F32)).astype(BF16)
    d = lambda p: jnp.dot(p, ones_bf16, preferred_element_type=F32)
    return d(hi) + d(lo)


def _dot_exact_lhs(ones_bf16, x):
    hi, mid, lo = _split3(x)
    d = lambda p: jnp.dot(ones_bf16, p, preferred_element_type=F32)
    return d(hi) + d(mid) + d(lo)


def _norm_matmul_kernel(x_ref, g_ref, w_ref, o_ref, xn_ref):
    @pl.when(pl.program_id(1) == 0)
    def _():
        x = x_ref[...]
        inv = lax.rsqrt(jnp.mean(x * x, axis=-1, keepdims=True) + RMS_EPS)
        xn_ref[...] = (x * inv * g_ref[...]).astype(BF16)

    o_ref[...] = jnp.dot(xn_ref[...], w_ref[...], preferred_element_type=F32)


def _norm_matmul(x, g, w, tm, tn):
    m, k = x.shape
    n = w.shape[1]
    return pl.pallas_call(
        _norm_matmul_kernel,
        grid=(m // tm, n // tn),
        in_specs=[pl.BlockSpec((tm, k), lambda i, j: (i, 0)),
                  pl.BlockSpec((1, k), lambda i, j: (0, 0)),
                  pl.BlockSpec((k, tn), lambda i, j: (0, j))],
        out_specs=pl.BlockSpec((tm, tn), lambda i, j: (i, j)),
        out_shape=jax.ShapeDtypeStruct((m, n), F32),
        scratch_shapes=[pltpu.VMEM((tm, k), BF16)],
        compiler_params=_params(("parallel", "arbitrary")),
        name="in_proj",
    )(x, g, w)


def _moba_kernel(q_ref, k_ref, v_ref, o_ref, kmean_ref, kb_ref, vt_ref, *, nb, heads):
    blk_len = MOBA_BLOCK
    hd = ATTN_HEAD_DIM
    hg = pl.program_id(1)
    qi = pl.program_id(2)
    hrange = range(heads)

    @pl.when(qi == 0)
    def _():
        for hh in hrange:
            hl = slice(hh * hd, (hh + 1) * hd)
            for j in range(nb):
                rows = slice(j * blk_len, (j + 1) * blk_len)
                kj = k_ref[0, rows, hl]
                kmean_ref[hh, j:j + 1, :] = jnp.mean(kj, axis=0, keepdims=True)
                kb_ref[hh, rows, :] = kj.astype(BF16)
                vt_ref[hh, :, rows] = v_ref[0, rows, hl].T.astype(BF16)

    log2e = math.log2(math.e)
    scale2 = (hd ** -0.5) * log2e
    kq = (lax.broadcasted_iota(jnp.int32, (blk_len, blk_len), 1)
          - lax.broadcasted_iota(jnp.int32, (blk_len, blk_len), 0))
    kqf = kq.astype(F32)
    blk = lax.broadcasted_iota(jnp.int32, (nb, blk_len), 0)
    past = blk < qi

    def select(hh):
        q = q_ref[0, :, hh * hd:(hh + 1) * hd]
        gate = lax.dot_general(kmean_ref[hh], q, _NT, precision=lax.Precision.HIGHEST,
                               preferred_element_type=F32)
        g = jnp.where(past, gate, NEG)
        selw = jnp.zeros(gate.shape, F32)
        for _ in range(min(MOBA_TOPK, nb)):
            m = jnp.max(g, axis=0, keepdims=True)
            first = jnp.min(jnp.where(g == m, blk, nb), axis=0, keepdims=True)
            pick = blk == first
            selw = jnp.where(pick, 1.0, selw)
            g = jnp.where(pick, -jnp.inf, g)
        return jnp.where(past, selw, 0.0), (q * scale2).astype(BF16)

    sel_q = [select(hh) for hh in hrange]
    selw = [x[0] for x in sel_q]
    qb = [x[1] for x in sel_q]
    slope2 = [jnp.exp2(jnp.zeros((1, 1), F32) - (hg * heads + hh + 1).astype(F32)) * log2e
              for hh in hrange]
    bias0 = [slope2[hh] * kqf for hh in hrange]

    def scores(j):
        r0 = pl.multiple_of(j * blk_len, blk_len)
        return [lax.dot_general(kb_ref[hh, pl.ds(r0, blk_len), :], qb[hh], _NT,
                                preferred_element_type=F32) - bias0[hh] for hh in hrange]

    def weighted_values(j, p):
        r0 = pl.multiple_of(j * blk_len, blk_len)
        return [jnp.dot(vt_ref[hh, :, pl.ds(r0, blk_len)], p[hh].astype(BF16),
                        preferred_element_type=F32) for hh in hrange]

    s = [jnp.where(kq >= 0, x, NEG) for x in scores(qi)]
    m = [jnp.max(x, axis=0, keepdims=True) for x in s]
    p = [jnp.exp2(s[hh] - m[hh]) for hh in hrange]
    l = [jnp.sum(x, axis=0, keepdims=True) for x in p]
    acc = weighted_values(qi, p)

    def body(j, carry):
        m, l, acc = carry
        off = ((qi - j) * blk_len).astype(F32)
        s = scores(j)
        selj = [jnp.max(jnp.where(blk == j, selw[hh], 0.0), axis=0, keepdims=True) > 0.0
                for hh in hrange]
        shift = [slope2[hh] * off for hh in hrange]
        top = [jnp.max(s[hh], axis=0, keepdims=True) - shift[hh] for hh in hrange]
        m_new = [jnp.maximum(m[hh], jnp.where(selj[hh], top[hh], NEG)) for hh in hrange]
        alpha = [jnp.exp2(m[hh] - m_new[hh]) for hh in hrange]
        p = [jnp.exp2(s[hh] - jnp.where(selj[hh], m_new[hh] + shift[hh], jnp.inf)) for hh in hrange]
        l = [alpha[hh] * l[hh] + jnp.sum(p[hh], axis=0, keepdims=True) for hh in hrange]
        pv = weighted_values(j, p)
        acc = [alpha[hh] * acc[hh] + pv[hh] for hh in hrange]
        return m_new, l, acc

    m, l, acc = lax.fori_loop(0, qi, body, (m, l, acc))
    for hh in hrange:
        o_ref[0, :, hh * hd:(hh + 1) * hd] = (acc[hh] / l[hh]).T


def _moba(proj, batch, seq, heads):
    nb = seq // MOBA_BLOCK
    hd = ATTN_HEAD_DIM
    hw = heads * hd
    groups = ATTN_HEADS // heads
    kern = lambda *refs: _moba_kernel(*refs, nb=nb, heads=heads)
    return pl.pallas_call(
        kern,
        grid=(batch, groups, nb),
        in_specs=[pl.BlockSpec((1, MOBA_BLOCK, hw), lambda b, h, i: (b, i, h)),
                  pl.BlockSpec((1, seq, hw), lambda b, h, i: (b, 0, groups + h)),
                  pl.BlockSpec((1, seq, hw), lambda b, h, i: (b, 0, 2 * groups + h))],
        out_specs=pl.BlockSpec((1, MOBA_BLOCK, hw), lambda b, h, i: (b, i, h)),
        out_shape=jax.ShapeDtypeStruct((batch, seq, ATTN_WIDTH), F32),
        scratch_shapes=[pltpu.VMEM((heads, nb, hd), F32),
                        pltpu.VMEM((heads, seq, hd), BF16),
                        pltpu.VMEM((heads, hd, seq), BF16)],
        compiler_params=_params(("parallel", "parallel", "arbitrary")),
        name="moba",
    )(proj, proj, proj)


def _rwkv_prep_kernel(r_ref, k_ref, v_ref, l_ref, rp_ref, kp_ref, vp_ref, lp_ref,
                      mu_r, mu_k, mu_v, mu_l, w0, w2p, a0, a2p, g2p, kk_w, ka_w, rk_w, bd_ref,
                      r_o, lw_o, k_o, v_o, kk_o, b_o, g_o, bonus_o):
    first = pl.program_id(1) == 0

    def shift(cur_ref, prev_ref, mu_ref):
        cur = cur_ref[0]
        prev_last = jnp.where(first, 0.0, prev_ref[0, 7:8, :])
        rows = lax.broadcasted_iota(jnp.int32, cur.shape, 0)
        prev = jnp.where(rows == 0, prev_last, pltpu.roll(cur, 1, axis=0))
        return cur + (prev - cur) * mu_ref[...]

    r = shift(r_ref, rp_ref, mu_r)
    k = shift(k_ref, kp_ref, mu_k)
    v = shift(v_ref, vp_ref, mu_v)
    lo = shift(l_ref, lp_ref, mu_l)

    bd = bd_ref[...]
    z = -(w0[...] + _dot(jnp.tanh(lo), w2p[...]))
    softplus = jnp.maximum(z, 0.0) + jnp.log1p(jnp.exp(-jnp.abs(z)))
    lw = -jnp.exp(-softplus - 0.5)
    a = jax.nn.sigmoid(a0[...] + _dot(lo, a2p[...]))
    g = _dot(jax.nn.sigmoid(lo), g2p[...])
    kk = k * kk_w[...]
    kk = kk / jnp.maximum(jnp.sqrt(_dot_exact_rhs(kk * kk, bd)), 1e-12)
    k2 = k * (1.0 + (a - 1.0) * ka_w[...])
    r_o[0] = r
    lw_o[0] = lw
    k_o[0] = k2
    v_o[0] = v
    kk_o[0] = kk
    b_o[0] = kk * a
    g_o[0] = g
    bonus_o[0] = _dot_exact_rhs(r * k2 * rk_w[...], bd) * v


def _rwkv_prep(proj, batch, seq, weights, tt):
    w = RWKV_WIDTH
    c0 = 3 * ATTN_WIDTH // w
    lc = (3 * ATTN_WIDTH + 3 * w) // LORA_PAD
    pb = tt // 8
    cur = lambda width, col: pl.BlockSpec((1, tt, width), lambda b, i: (b, i, col))
    prev = lambda width, col: pl.BlockSpec(
        (1, 8, width), lambda b, i: (b, jnp.maximum(i * pb - 1, 0), col))
    full = lambda a: pl.BlockSpec(a.shape, lambda b, i: (0,) * a.ndim)
    out = pl.BlockSpec((1, tt, w), lambda b, i: (b, i, 0))
    return pl.pallas_call(
        _rwkv_prep_kernel,
        grid=(batch, seq // tt),
        in_specs=[cur(w, c0), cur(w, c0 + 1), cur(w, c0 + 2), cur(LORA_PAD, lc),
                  prev(w, c0), prev(w, c0 + 1), prev(w, c0 + 2), prev(LORA_PAD, lc)]
                 + [full(a) for a in weights],
        out_specs=[out] * 8,
        out_shape=[jax.ShapeDtypeStruct((batch, seq, w), F32)] * 8,
        compiler_params=_params(("parallel", "arbitrary")),
        name="rwkv_prep",
    )(proj, proj, proj, proj, proj, proj, proj, proj, *weights)


def _scan_kernel(r_ref, lw_ref, k_ref, v_ref, kk_ref, b_ref, y_ref, s_ref, *, tb, pairs):
    c = SCAN_CHUNK
    c2 = 2 * c

    @pl.when(pl.program_id(1) == 0)
    def _():
        s_ref[...] = jnp.zeros_like(s_ref)

    head0 = lax.broadcasted_iota(jnp.int32, (c, LANES), 1) < RWKV_HEAD_DIM
    row = lax.broadcasted_iota(jnp.int32, (c2, c2), 0)
    col = lax.broadcasted_iota(jnp.int32, (c2, c2), 1)
    same_head = (row // c) == (col // c)
    strict = jnp.logical_and(same_head, col < row)
    incl = jnp.logical_and(same_head, col <= row)
    inv_blk = (row // SCAN_INV_BLOCK) == (col // SCAN_INV_BLOCK)
    eye = jnp.where(row == col, 1.0, 0.0).astype(F32)
    tri = jnp.where(lax.broadcasted_iota(jnp.int32, (c, c), 0)
                    >= lax.broadcasted_iota(jnp.int32, (c, c), 1), 1.0, 0.0).astype(BF16)

    def stack(x):
        return jnp.concatenate([jnp.where(head0, x, 0.0), jnp.where(head0, 0.0, x)], axis=0)

    def pair_chunk(t0, p):
        lanes = slice(p * LANES, (p + 1) * LANES)
        rows = pl.ds(t0, c)
        lw = lw_ref[0, rows, lanes]
        cum = _dot_exact_lhs(tri, lw)
        yield
        r = r_ref[0, rows, lanes]
        k = k_ref[0, rows, lanes]
        kk = kk_ref[0, rows, lanes]
        b = b_ref[0, rows, lanes]
        cum_end = cum[c - 1:c, :]
        e_neg = jnp.exp(-cum)
        e_end = jnp.exp(cum_end - cum)
        a_s = stack(-kk * jnp.exp(cum - lw))
        r_s = stack(r * jnp.exp(cum))
        b_s = stack(b * e_neg)
        k_s = stack(k * e_neg)
        bend_s = stack(b * e_end)
        kend_s = stack(k * e_end)
        v_s = stack(v_ref[0, rows, lanes])
        gram = _dot_nt(jnp.concatenate([a_s, r_s], axis=0),
                       jnp.concatenate([b_s, k_s], axis=0))
        kv = _dot_tn(v_s, kend_s)
        yield
        l_ab = jnp.where(strict, gram[:c2, :c2], 0.0)
        l_ak = jnp.where(strict, gram[:c2, c2:], 0.0)
        l_rb = jnp.where(incl, gram[c2:, :c2], 0.0)
        l_rk = jnp.where(incl, gram[c2:, c2:], 0.0)

        x = jnp.where(inv_blk, l_ab, 0.0)
        off = jnp.where(inv_blk, 0.0, l_ab)
        dinv = eye + x
        w1 = _dot(l_ak, v_s)
        y1 = _dot(l_rk, v_s)
        for _ in range(int(math.log2(SCAN_INV_BLOCK)) - 1):
            x = _dot(x, x)
            yield
            dinv = dinv + _dot(dinv, x)
            yield
        f = _dot(dinv, off)
        yield
        f2 = _dot(f, f)
        yield
        gm = eye + f
        gm = gm + _dot(gm, f2)
        yield
        t_inv = _dot(gm, dinv)
        yield
        ta = _dot(t_inv, jnp.concatenate([a_s, w1], axis=1))
        yield
        a_hat = ta[:, :LANES]
        v_hat = ta[:, LANES:]
        state = s_ref[p]
        m1 = _dot_nt(jnp.concatenate([a_hat, r_s], axis=0), state)
        yield
        u = m1[:c2] + v_hat
        y = m1[c2:] + y1 + _dot(l_rb, u)
        s_ref[p] = state * jnp.exp(cum_end) + kv + _dot_tn(u, bend_s)
        y_ref[0, rows, lanes] = y[:c] + y[c:]

    def chunk(ci, carry):
        t0 = pl.multiple_of(ci * c, c)
        for _ in itertools.zip_longest(*[pair_chunk(t0, p) for p in range(pairs)]):
            pass
        return carry

    lax.fori_loop(0, tb // c, chunk, 0)


def _rwkv_scan(r, lw, k, v, kk, b, tb):
    batch, seq, w = r.shape
    pairs = w // LANES
    spec = pl.BlockSpec((1, tb, w), lambda bi, t: (bi, t, 0))
    kern = lambda *refs: _scan_kernel(*refs, tb=tb, pairs=pairs)
    return pl.pallas_call(
        kern,
        grid=(batch, seq // tb),
        in_specs=[spec] * 6,
        out_specs=spec,
        out_shape=jax.ShapeDtypeStruct((batch, seq, w), F32),
        scratch_shapes=[pltpu.VMEM((pairs, LANES, LANES), F32)],
        compiler_params=_params(("parallel", "arbitrary")),
        name="rwkv_scan",
    )(r, lw, k, v, kk, b)


def _out_proj_kernel(x_ref, attn_ref, y_ref, bonus_ref, g_ref, lng, lnb, bd_ref, w_ref, n2g,
                     h_ref, hn_ref):
    bd = bd_ref[...]
    y = y_ref[...]
    inv_n = 1.0 / RWKV_HEAD_DIM
    d = y - _dot_exact_rhs(y, bd) * inv_n
    var = _dot_exact_rhs(d * d, bd) * inv_n
    yn = d * lax.rsqrt(var + RWKV_GN_EPS) * lng[...] + lnb[...]
    rw = (yn + bonus_ref[...]) * g_ref[...]
    h = (x_ref[...] + _dot(attn_ref[...], w_ref[:ATTN_WIDTH, :])
         + _dot(rw, w_ref[ATTN_WIDTH:, :]))
    h_ref[...] = h
    inv = lax.rsqrt(jnp.mean(h * h, axis=-1, keepdims=True) + RMS_EPS)
    hn_ref[...] = (h * inv * n2g[...]).astype(BF16)


def _out_proj(x, attn, y, bonus, g, lng, lnb, bd, w_out, n2g, tm):
    m = x.shape[0]
    row = lambda width: pl.BlockSpec((tm, width), lambda i: (i, 0))
    full = lambda a: pl.BlockSpec(a.shape, lambda i: (0,) * a.ndim)
    return pl.pallas_call(
        _out_proj_kernel,
        grid=(m // tm,),
        in_specs=[row(D_MODEL), row(ATTN_WIDTH), row(RWKV_WIDTH), row(RWKV_WIDTH), row(RWKV_WIDTH),
                  full(lng), full(lnb), full(bd), full(w_out), full(n2g)],
        out_specs=[row(D_MODEL), row(D_MODEL)],
        out_shape=[jax.ShapeDtypeStruct((m, D_MODEL), F32),
                   jax.ShapeDtypeStruct((m, D_MODEL), BF16)],
        compiler_params=_params(("parallel",)),
        name="out_proj",
    )(x, attn, y, bonus, g, lng, lnb, bd, w_out, n2g)


def _peer_q_kernel(hn_ref, wq_ref, keys_ref, st_ref):
    q = jnp.dot(hn_ref[...], wq_ref[...], preferred_element_type=F32)
    for hp in range(2 * PEER_HEADS):
        qs = q[:, hp * PEER_HALF:(hp + 1) * PEER_HALF]
        st_ref[hp] = _dot_nt(keys_ref[hp], qs)


def _peer_q(hn, wq, keys, tm):
    m = hn.shape[0]
    return pl.pallas_call(
        _peer_q_kernel,
        grid=(m // tm,),
        in_specs=[pl.BlockSpec((tm, D_MODEL), lambda i: (i, 0)),
                  pl.BlockSpec(wq.shape, lambda i: (0, 0)),
                  pl.BlockSpec(keys.shape, lambda i: (0, 0, 0))],
        out_specs=pl.BlockSpec((2 * PEER_HEADS, PEER_NKEYS, tm), lambda i: (0, 0, i)),
        out_shape=jax.ShapeDtypeStruct((2 * PEER_HEADS, PEER_NKEYS, m), F32),
        compiler_params=_params(("parallel",)),
        name="peer_q",
    )(hn, wq, keys)


def _top_desc(x, k, exact_ties):
    n = x.shape[0]
    rows = lax.broadcasted_iota(jnp.int32, x.shape, 0)
    rank = jnp.full(x.shape, float(k), F32)
    outs = []
    for i in range(k):
        m = jnp.max(x, axis=0, keepdims=True)
        outs.append(m)
        pick = x == m
        if exact_ties:
            pick = rows == jnp.min(jnp.where(pick, rows, n), axis=0, keepdims=True)
        x = jnp.where(pick, -jnp.inf, x)
        rank = jnp.where(pick, float(i), rank)
    gone = jnp.sum(jnp.where(x == -jnp.inf, 1.0, 0.0), axis=0, keepdims=True)
    return jnp.concatenate(outs, axis=0), rank, gone


def _peer_topk_kernel(s_ref, cnt1_ref, e1_ref, rank2_ref, e2_ref, *, n1):
    k = PEER_TOPK
    widths = [k // (a + 1) for a in range(k)]
    pad = -sum(widths) % 8

    def run(exact_ties):
        s1 = s_ref[0]
        s2 = s_ref[1]
        sv1, rank1, gone1 = _top_desc(s1, k, exact_ties)
        sv2, rank2, gone2 = _top_desc(s2, k, exact_ties)
        cand = jnp.concatenate([sv1[a:a + 1] + sv2[:widths[a]] for a in range(k)]
                               + [jnp.full((pad, s1.shape[1]), -jnp.inf, F32)], axis=0)
        best, pos, gone3 = _top_desc(cand, k, exact_ties)
        chosen = jnp.where(pos < float(k), 1.0, 0.0)
        z = jnp.sum(jnp.exp(best - best[0:1]), axis=0, keepdims=True)
        cnt1 = jnp.zeros(s1.shape, F32)
        row = 0
        for a in range(k):
            cnt_a = jnp.sum(chosen[row:row + widths[a]], axis=0, keepdims=True)
            cnt1 = jnp.where(rank1 == float(a), cnt_a, cnt1)
            row += widths[a]
        e1 = jnp.exp(s1 - sv1[0:1]) / z
        for t in range(PEER_NKEYS // n1):
            cnt1_ref[0, t] = cnt1[t * n1:(t + 1) * n1]
            e1_ref[0, t] = e1[t * n1:(t + 1) * n1]
        rank2_ref[0] = rank2
        e2_ref[0] = jnp.exp(s2 - sv2[0:1])
        return jnp.max(jnp.maximum(jnp.maximum(gone1, gone2), gone3 - float(pad)))

    most_gone = run(exact_ties=False)

    @pl.when(most_gone > float(k))
    def _():
        run(exact_ties=True)


def _peer_topk(st, tl, n1):
    m = st.shape[2]
    n_tiles = PEER_NKEYS // n1
    spec = pl.BlockSpec((1, PEER_NKEYS, tl), lambda h, i: (h, 0, i))
    shape = jax.ShapeDtypeStruct((PEER_HEADS, PEER_NKEYS, m), F32)
    tile_spec = pl.BlockSpec((1, n_tiles, n1, tl), lambda h, i: (h, 0, 0, i))
    tile_shape = jax.ShapeDtypeStruct((PEER_HEADS, n_tiles, n1, m), F32)
    kern = lambda *refs: _peer_topk_kernel(*refs, n1=n1)
    return pl.pallas_call(
        kern,
        grid=(PEER_HEADS, m // tl),
        in_specs=[pl.BlockSpec((2, PEER_NKEYS, tl), lambda h, i: (h, 0, i))],
        out_specs=[tile_spec, tile_spec, spec, spec],
        out_shape=[tile_shape, tile_shape, shape, shape],
        compiler_params=_params(("parallel", "parallel")),
        name="peer_topk",
    )(st)


def _peer_dense_kernel(u_ref, hnt_ref, vt_ref, rank2_ref, e2_ref, cnta_ref, cntb_ref,
                       e1a_ref, e1b_ref, yt_ref, act_ref, hpre_ref, *, n1):
    step = pl.program_id(1)
    last_step = pl.num_programs(1) - 1
    et = hpre_ref.shape[1]
    row_refs = ((cnta_ref, e1a_ref), (cntb_ref, e1b_ref))

    def gate_tile(half, jt, ib):
        prev = 1 - half
        cnt_ref, e1r_ref = row_refs[half]
        ls = slice(jt * LANES, (jt + 1) * LANES)
        rs = slice(ib * DENSE_ROWS, (ib + 1) * DENSE_ROWS)
        gates = [jnp.zeros((DENSE_ROWS, LANES), F32) for _ in range(n1)]
        for h in range(PEER_HEADS):
            r2 = rank2_ref[h, rs, ls]
            e2 = e2_ref[h, rs, ls]
            for a in range(n1):
                hit = r2 < cnt_ref[h, 0, a:a + 1, ls]
                gates[a] = gates[a] + jnp.where(hit, e2, 0.0) * e1r_ref[h, 0, a:a + 1, ls]
        for a in range(n1):
            ers = slice(a * PEER_NKEYS + ib * DENSE_ROWS, a * PEER_NKEYS + (ib + 1) * DENSE_ROWS)
            x = hpre_ref[prev, ers, ls]
            gelu = 0.5 * x * (1.0 + lax.erf(x * (0.5 ** 0.5)))
            act_ref[prev, ers, ls] = gelu * gates[a]

    tt = hpre_ref.shape[2]
    n_ib = PEER_NKEYS // DENSE_ROWS

    def run(first, last):
        for half in range(2):
            cur = half
            hs = slice(half * et, (half + 1) * et)
            gated = not (first and half == 0) and not (last and half == 1)
            for piece in range(tt // MXU_COLS):
                cs = slice(piece * MXU_COLS, (piece + 1) * MXU_COLS)
                tiles = [(jt, ib)
                         for jt in range(piece * MXU_COLS // LANES, (piece + 1) * MXU_COLS // LANES)
                         for ib in range(n_ib)]
                if not last:
                    hpre_ref[cur, :, cs] = jnp.dot(u_ref[hs, :], hnt_ref[:, cs],
                                                   preferred_element_type=F32)
                per_down = len(tiles) * DOWN_ROWS // D_MODEL
                for i, tile in enumerate(tiles):
                    if not first and i % per_down == per_down // 2:
                        ms = slice(i // per_down * DOWN_ROWS, (i // per_down + 1) * DOWN_ROWS)
                        yt_ref[ms, cs] += jnp.dot(vt_ref[ms, hs],
                                                  act_ref[cur, :, cs].astype(BF16),
                                                  preferred_element_type=F32)
                    if gated:
                        gate_tile(half, *tile)

    @pl.when(step == 0)
    def _():
        yt_ref[...] = jnp.zeros_like(yt_ref)
        run(first=True, last=False)

    @pl.when(jnp.logical_and(step > 0, step < last_step))
    def _():
        run(first=False, last=False)

    @pl.when(step == last_step)
    def _():
        run(first=False, last=True)


def _peer_dense(u, hnt, vt, cntr, e1r, rank2, e2, tt, n1):
    m = hnt.shape[1]
    et = n1 * PEER_NKEYS
    n_steps = PEER_NKEYS // (2 * n1)
    n_tiles = 2 * n_steps
    tok3 = lambda a: pl.BlockSpec((a.shape[0], a.shape[1], tt), lambda i, e: (0, 0, i))
    rows = lambda half: pl.BlockSpec(
        (PEER_HEADS, 1, n1, tt),
        lambda i, e: (0, jnp.clip(2 * e + half - 1, 0, n_tiles - 1), 0, i))
    kern = lambda *refs: _peer_dense_kernel(*refs, n1=n1)
    return pl.pallas_call(
        kern,
        grid=(m // tt, n_steps + 1),
        in_specs=[pl.BlockSpec((2 * et, D_MODEL), lambda i, e: (jnp.minimum(e, n_steps - 1), 0)),
                  pl.BlockSpec((D_MODEL, tt), lambda i, e: (0, i)),
                  pl.BlockSpec((D_MODEL, 2 * et), lambda i, e: (0, jnp.maximum(e - 1, 0))),
                  tok3(rank2), tok3(e2), rows(0), rows(1), rows(0), rows(1)],
        out_specs=pl.BlockSpec((D_MODEL, tt), lambda i, e: (0, i)),
        out_shape=jax.ShapeDtypeStruct((D_MODEL, m), F32),
        scratch_shapes=[pltpu.VMEM((2, et, tt), F32), pltpu.VMEM((2, et, tt), F32)],
        compiler_params=_params(("parallel", "arbitrary")),
        name="peer_dense",
    )(u, hnt, vt, rank2, e2, cntr, cntr, e1r, e1r)


def _final_kernel(h_ref, yt_ref, g_ref, o_ref):
    h = h_ref[...] + yt_ref[...].T
    inv = lax.rsqrt(jnp.mean(h * h, axis=-1, keepdims=True) + RMS_EPS)
    o_ref[...] = h * inv * g_ref[...]


def _final(h, yt, g, tm):
    m = h.shape[0]
    row = pl.BlockSpec((tm, D_MODEL), lambda i: (i, 0))
    return pl.pallas_call(
        _final_kernel,
        grid=(m // tm,),
        in_specs=[row, pl.BlockSpec((D_MODEL, tm), lambda i: (0, i)),
                  pl.BlockSpec((1, D_MODEL), lambda i: (0, 0))],
        out_specs=row,
        out_shape=jax.ShapeDtypeStruct((m, D_MODEL), F32),
        compiler_params=_params(("parallel",)),
        name="final_norm",
    )(h, yt, g)


def _row(a):
    return a.reshape(1, -1).astype(F32)


def _pad_rows(w, start, total):
    return jnp.zeros((total, w.shape[1]), BF16).at[start:start + w.shape[0]].set(w.astype(BF16))


def kernel(x, norm1_g, w_in, rwkv_mu, rwkv_w0, rwkv_w2, rwkv_a0, rwkv_a2, rwkv_g2, rwkv_k_k, rwkv_k_a, rwkv_r_k, rwkv_ln_g, rwkv_ln_b, w_out, norm2_g, peer_wq, peer_sub_keys, peer_u, peer_v, final_g):
    batch, seq, d = x.shape
    tokens = batch * seq
    w = RWKV_WIDTH
    xt = x.reshape(tokens, d)

    w_in_p = jnp.pad(w_in[0].astype(BF16), ((0, 0), (0, IN_COLS_PAD - IN_COLS)))
    proj = _norm_matmul(xt, _row(norm1_g[0]), w_in_p, tm=1024, tn=512)
    proj = proj.reshape(batch, seq, IN_COLS_PAD)

    attn = _moba(proj, batch, seq, heads=4)

    mu = rwkv_mu[0]
    mu_l = jnp.pad(mu[3 * w:], (0, LORA_PAD - LORA_COLS))
    hid = lax.broadcasted_iota(jnp.int32, (w, w), 0) // RWKV_HEAD_DIM
    bd = (hid == hid.T).astype(BF16)
    prep_w = [_row(mu[:w]), _row(mu[w:2 * w]), _row(mu[2 * w:3 * w]), _row(mu_l),
              _row(rwkv_w0[0]), _pad_rows(rwkv_w2[0], 0, LORA_PAD),
              _row(rwkv_a0[0]), _pad_rows(rwkv_a2[0], DECAY_LORA, LORA_PAD),
              _pad_rows(rwkv_g2[0], DECAY_LORA + AAA_LORA, LORA_PAD),
              _row(rwkv_k_k[0]), _row(rwkv_k_a[0]), _row(rwkv_r_k[0]), bd]
    r, lw, k2, v, kk, b, g, bonus = _rwkv_prep(proj, batch, seq, prep_w, tt=256)
    y = _rwkv_scan(r, lw, k2, v, kk, b, tb=256)

    flat = lambda a: a.reshape(tokens, a.shape[-1])
    h, hn = _out_proj(xt, flat(attn), flat(y), flat(bonus), flat(g), _row(rwkv_ln_g[0]),
                      _row(rwkv_ln_b[0]), bd, w_out[0].astype(BF16), _row(norm2_g[0]), tm=256)

    keys = peer_sub_keys[0].reshape(2 * PEER_HEADS, PEER_NKEYS, PEER_HALF).astype(BF16)
    st = _peer_q(hn, peer_wq[0].astype(BF16), keys, tm=512)
    cnt1, e1, rank2, e2 = _peer_topk(st, tl=512, n1=4)
    yt = _peer_dense(peer_u[0].astype(BF16), hn.T, peer_v[0].astype(BF16).T, cnt1, e1, rank2, e2,
                     tt=512, n1=4)
    out = _final(h, yt, final_g.reshape(1, d), tm=512)
    return out.reshape(batch, seq, d)
```

```python
import itertools
import math

import jax
import jax.numpy as jnp
from jax import lax
from jax.experimental import pallas as pl
from jax.experimental.pallas import tpu as pltpu

F32 = jnp.float32
BF16 = jnp.bfloat16

D_MODEL = 2048
ATTN_HEADS = 8
ATTN_HEAD_DIM = 128
ATTN_WIDTH = ATTN_HEADS * ATTN_HEAD_DIM
MOBA_BLOCK = 256
MOBA_TOPK = 3

RWKV_HEAD_DIM = 64
RWKV_WIDTH = D_MODEL - ATTN_WIDTH
RWKV_HEADS = RWKV_WIDTH // RWKV_HEAD_DIM
DECAY_LORA = 96
AAA_LORA = 96
GATE_LORA = 256
LORA_COLS = DECAY_LORA + AAA_LORA + GATE_LORA
LORA_PAD = 512
RWKV_GN_EPS = RWKV_HEAD_DIM * 1e-5
IN_COLS = 3 * ATTN_WIDTH + 3 * RWKV_WIDTH + LORA_COLS
IN_COLS_PAD = 3 * ATTN_WIDTH + 3 * RWKV_WIDTH + LORA_PAD

PEER_HEADS = 8
PEER_NKEYS = 128
PEER_HALF = 128
PEER_TOPK = 16

RMS_EPS = 1e-6
NEG = -1e30

LANES = 128
SCAN_CHUNK = 64
SCAN_INV_BLOCK = 16
DOWN_ROWS = 2048
MXU_COLS = 256
DENSE_ROWS = 32
VMEM_LIMIT = 56 * 1024 * 1024

_NT = (((1,), (1,)), ((), ()))
_TN = (((0,), (0,)), ((), ()))


def _params(sem, flags=None):
    return pltpu.CompilerParams(dimension_semantics=sem, vmem_limit_bytes=VMEM_LIMIT, flags=flags)


def _dot(a, b):
    return jnp.dot(a.astype(BF16), b.astype(BF16), preferred_element_type=F32)


def _dot_nt(a, b):
    return lax.dot_general(a.astype(BF16), b.astype(BF16), _NT, preferred_element_type=F32)


def _dot_tn(a, b):
    return lax.dot_general(a.astype(BF16), b.astype(BF16), _TN, preferred_element_type=F32)


def _split3(x):
    hi = x.astype(BF16)
    r1 = x - hi.astype(F32)
    mid = r1.astype(BF16)
    lo = (r1 - mid.astype(F32)).astype(BF16)
    return hi, mid, lo


def _dot_exact_rhs(x, ones_bf16):
    hi = x.astype(BF16)
    lo = (x - hi.astype(F32)).astype(BF16)
    d = lambda p: jnp.dot(p, ones_bf16, preferred_element_type=F32)
    return d(hi) + d(lo)


def _dot_exact_lhs(ones_bf16, x):
    hi, mid, lo = _split3(x)
    d = lambda p: jnp.dot(ones_bf16, p, preferred_element_type=F32)
    return d(hi) + d(mid) + d(lo)


def _norm_matmul_kernel(x_ref, g_ref, w_ref, o_ref, xn_ref):
    @pl.when(pl.program_id(1) == 0)
    def _():
        x = x_ref[...]
        inv = lax.rsqrt(jnp.mean(x * x, axis=-1, keepdims=True) + RMS_EPS)
        xn_ref[...] = (x * inv * g_ref[...]).astype(BF16)

    o_ref[...] = jnp.dot(xn_ref[...], w_ref[...], preferred_element_type=F32)


def _norm_matmul(x, g, w, tm, tn):
    m, k = x.shape
    n = w.shape[1]
    return pl.pallas_call(
        _norm_matmul_kernel,
        grid=(m // tm, n // tn),
        in_specs=[pl.BlockSpec((tm, k), lambda i, j: (i, 0)),
                  pl.BlockSpec((1, k), lambda i, j: (0, 0)),
                  pl.BlockSpec((k, tn), lambda i, j: (0, j))],
        out_specs=pl.BlockSpec((tm, tn), lambda i, j: (i, j)),
        out_shape=jax.ShapeDtypeStruct((m, n), F32),
        scratch_shapes=[pltpu.VMEM((tm, k), BF16)],
        compiler_params=_params(("parallel", "arbitrary")),
        name="in_proj",
    )(x, g, w)


def _moba_kernel(q_ref, k_ref, v_ref, o_ref, kmean_ref, kb_ref, vt_ref, *, nb, heads):
    blk_len = MOBA_BLOCK
    hd = ATTN_HEAD_DIM
    hg = pl.program_id(1)
    qi = pl.program_id(2)
    hrange = range(heads)

    @pl.when(qi == 0)
    def _():
        for hh in hrange:
            hl = slice(hh * hd, (hh + 1) * hd)
            for j in range(nb):
                rows = slice(j * blk_len, (j + 1) * blk_len)
                kj = k_ref[0, rows, hl]
                kmean_ref[hh, j:j + 1, :] = jnp.mean(kj, axis=0, keepdims=True)
                kb_ref[hh, rows, :] = kj.astype(BF16)
                vt_ref[hh, :, rows] = v_ref[0, rows, hl].T.astype(BF16)

    log2e = math.log2(math.e)
    scale2 = (hd ** -0.5) * log2e
    kq = (lax.broadcasted_iota(jnp.int32, (blk_len, blk_len), 1)
          - lax.broadcasted_iota(jnp.int32, (blk_len, blk_len), 0))
    kqf = kq.astype(F32)
    blk = lax.broadcasted_iota(jnp.int32, (nb, blk_len), 0)
    past = blk < qi

    def select(hh):
        q = q_ref[0, :, hh * hd:(hh + 1) * hd]
        gate = lax.dot_general(kmean_ref[hh], q, _NT, precision=lax.Precision.HIGHEST,
                               preferred_element_type=F32)
        g = jnp.where(past, gate, NEG)
        selw = jnp.zeros(gate.shape, F32)
        for _ in range(min(MOBA_TOPK, nb)):
            m = jnp.max(g, axis=0, keepdims=True)
            first = jnp.min(jnp.where(g == m, blk, nb), axis=0, keepdims=True)
            pick = blk == first
            selw = jnp.where(pick, 1.0, selw)
            g = jnp.where(pick, -jnp.inf, g)
        return jnp.where(past, selw, 0.0), (q * scale2).astype(BF16)

    sel_q = [select(hh) for hh in hrange]
    selw = [x[0] for x in sel_q]
    qb = [x[1] for x in sel_q]
    slope2 = [jnp.exp2(jnp.zeros((1, 1), F32) - (hg * heads + hh + 1).astype(F32)) * log2e
              for hh in hrange]
    bias0 = [slope2[hh] * kqf for hh in hrange]

    def scores(j):
        r0 = pl.multiple_of(j * blk_len, blk_len)
        return [lax.dot_general(kb_ref[hh, pl.ds(r0, blk_len), :], qb[hh], _NT,
                                preferred_element_type=F32) - bias0[hh] for hh in hrange]

    def weighted_values(j, p):
        r0 = pl.multiple_of(j * blk_len, blk_len)
        return [jnp.dot(vt_ref[hh, :, pl.ds(r0, blk_len)], p[hh].astype(BF16),
                        preferred_element_type=F32) for hh in hrange]

    s = [jnp.where(kq >= 0, x, NEG) for x in scores(qi)]
    m = [jnp.max(x, axis=0, keepdims=True) for x in s]
    p = [jnp.exp2(s[hh] - m[hh]) for hh in hrange]
    l = [jnp.sum(x, axis=0, keepdims=True) for x in p]
    acc = weighted_values(qi, p)

    def body(j, carry):
        m, l, acc = carry
        off = ((qi - j) * blk_len).astype(F32)
        s = scores(j)
        selj = [jnp.max(jnp.where(blk == j, selw[hh], 0.0), axis=0, keepdims=True) > 0.0
                for hh in hrange]
        shift = [slope2[hh] * off for hh in hrange]
        top = [jnp.max(s[hh], axis=0, keepdims=True) - shift[hh] for hh in hrange]
        m_new = [jnp.maximum(m[hh], jnp.where(selj[hh], top[hh], NEG)) for hh in hrange]
        alpha = [jnp.exp2(m[hh] - m_new[hh]) for hh in hrange]
        p = [jnp.exp2(s[hh] - jnp.where(selj[hh], m_new[hh] + shift[hh], jnp.inf)) for hh in hrange]
        l = [alpha[hh] * l[hh] + jnp.sum(p[hh], axis=0, keepdims=True) for hh in hrange]
        pv = weighted_values(j, p)
        acc = [alpha[hh] * acc[hh] + pv[hh] for hh in hrange]
        return m_new, l, acc

    m, l, acc = lax.fori_loop(0, qi, body, (m, l, acc))
    for hh in hrange:
        o_ref[0, :, hh * hd:(hh + 1) * hd] = (acc[hh] / l[hh]).T


def _moba(proj, batch, seq, heads):
    nb = seq // MOBA_BLOCK
    hd = ATTN_HEAD_DIM
    hw = heads * hd
    groups = ATTN_HEADS // heads
    kern = lambda *refs: _moba_kernel(*refs, nb=nb, heads=heads)
    return pl.pallas_call(
        kern,
        grid=(batch, groups, nb),
        in_specs=[pl.BlockSpec((1, MOBA_BLOCK, hw), lambda b, h, i: (b, i, h)),
                  pl.BlockSpec((1, seq, hw), lambda b, h, i: (b, 0, groups + h)),
                  pl.BlockSpec((1, seq, hw), lambda b, h, i: (b, 0, 2 * groups + h))],
        out_specs=pl.BlockSpec((1, MOBA_BLOCK, hw), lambda b, h, i: (b, i, h)),
        out_shape=jax.ShapeDtypeStruct((batch, seq, ATTN_WIDTH), F32),
        scratch_shapes=[pltpu.VMEM((heads, nb, hd), F32),
                        pltpu.VMEM((heads, seq, hd), BF16),
                        pltpu.VMEM((heads, hd, seq), BF16)],
        compiler_params=_params(("parallel", "parallel", "arbitrary")),
        name="moba",
    )(proj, proj, proj)


def _rwkv_prep_kernel(r_ref, k_ref, v_ref, l_ref, rp_ref, kp_ref, vp_ref, lp_ref,
                      mu_r, mu_k, mu_v, mu_l, w0, w2p, a0, a2p, g2p, kk_w, ka_w, rk_w, bd_ref,
                      r_o, lw_o, k_o, v_o, kk_o, b_o, g_o, bonus_o):
    first = pl.program_id(1) == 0

    def shift(cur_ref, prev_ref, mu_ref):
        cur = cur_ref[0]
        prev_last = jnp.where(first, 0.0, prev_ref[0, 7:8, :])
        rows = lax.broadcasted_iota(jnp.int32, cur.shape, 0)
        prev = jnp.where(rows == 0, prev_last, pltpu.roll(cur, 1, axis=0))
        return cur + (prev - cur) * mu_ref[...]

    r = shift(r_ref, rp_ref, mu_r)
    k = shift(k_ref, kp_ref, mu_k)
    v = shift(v_ref, vp_ref, mu_v)
    lo = shift(l_ref, lp_ref, mu_l)

    bd = bd_ref[...]
    z = -(w0[...] + _dot(jnp.tanh(lo), w2p[...]))
    softplus = jnp.maximum(z, 0.0) + jnp.log1p(jnp.exp(-jnp.abs(z)))
    lw = -jnp.exp(-softplus - 0.5)
    a = jax.nn.sigmoid(a0[...] + _dot(lo, a2p[...]))
    g = _dot(jax.nn.sigmoid(lo), g2p[...])
    kk = k * kk_w[...]
    kk = kk / jnp.maximum(jnp.sqrt(_dot_exact_rhs(kk * kk, bd)), 1e-12)
    k2 = k * (1.0 + (a - 1.0) * ka_w[...])
    r_o[0] = r
    lw_o[0] = lw
    k_o[0] = k2
    v_o[0] = v
    kk_o[0] = kk
    b_o[0] = kk * a
    g_o[0] = g
    bonus_o[0] = _dot_exact_rhs(r * k2 * rk_w[...], bd) * v


def _rwkv_prep(proj, batch, seq, weights, tt):
    w = RWKV_WIDTH
    c0 = 3 * ATTN_WIDTH // w
    lc = (3 * ATTN_WIDTH + 3 * w) // LORA_PAD
    pb = tt // 8
    cur = lambda width, col: pl.BlockSpec((1, tt, width), lambda b, i: (b, i, col))
    prev = lambda width, col: pl.BlockSpec(
        (1, 8, width), lambda b, i: (b, jnp.maximum(i * pb - 1, 0), col))
    full = lambda a: pl.BlockSpec(a.shape, lambda b, i: (0,) * a.ndim)
    out = pl.BlockSpec((1, tt, w), lambda b, i: (b, i, 0))
    return pl.pallas_call(
        _rwkv_prep_kernel,
        grid=(batch, seq // tt),
        in_specs=[cur(w, c0), cur(w, c0 + 1), cur(w, c0 + 2), cur(LORA_PAD, lc),
                  prev(w, c0), prev(w, c0 + 1), prev(w, c0 + 2), prev(LORA_PAD, lc)]
                 + [full(a) for a in weights],
        out_specs=[out] * 8,
        out_shape=[jax.ShapeDtypeStruct((batch, seq, w), F32)] * 8,
        compiler_params=_params(("parallel", "arbitrary")),
        name="rwkv_prep",
    )(proj, proj, proj, proj, proj, proj, proj, proj, *weights)


def _scan_kernel(r_ref, lw_ref, k_ref, v_ref, kk_ref, b_ref, y_ref, s_ref, *, tb, pairs):
    c = SCAN_CHUNK
    c2 = 2 * c

    @pl.when(pl.program_id(1) == 0)
    def _():
        s_ref[...] = jnp.zeros_like(s_ref)

    head0 = lax.broadcasted_iota(jnp.int32, (c, LANES), 1) < RWKV_HEAD_DIM
    row = lax.broadcasted_iota(jnp.int32, (c2, c2), 0)
    col = lax.broadcasted_iota(jnp.int32, (c2, c2), 1)
    same_head = (row // c) == (col // c)
    strict = jnp.logical_and(same_head, col < row)
    incl = jnp.logical_and(same_head, col <= row)
    inv_blk = (row // SCAN_INV_BLOCK) == (col // SCAN_INV_BLOCK)
    eye = jnp.where(row == col, 1.0, 0.0).astype(F32)
    tri = jnp.where(lax.broadcasted_iota(jnp.int32, (c, c), 0)
                    >= lax.broadcasted_iota(jnp.int32, (c, c), 1), 1.0, 0.0).astype(BF16)

    def stack(x):
        return jnp.concatenate([jnp.where(head0, x, 0.0), jnp.where(head0, 0.0, x)], axis=0)

    def pair_chunk(t0, p):
        lanes = slice(p * LANES, (p + 1) * LANES)
        rows = pl.ds(t0, c)
        lw = lw_ref[0, rows, lanes]
        cum = _dot_exact_lhs(tri, lw)
        yield
        r = r_ref[0, rows, lanes]
        k = k_ref[0, rows, lanes]
        kk = kk_ref[0, rows, lanes]
        b = b_ref[0, rows, lanes]
        cum_end = cum[c - 1:c, :]
        e_neg = jnp.exp(-cum)
        e_end = jnp.exp(cum_end - cum)
        a_s = stack(-kk * jnp.exp(cum - lw))
        r_s = stack(r * jnp.exp(cum))
        b_s = stack(b * e_neg)
        k_s = stack(k * e_neg)
        bend_s = stack(b * e_end)
        kend_s = stack(k * e_end)
        v_s = stack(v_ref[0, rows, lanes])
        gram = _dot_nt(jnp.concatenate([a_s, r_s], axis=0),
                       jnp.concatenate([b_s, k_s], axis=0))
        kv = _dot_tn(v_s, kend_s)
        yield
        l_ab = jnp.where(strict, gram[:c2, :c2], 0.0)
        l_ak = jnp.where(strict, gram[:c2, c2:], 0.0)
        l_rb = jnp.where(incl, gram[c2:, :c2], 0.0)
        l_rk = jnp.where(incl, gram[c2:, c2:], 0.0)

        x = jnp.where(inv_blk, l_ab, 0.0)
        off = jnp.where(inv_blk, 0.0, l_ab)
        dinv = eye + x
        w1 = _dot(l_ak, v_s)
        y1 = _dot(l_rk, v_s)
        for _ in range(int(math.log2(SCAN_INV_BLOCK)) - 1):
            x = _dot(x, x)
            yield
            dinv = dinv + _dot(dinv, x)
            yield
        f = _dot(dinv, off)
        yield
        f2 = _dot(f, f)
        yield
        gm = eye + f
        gm = gm + _dot(gm, f2)
        yield
        t_inv = _dot(gm, dinv)
        yield
        ta = _dot(t_inv, jnp.concatenate([a_s, w1], axis=1))
        yield
        a_hat = ta[:, :LANES]
        v_hat = ta[:, LANES:]
        state = s_ref[p]
        m1 = _dot_nt(jnp.concatenate([a_hat, r_s], axis=0), state)
        yield
        u = m1[:c2] + v_hat
        y = m1[c2:] + y1 + _dot(l_rb, u)
        s_ref[p] = state * jnp.exp(cum_end) + kv + _dot_tn(u, bend_s)
        y_ref[0, rows, lanes] = y[:c] + y[c:]

    def chunk(ci, carry):
        t0 = pl.multiple_of(ci * c, c)
        for _ in itertools.zip_longest(*[pair_chunk(t0, p) for p in range(pairs)]):
            pass
        return carry

    lax.fori_loop(0, tb // c, chunk, 0)


def _rwkv_scan(r, lw, k, v, kk, b, tb):
    batch, seq, w = r.shape
    pairs = w // LANES
    spec = pl.BlockSpec((1, tb, w), lambda bi, t: (bi, t, 0))
    kern = lambda *refs: _scan_kernel(*refs, tb=tb, pairs=pairs)
    return pl.pallas_call(
        kern,
        grid=(batch, seq // tb),
        in_specs=[spec] * 6,
        out_specs=spec,
        out_shape=jax.ShapeDtypeStruct((batch, seq, w), F32),
        scratch_shapes=[pltpu.VMEM((pairs, LANES, LANES), F32)],
        compiler_params=_params(("parallel", "arbitrary")),
        name="rwkv_scan",
    )(r, lw, k, v, kk, b)


def _out_proj_kernel(x_ref, attn_ref, y_ref, bonus_ref, g_ref, lng, lnb, bd_ref, w_ref, n2g,
                     h_ref, hn_ref):
    bd = bd_ref[...]
    y = y_ref[...]
    inv_n = 1.0 / RWKV_HEAD_DIM
    d = y - _dot_exact_rhs(y, bd) * inv_n
    var = _dot_exact_rhs(d * d, bd) * inv_n
    yn = d * lax.rsqrt(var + RWKV_GN_EPS) * lng[...] + lnb[...]
    rw = (yn + bonus_ref[...]) * g_ref[...]
    h = (x_ref[...] + _dot(attn_ref[...], w_ref[:ATTN_WIDTH, :])
         + _dot(rw, w_ref[ATTN_WIDTH:, :]))
    h_ref[...] = h
    inv = lax.rsqrt(jnp.mean(h * h, axis=-1, keepdims=True) + RMS_EPS)
    hn_ref[...] = (h * inv * n2g[...]).astype(BF16)


def _out_proj(x, attn, y, bonus, g, lng, lnb, bd, w_out, n2g, tm):
    m = x.shape[0]
    row = lambda width: pl.BlockSpec((tm, width), lambda i: (i, 0))
    full = lambda a: pl.BlockSpec(a.shape, lambda i: (0,) * a.ndim)
    return pl.pallas_call(
        _out_proj_kernel,
        grid=(m // tm,),
        in_specs=[row(D_MODEL), row(ATTN_WIDTH), row(RWKV_WIDTH), row(RWKV_WIDTH), row(RWKV_WIDTH),
                  full(lng), full(lnb), full(bd), full(w_out), full(n2g)],
        out_specs=[row(D_MODEL), row(D_MODEL)],
        out_shape=[jax.ShapeDtypeStruct((m, D_MODEL), F32),
                   jax.ShapeDtypeStruct((m, D_MODEL), BF16)],
        compiler_params=_params(("parallel",)),
        name="out_proj",
    )(x, attn, y, bonus, g, lng, lnb, bd, w_out, n2g)


def _peer_q_kernel(hn_ref, wq_ref, keys_ref, st_ref):
    q = jnp.dot(hn_ref[...], wq_ref[...], preferred_element_type=F32)
    for hp in range(2 * PEER_HEADS):
        qs = q[:, hp * PEER_HALF:(hp + 1) * PEER_HALF]
        st_ref[hp] = _dot_nt(keys_ref[hp], qs)


def _peer_q(hn, wq, keys, tm):
    m = hn.shape[0]
    return pl.pallas_call(
        _peer_q_kernel,
        grid=(m // tm,),
        in_specs=[pl.BlockSpec((tm, D_MODEL), lambda i: (i, 0)),
                  pl.BlockSpec(wq.shape, lambda i: (0, 0)),
                  pl.BlockSpec(keys.shape, lambda i: (0, 0, 0))],
        out_specs=pl.BlockSpec((2 * PEER_HEADS, PEER_NKEYS, tm), lambda i: (0, 0, i)),
        out_shape=jax.ShapeDtypeStruct((2 * PEER_HEADS, PEER_NKEYS, m), F32),
        compiler_params=_params(("parallel",)),
        name="peer_q",
    )(hn, wq, keys)


def _top_desc(x, k, exact_ties):
    n = x.shape[0]
    rows = lax.broadcasted_iota(jnp.int32, x.shape, 0)
    rank = jnp.full(x.shape, float(k), F32)
    outs = []
    for i in range(k):
        m = jnp.max(x, axis=0, keepdims=True)
        outs.append(m)
        pick = x == m
        if exact_ties:
            pick = rows == jnp.min(jnp.where(pick, rows, n), axis=0, keepdims=True)
        x = jnp.where(pick, -jnp.inf, x)
        rank = jnp.where(pick, float(i), rank)
    gone = jnp.sum(jnp.where(x == -jnp.inf, 1.0, 0.0), axis=0, keepdims=True)
    return jnp.concatenate(outs, axis=0), rank, gone


def _peer_topk_kernel(s_ref, cnt1_ref, e1_ref, rank2_ref, e2_ref, *, n1):
    k = PEER_TOPK
    widths = [k // (a + 1) for a in range(k)]
    pad = -sum(widths) % 8

    def run(exact_ties):
        s1 = s_ref[0]
        s2 = s_ref[1]
        sv1, rank1, gone1 = _top_desc(s1, k, exact_ties)
        sv2, rank2, gone2 = _top_desc(s2, k, exact_ties)
        cand = jnp.concatenate([sv1[a:a + 1] + sv2[:widths[a]] for a in range(k)]
                               + [jnp.full((pad, s1.shape[1]), -jnp.inf, F32)], axis=0)
        best, pos, gone3 = _top_desc(cand, k, exact_ties)
        chosen = jnp.where(pos < float(k), 1.0, 0.0)
        z = jnp.sum(jnp.exp(best - best[0:1]), axis=0, keepdims=True)
        cnt1 = jnp.zeros(s1.shape, F32)
        row = 0
        for a in range(k):
            cnt_a = jnp.sum(chosen[row:row + widths[a]], axis=0, keepdims=True)
            cnt1 = jnp.where(rank1 == float(a), cnt_a, cnt1)
            row += widths[a]
        e1 = jnp.exp(s1 - sv1[0:1]) / z
        for t in range(PEER_NKEYS // n1):
            cnt1_ref[0, t] = cnt1[t * n1:(t + 1) * n1]
            e1_ref[0, t] = e1[t * n1:(t + 1) * n1]
        rank2_ref[0] = rank2
        e2_ref[0] = jnp.exp(s2 - sv2[0:1])
        return jnp.max(jnp.maximum(jnp.maximum(gone1, gone2), gone3 - float(pad)))

    most_gone = run(exact_ties=False)

    @pl.when(most_gone > float(k))
    def _():
        run(exact_ties=True)


def _peer_topk(st, tl, n1):
    m = st.shape[2]
    n_tiles = PEER_NKEYS // n1
    spec = pl.BlockSpec((1, PEER_NKEYS, tl), lambda h, i: (h, 0, i))
    shape = jax.ShapeDtypeStruct((PEER_HEADS, PEER_NKEYS, m), F32)
    tile_spec = pl.BlockSpec((1, n_tiles, n1, tl), lambda h, i: (h, 0, 0, i))
    tile_shape = jax.ShapeDtypeStruct((PEER_HEADS, n_tiles, n1, m), F32)
    kern = lambda *refs: _peer_topk_kernel(*refs, n1=n1)
    return pl.pallas_call(
        kern,
        grid=(PEER_HEADS, m // tl),
        in_specs=[pl.BlockSpec((2, PEER_NKEYS, tl), lambda h, i: (h, 0, i))],
        out_specs=[tile_spec, tile_spec, spec, spec],
        out_shape=[tile_shape, tile_shape, shape, shape],
        compiler_params=_params(("parallel", "parallel")),
        name="peer_topk",
    )(st)


def _peer_dense_kernel(u_ref, hnt_ref, vt_ref, rank2_ref, e2_ref, cnta_ref, cntb_ref,
                       e1a_ref, e1b_ref, yt_ref, act_ref, hpre_ref, *, n1):
    step = pl.program_id(1)
    last_step = pl.num_programs(1) - 1
    et = hpre_ref.shape[1]
    row_refs = ((cnta_ref, e1a_ref), (cntb_ref, e1b_ref))

    def gate_tile(half, jt, ib):
        prev = 1 - half
        cnt_ref, e1r_ref = row_refs[half]
        ls = slice(jt * LANES, (jt + 1) * LANES)
        rs = slice(ib * DENSE_ROWS, (ib + 1) * DENSE_ROWS)
        gates = [jnp.zeros((DENSE_ROWS, LANES), F32) for _ in range(n1)]
        for h in range(PEER_HEADS):
            r2 = rank2_ref[h, rs, ls]
            e2 = e2_ref[h, rs, ls]
            for a in range(n1):
                hit = r2 < cnt_ref[h, 0, a:a + 1, ls]
                gates[a] = gates[a] + jnp.where(hit, e2, 0.0) * e1r_ref[h, 0, a:a + 1, ls]
        for a in range(n1):
            ers = slice(a * PEER_NKEYS + ib * DENSE_ROWS, a * PEER_NKEYS + (ib + 1) * DENSE_ROWS)
            x = hpre_ref[prev, ers, ls]
            gelu = 0.5 * x * (1.0 + lax.erf(x * (0.5 ** 0.5)))
            act_ref[prev, ers, ls] = gelu * gates[a]

    tt = hpre_ref.shape[2]
    n_ib = PEER_NKEYS // DENSE_ROWS

    def run(first, last):
        for half in range(2):
            cur = half
            hs = slice(half * et, (half + 1) * et)
            gated = not (first and half == 0) and not (last and half == 1)
            for piece in range(tt // MXU_COLS):
                cs = slice(piece * MXU_COLS, (piece + 1) * MXU_COLS)
                tiles = [(jt, ib)
                         for jt in range(piece * MXU_COLS // LANES, (piece + 1) * MXU_COLS // LANES)
                         for ib in range(n_ib)]
                if not last:
                    hpre_ref[cur, :, cs] = jnp.dot(u_ref[hs, :], hnt_ref[:, cs],
                                                   preferred_element_type=F32)
                per_down = len(tiles) * DOWN_ROWS // D_MODEL
                for i, tile in enumerate(tiles):
                    if not first and i % per_down == per_down // 2:
                        ms = slice(i // per_down * DOWN_ROWS, (i // per_down + 1) * DOWN_ROWS)
                        yt_ref[ms, cs] += jnp.dot(vt_ref[ms, hs],
                                                  act_ref[cur, :, cs].astype(BF16),
                                                  preferred_element_type=F32)
                    if gated:
                        gate_tile(half, *tile)

    @pl.when(step == 0)
    def _():
        yt_ref[...] = jnp.zeros_like(yt_ref)
        run(first=True, last=False)

    @pl.when(jnp.logical_and(step > 0, step < last_step))
    def _():
        run(first=False, last=False)

    @pl.when(step == last_step)
    def _():
        run(first=False, last=True)


def _peer_dense(u, hnt, vt, cntr, e1r, rank2, e2, tt, n1):
    m = hnt.shape[1]
    et = n1 * PEER_NKEYS
    n_steps = PEER_NKEYS // (2 * n1)
    n_tiles = 2 * n_steps
    tok3 = lambda a: pl.BlockSpec((a.shape[0], a.shape[1], tt), lambda i, e: (0, 0, i))
    rows = lambda half: pl.BlockSpec(
        (PEER_HEADS, 1, n1, tt),
        lambda i, e: (0, jnp.clip(2 * e + half - 1, 0, n_tiles - 1), 0, i))
    kern = lambda *refs: _peer_dense_kernel(*refs, n1=n1)
    return pl.pallas_call(
        kern,
        grid=(m // tt, n_steps + 1),
        in_specs=[pl.BlockSpec((2 * et, D_MODEL), lambda i, e: (jnp.minimum(e, n_steps - 1), 0)),
                  pl.BlockSpec((D_MODEL, tt), lambda i, e: (0, i)),
                  pl.BlockSpec((D_MODEL, 2 * et), lambda i, e: (0, jnp.maximum(e - 1, 0))),
                  tok3(rank2), tok3(e2), rows(0), rows(1), rows(0), rows(1)],
        out_specs=pl.BlockSpec((D_MODEL, tt), lambda i, e: (0, i)),
        out_shape=jax.ShapeDtypeStruct((D_MODEL, m), F32),
        scratch_shapes=[pltpu.VMEM((2, et, tt), F32), pltpu.VMEM((2, et, tt), F32)],
        compiler_params=_params(("parallel", "arbitrary")),
        name="peer_dense",
    )(u, hnt, vt, rank2, e2, cntr, cntr, e1r, e1r)


def _transpose_cast_kernel(x_ref, o_ref):
    o_ref[...] = x_ref[...].T.astype(o_ref.dtype)


def _transpose_cast(x, tr):
    rows, cols = x.shape
    return pl.pallas_call(
        _transpose_cast_kernel,
        grid=(rows // tr,),
        in_specs=[pl.BlockSpec((tr, cols), lambda i: (i, 0))],
        out_specs=pl.BlockSpec((cols, tr), lambda i: (0, i)),
        out_shape=jax.ShapeDtypeStruct((cols, rows), BF16),
        compiler_params=_params(("parallel",)),
        name="transpose_cast",
    )(x)


def _final_kernel(h_ref, yt_ref, g_ref, o_ref):
    h = h_ref[...] + yt_ref[...].T
    inv = lax.rsqrt(jnp.mean(h * h, axis=-1, keepdims=True) + RMS_EPS)
    o_ref[...] = h * inv * g_ref[...]


def _final(h, yt, g, tm):
    m = h.shape[0]
    row = pl.BlockSpec((tm, D_MODEL), lambda i: (i, 0))
    return pl.pallas_call(
        _final_kernel,
        grid=(m // tm,),
        in_specs=[row, pl.BlockSpec((D_MODEL, tm), lambda i: (0, i)),
                  pl.BlockSpec((1, D_MODEL), lambda i: (0, 0))],
        out_specs=row,
        out_shape=jax.ShapeDtypeStruct((m, D_MODEL), F32),
        compiler_params=_params(("parallel",)),
        name="final_norm",
    )(h, yt, g)


def _row(a):
    return a.reshape(1, -1).astype(F32)


def _pad_rows(w, start, total):
    return jnp.zeros((total, w.shape[1]), BF16).at[start:start + w.shape[0]].set(w.astype(BF16))


def kernel(x, norm1_g, w_in, rwkv_mu, rwkv_w0, rwkv_w2, rwkv_a0, rwkv_a2, rwkv_g2, rwkv_k_k, rwkv_k_a, rwkv_r_k, rwkv_ln_g, rwkv_ln_b, w_out, norm2_g, peer_wq, peer_sub_keys, peer_u, peer_v, final_g):
    batch, seq, d = x.shape
    tokens = batch * seq
    w = RWKV_WIDTH
    xt = x.reshape(tokens, d)

    w_in_p = jnp.pad(w_in[0].astype(BF16), ((0, 0), (0, IN_COLS_PAD - IN_COLS)))
    proj = _norm_matmul(xt, _row(norm1_g[0]), w_in_p, tm=1024, tn=512)
    proj = proj.reshape(batch, seq, IN_COLS_PAD)

    attn = _moba(proj, batch, seq, heads=4)

    mu = rwkv_mu[0]
    mu_l = jnp.pad(mu[3 * w:], (0, LORA_PAD - LORA_COLS))
    hid = lax.broadcasted_iota(jnp.int32, (w, w), 0) // RWKV_HEAD_DIM
    bd = (hid == hid.T).astype(BF16)
    prep_w = [_row(mu[:w]), _row(mu[w:2 * w]), _row(mu[2 * w:3 * w]), _row(mu_l),
              _row(rwkv_w0[0]), _pad_rows(rwkv_w2[0], 0, LORA_PAD),
              _row(rwkv_a0[0]), _pad_rows(rwkv_a2[0], DECAY_LORA, LORA_PAD),
              _pad_rows(rwkv_g2[0], DECAY_LORA + AAA_LORA, LORA_PAD),
              _row(rwkv_k_k[0]), _row(rwkv_k_a[0]), _row(rwkv_r_k[0]), bd]
    r, lw, k2, v, kk, b, g, bonus = _rwkv_prep(proj, batch, seq, prep_w, tt=256)
    y = _rwkv_scan(r, lw, k2, v, kk, b, tb=256)

    flat = lambda a: a.reshape(tokens, a.shape[-1])
    h, hn = _out_proj(xt, flat(attn), flat(y), flat(bonus), flat(g), _row(rwkv_ln_g[0]),
                      _row(rwkv_ln_b[0]), bd, w_out[0].astype(BF16), _row(norm2_g[0]), tm=256)

    keys = peer_sub_keys[0].reshape(2 * PEER_HEADS, PEER_NKEYS, PEER_HALF).astype(BF16)
    st = _peer_q(hn, peer_wq[0].astype(BF16), keys, tm=512)
    cnt1, e1, rank2, e2 = _peer_topk(st, tl=512, n1=4)
    yt = _peer_dense(peer_u[0].astype(BF16), hn.T, _transpose_cast(peer_v[0], tr=512),
                     cnt1, e1, rank2, e2, tt=512, n1=4)
    out = _final(h, yt, final_g.reshape(1, d), tm=512)
    return out.reshape(batch, seq, d)
```

```python
import itertools
import math

import jax
import jax.numpy as jnp
from jax import lax
from jax.experimental import pallas as pl
from jax.experimental.pallas import tpu as pltpu

F32 = jnp.float32
BF16 = jnp.bfloat16

D_MODEL = 2048
ATTN_HEADS = 8
ATTN_HEAD_DIM = 128
ATTN_WIDTH = ATTN_HEADS * ATTN_HEAD_DIM
MOBA_BLOCK = 256
MOBA_TOPK = 3

RWKV_HEAD_DIM = 64
RWKV_WIDTH = D_MODEL - ATTN_WIDTH
RWKV_HEADS = RWKV_WIDTH // RWKV_HEAD_DIM
DECAY_LORA = 96
AAA_LORA = 96
GATE_LORA = 256
LORA_COLS = DECAY_LORA + AAA_LORA + GATE_LORA
LORA_PAD = 512
RWKV_GN_EPS = RWKV_HEAD_DIM * 1e-5
IN_COLS = 3 * ATTN_WIDTH + 3 * RWKV_WIDTH + LORA_COLS
IN_COLS_PAD = 3 * ATTN_WIDTH + 3 * RWKV_WIDTH + LORA_PAD

PEER_HEADS = 8
PEER_NKEYS = 128
PEER_HALF = 128
PEER_TOPK = 16

RMS_EPS = 1e-6
NEG = -1e30

LANES = 128
SCAN_CHUNK = 64
SCAN_INV_BLOCK = 16
DOWN_ROWS = 2048
MXU_COLS = 256
DENSE_ROWS = 32
VMEM_LIMIT = 56 * 1024 * 1024

_NT = (((1,), (1,)), ((), ()))
_TN = (((0,), (0,)), ((), ()))


def _params(sem, flags=None):
    return pltpu.CompilerParams(dimension_semantics=sem, vmem_limit_bytes=VMEM_LIMIT, flags=flags)


def _dot(a, b):
    return jnp.dot(a.astype(BF16), b.astype(BF16), preferred_element_type=F32)


def _dot_nt(a, b):
    return lax.dot_general(a.astype(BF16), b.astype(BF16), _NT, preferred_element_type=F32)


def _dot_tn(a, b):
    return lax.dot_general(a.astype(BF16), b.astype(BF16), _TN, preferred_element_type=F32)


def _split3(x):
    hi = x.astype(BF16)
    r1 = x - hi.astype(F32)
    mid = r1.astype(BF16)
    lo = (r1 - mid.astype(F32)).astype(BF16)
    return hi, mid, lo


def _dot_exact_rhs(x, ones_bf16):
    hi = x.astype(BF16)
    lo = (x - hi.astype(F32)).astype(BF16)
    d = lambda p: jnp.dot(p, ones_bf16, preferred_element_type=F32)
    return d(hi) + d(lo)


def _dot_exact_lhs(ones_bf16, x):
    hi, mid, lo = _split3(x)
    d = lambda p: jnp.dot(ones_bf16, p, preferred_element_type=F32)
    return d(hi) + d(mid) + d(lo)


def _norm_matmul_kernel(x_ref, g_ref, w_ref, o_ref, xn_ref):
    @pl.when(pl.program_id(1) == 0)
    def _():
        x = x_ref[...]
        inv = lax.rsqrt(jnp.mean(x * x, axis=-1, keepdims=True) + RMS_EPS)
        xn_ref[...] = (x * inv * g_ref[...]).astype(BF16)

    o_ref[...] = jnp.dot(xn_ref[...], w_ref[...], preferred_element_type=F32)


def _norm_matmul(x, g, w, tm, tn):
    m, k = x.shape
    n = w.shape[1]
    return pl.pallas_call(
        _norm_matmul_kernel,
        grid=(m // tm, n // tn),
        in_specs=[pl.BlockSpec((tm, k), lambda i, j: (i, 0)),
                  pl.BlockSpec((1, k), lambda i, j: (0, 0)),
                  pl.BlockSpec((k, tn), lambda i, j: (0, j))],
        out_specs=pl.BlockSpec((tm, tn), lambda i, j: (i, j)),
        out_shape=jax.ShapeDtypeStruct((m, n), F32),
        scratch_shapes=[pltpu.VMEM((tm, k), BF16)],
        compiler_params=_params(("parallel", "arbitrary")),
        name="in_proj",
    )(x, g, w)


def _moba_kernel(q_ref, k_ref, v_ref, o_ref, kmean_ref, kb_ref, vt_ref, *, nb, heads):
    blk_len = MOBA_BLOCK
    hd = ATTN_HEAD_DIM
    hg = pl.program_id(1)
    qi = pl.program_id(2)
    hrange = range(heads)

    @pl.when(qi == 0)
    def _():
        for hh in hrange:
            hl = slice(hh * hd, (hh + 1) * hd)
            for j in range(nb):
                rows = slice(j * blk_len, (j + 1) * blk_len)
                kj = k_ref[0, rows, hl]
                kmean_ref[hh, j:j + 1, :] = jnp.mean(kj, axis=0, keepdims=True)
                kb_ref[hh, rows, :] = kj.astype(BF16)
                vt_ref[hh, :, rows] = v_ref[0, rows, hl].T.astype(BF16)

    log2e = math.log2(math.e)
    scale2 = (hd ** -0.5) * log2e
    kq = (lax.broadcasted_iota(jnp.int32, (blk_len, blk_len), 1)
          - lax.broadcasted_iota(jnp.int32, (blk_len, blk_len), 0))
    kqf = kq.astype(F32)
    blk = lax.broadcasted_iota(jnp.int32, (nb, blk_len), 0)
    past = blk < qi

    def select(hh):
        q = q_ref[0, :, hh * hd:(hh + 1) * hd]
        gate = lax.dot_general(kmean_ref[hh], q, _NT, precision=lax.Precision.HIGHEST,
                               preferred_element_type=F32)
        g = jnp.where(past, gate, NEG)
        selw = jnp.zeros(gate.shape, F32)
        for _ in range(min(MOBA_TOPK, nb)):
            m = jnp.max(g, axis=0, keepdims=True)
            first = jnp.min(jnp.where(g == m, blk, nb), axis=0, keepdims=True)
            pick = blk == first
            selw = jnp.where(pick, 1.0, selw)
            g = jnp.where(pick, -jnp.inf, g)
        return jnp.where(past, selw, 0.0), (q * scale2).astype(BF16)

    sel_q = [select(hh) for hh in hrange]
    selw = [x[0] for x in sel_q]
    qb = [x[1] for x in sel_q]
    slope2 = [jnp.exp2(jnp.zeros((1, 1), F32) - (hg * heads + hh + 1).astype(F32)) * log2e
              for hh in hrange]
    bias0 = [slope2[hh] * kqf for hh in hrange]

    def scores(j):
        r0 = pl.multiple_of(j * blk_len, blk_len)
        return [lax.dot_general(kb_ref[hh, pl.ds(r0, blk_len), :], qb[hh], _NT,
                                preferred_element_type=F32) - bias0[hh] for hh in hrange]

    def weighted_values(j, p):
        r0 = pl.multiple_of(j * blk_len, blk_len)
        return [jnp.dot(vt_ref[hh, :, pl.ds(r0, blk_len)], p[hh].astype(BF16),
                        preferred_element_type=F32) for hh in hrange]

    s = [jnp.where(kq >= 0, x, NEG) for x in scores(qi)]
    m = [jnp.max(x, axis=0, keepdims=True) for x in s]
    p = [jnp.exp2(s[hh] - m[hh]) for hh in hrange]
    l = [jnp.sum(x, axis=0, keepdims=True) for x in p]
    acc = weighted_values(qi, p)

    def body(j, carry):
        m, l, acc = carry
        off = ((qi - j) * blk_len).astype(F32)
        s = scores(j)
        selj = [jnp.max(jnp.where(blk == j, selw[hh], 0.0), axis=0, keepdims=True) > 0.0
                for hh in hrange]
        shift = [slope2[hh] * off for hh in hrange]
        top = [jnp.max(s[hh], axis=0, keepdims=True) - shift[hh] for hh in hrange]
        m_new = [jnp.maximum(m[hh], jnp.where(selj[hh], top[hh], NEG)) for hh in hrange]
        alpha = [jnp.exp2(m[hh] - m_new[hh]) for hh in hrange]
        p = [jnp.exp2(s[hh] - jnp.where(selj[hh], m_new[hh] + shift[hh], jnp.inf)) for hh in hrange]
        l = [alpha[hh] * l[hh] + jnp.sum(p[hh], axis=0, keepdims=True) for hh in hrange]
        pv = weighted_values(j, p)
        acc = [alpha[hh] * acc[hh] + pv[hh] for hh in hrange]
        return m_new, l, acc

    m, l, acc = lax.fori_loop(0, qi, body, (m, l, acc))
    for hh in hrange:
        o_ref[0, :, hh * hd:(hh + 1) * hd] = (acc[hh] / l[hh]).T


def _moba(proj, batch, seq, heads):
    nb = seq // MOBA_BLOCK
    hd = ATTN_HEAD_DIM
    hw = heads * hd
    groups = ATTN_HEADS // heads
    kern = lambda *refs: _moba_kernel(*refs, nb=nb, heads=heads)
    return pl.pallas_call(
        kern,
        grid=(batch, groups, nb),
        in_specs=[pl.BlockSpec((1, MOBA_BLOCK, hw), lambda b, h, i: (b, i, h)),
                  pl.BlockSpec((1, seq, hw), lambda b, h, i: (b, 0, groups + h)),
                  pl.BlockSpec((1, seq, hw), lambda b, h, i: (b, 0, 2 * groups + h))],
        out_specs=pl.BlockSpec((1, MOBA_BLOCK, hw), lambda b, h, i: (b, i, h)),
        out_shape=jax.ShapeDtypeStruct((batch, seq, ATTN_WIDTH), F32),
        scratch_shapes=[pltpu.VMEM((heads, nb, hd), F32),
                        pltpu.VMEM((heads, seq, hd), BF16),
                        pltpu.VMEM((heads, hd, seq), BF16)],
        compiler_params=_params(("parallel", "parallel", "arbitrary")),
        name="moba",
    )(proj, proj, proj)


def _rwkv_prep_kernel(r_ref, k_ref, v_ref, l_ref, rp_ref, kp_ref, vp_ref, lp_ref,
                      mu_r, mu_k, mu_v, mu_l, w0, w2p, a0, a2p, g2p, kk_w, ka_w, rk_w, bd_ref,
                      r_o, lw_o, k_o, v_o, kk_o, b_o, g_o, bonus_o):
    first = pl.program_id(1) == 0

    def shift(cur_ref, prev_ref, mu_ref):
        cur = cur_ref[0]
        prev_last = jnp.where(first, 0.0, prev_ref[0, 7:8, :])
        rows = lax.broadcasted_iota(jnp.int32, cur.shape, 0)
        prev = jnp.where(rows == 0, prev_last, pltpu.roll(cur, 1, axis=0))
        return cur + (prev - cur) * mu_ref[...]

    r = shift(r_ref, rp_ref, mu_r)
    k = shift(k_ref, kp_ref, mu_k)
    v = shift(v_ref, vp_ref, mu_v)
    lo = shift(l_ref, lp_ref, mu_l)

    bd = bd_ref[...]
    z = -(w0[...] + _dot(jnp.tanh(lo), w2p[...]))
    softplus = jnp.maximum(z, 0.0) + jnp.log1p(jnp.exp(-jnp.abs(z)))
    lw = -jnp.exp(-softplus - 0.5)
    a = jax.nn.sigmoid(a0[...] + _dot(lo, a2p[...]))
    g = _dot(jax.nn.sigmoid(lo), g2p[...])
    kk = k * kk_w[...]
    kk = kk / jnp.maximum(jnp.sqrt(_dot_exact_rhs(kk * kk, bd)), 1e-12)
    k2 = k * (1.0 + (a - 1.0) * ka_w[...])
    r_o[0] = r
    lw_o[0] = lw
    k_o[0] = k2
    v_o[0] = v
    kk_o[0] = kk
    b_o[0] = kk * a
    g_o[0] = g
    bonus_o[0] = _dot_exact_rhs(r * k2 * rk_w[...], bd) * v


def _rwkv_prep(proj, batch, seq, weights, tt):
    w = RWKV_WIDTH
    c0 = 3 * ATTN_WIDTH // w
    lc = (3 * ATTN_WIDTH + 3 * w) // LORA_PAD
    pb = tt // 8
    cur = lambda width, col: pl.BlockSpec((1, tt, width), lambda b, i: (b, i, col))
    prev = lambda width, col: pl.BlockSpec(
        (1, 8, width), lambda b, i: (b, jnp.maximum(i * pb - 1, 0), col))
    full = lambda a: pl.BlockSpec(a.shape, lambda b, i: (0,) * a.ndim)
    out = pl.BlockSpec((1, tt, w), lambda b, i: (b, i, 0))
    return pl.pallas_call(
        _rwkv_prep_kernel,
        grid=(batch, seq // tt),
        in_specs=[cur(w, c0), cur(w, c0 + 1), cur(w, c0 + 2), cur(LORA_PAD, lc),
                  prev(w, c0), prev(w, c0 + 1), prev(w, c0 + 2), prev(LORA_PAD, lc)]
                 + [full(a) for a in weights],
        out_specs=[out] * 8,
        out_shape=[jax.ShapeDtypeStruct((batch, seq, w), F32)] * 8,
        compiler_params=_params(("parallel", "arbitrary")),
        name="rwkv_prep",
    )(proj, proj, proj, proj, proj, proj, proj, proj, *weights)


def _scan_kernel(r_ref, lw_ref, k_ref, v_ref, kk_ref, b_ref, y_ref, s_ref, *, tb, pairs):
    c = SCAN_CHUNK
    c2 = 2 * c

    @pl.when(pl.program_id(1) == 0)
    def _():
        s_ref[...] = jnp.zeros_like(s_ref)

    head0 = lax.broadcasted_iota(jnp.int32, (c, LANES), 1) < RWKV_HEAD_DIM
    row = lax.broadcasted_iota(jnp.int32, (c2, c2), 0)
    col = lax.broadcasted_iota(jnp.int32, (c2, c2), 1)
    same_head = (row // c) == (col // c)
    strict = jnp.logical_and(same_head, col < row)
    incl = jnp.logical_and(same_head, col <= row)
    inv_blk = (row // SCAN_INV_BLOCK) == (col // SCAN_INV_BLOCK)
    eye = jnp.where(row == col, 1.0, 0.0).astype(F32)
    tri = jnp.where(lax.broadcasted_iota(jnp.int32, (c, c), 0)
                    >= lax.broadcasted_iota(jnp.int32, (c, c), 1), 1.0, 0.0).astype(BF16)

    def stack(x):
        return jnp.concatenate([jnp.where(head0, x, 0.0), jnp.where(head0, 0.0, x)], axis=0)

    def pair_chunk(t0, p):
        lanes = slice(p * LANES, (p + 1) * LANES)
        rows = pl.ds(t0, c)
        lw = lw_ref[0, rows, lanes]
        cum = _dot_exact_lhs(tri, lw)
        yield
        r = r_ref[0, rows, lanes]
        k = k_ref[0, rows, lanes]
        kk = kk_ref[0, rows, lanes]
        b = b_ref[0, rows, lanes]
        cum_end = cum[c - 1:c, :]
        e_neg = jnp.exp(-cum)
        e_end = jnp.exp(cum_end - cum)
        a_s = stack(-kk * jnp.exp(cum - lw))
        r_s = stack(r * jnp.exp(cum))
        b_s = stack(b * e_neg)
        k_s = stack(k * e_neg)
        bend_s = stack(b * e_end)
        kend_s = stack(k * e_end)
        v_s = stack(v_ref[0, rows, lanes])
        gram = _dot_nt(jnp.concatenate([a_s, r_s], axis=0),
                       jnp.concatenate([b_s, k_s], axis=0))
        kv = _dot_tn(v_s, kend_s)
        yield
        l_ab = jnp.where(strict, gram[:c2, :c2], 0.0)
        l_ak = jnp.where(strict, gram[:c2, c2:], 0.0)
        l_rb = jnp.where(incl, gram[c2:, :c2], 0.0)
        l_rk = jnp.where(incl, gram[c2:, c2:], 0.0)

        x = jnp.where(inv_blk, l_ab, 0.0)
        off = jnp.where(inv_blk, 0.0, l_ab)
        dinv = eye + x
        w1 = _dot(l_ak, v_s)
        y1 = _dot(l_rk, v_s)
        for _ in range(int(math.log2(SCAN_INV_BLOCK)) - 1):
            x = _dot(x, x)
            yield
            dinv = dinv + _dot(dinv, x)
            yield
        f = _dot(dinv, off)
        yield
        f2 = _dot(f, f)
        yield
        gm = eye + f
        gm = gm + _dot(gm, f2)
        yield
        t_inv = _dot(gm, dinv)
        yield
        ta = _dot(t_inv, jnp.concatenate([a_s, w1], axis=1))
        yield
        a_hat = ta[:, :LANES]
        v_hat = ta[:, LANES:]
        state = s_ref[p]
        m1 = _dot_nt(jnp.concatenate([a_hat, r_s], axis=0), state)
        yield
        u = m1[:c2] + v_hat
        y = m1[c2:] + y1 + _dot(l_rb, u)
        s_ref[p] = state * jnp.exp(cum_end) + kv + _dot_tn(u, bend_s)
        y_ref[0, rows, lanes] = y[:c] + y[c:]

    def chunk(ci, carry):
        t0 = pl.multiple_of(ci * c, c)
        for _ in itertools.zip_longest(*[pair_chunk(t0, p) for p in range(pairs)]):
            pass
        return carry

    lax.fori_loop(0, tb // c, chunk, 0)


def _rwkv_scan(r, lw, k, v, kk, b, tb):
    batch, seq, w = r.shape
    pairs = w // LANES
    spec = pl.BlockSpec((1, tb, w), lambda bi, t: (bi, t, 0))
    kern = lambda *refs: _scan_kernel(*refs, tb=tb, pairs=pairs)
    return pl.pallas_call(
        kern,
        grid=(batch, seq // tb),
        in_specs=[spec] * 6,
        out_specs=spec,
        out_shape=jax.ShapeDtypeStruct((batch, seq, w), F32),
        scratch_shapes=[pltpu.VMEM((pairs, LANES, LANES), F32)],
        compiler_params=_params(("parallel", "arbitrary")),
        name="rwkv_scan",
    )(r, lw, k, v, kk, b)


def _out_proj_kernel(x_ref, attn_ref, y_ref, bonus_ref, g_ref, lng, lnb, bd_ref, w_ref, n2g,
                     h_ref, hn_ref, hnt_ref):
    bd = bd_ref[...]
    y = y_ref[...]
    inv_n = 1.0 / RWKV_HEAD_DIM
    d = y - _dot(y, bd) * inv_n
    var = _dot(d * d, bd) * inv_n
    yn = d * lax.rsqrt(var + RWKV_GN_EPS) * lng[...] + lnb[...]
    rw = (yn + bonus_ref[...]) * g_ref[...]
    h = (x_ref[...] + _dot(attn_ref[...], w_ref[:ATTN_WIDTH, :])
         + _dot(rw, w_ref[ATTN_WIDTH:, :]))
    h_ref[...] = h
    inv = lax.rsqrt(jnp.mean(h * h, axis=-1, keepdims=True) + RMS_EPS)
    hn = h * inv * n2g[...]
    hn_ref[...] = hn.astype(BF16)
    hnt_ref[...] = hn.T.astype(BF16)


def _out_proj(x, attn, y, bonus, g, lng, lnb, bd, w_out, n2g, tm):
    m = x.shape[0]
    row = lambda width: pl.BlockSpec((tm, width), lambda i: (i, 0))
    full = lambda a: pl.BlockSpec(a.shape, lambda i: (0,) * a.ndim)
    return pl.pallas_call(
        _out_proj_kernel,
        grid=(m // tm,),
        in_specs=[row(D_MODEL), row(ATTN_WIDTH), row(RWKV_WIDTH), row(RWKV_WIDTH), row(RWKV_WIDTH),
                  full(lng), full(lnb), full(bd), full(w_out), full(n2g)],
        out_specs=[row(D_MODEL), row(D_MODEL), pl.BlockSpec((D_MODEL, tm), lambda i: (0, i))],
        out_shape=[jax.ShapeDtypeStruct((m, D_MODEL), F32),
                   jax.ShapeDtypeStruct((m, D_MODEL), BF16),
                   jax.ShapeDtypeStruct((D_MODEL, m), BF16)],
        compiler_params=_params(("parallel",)),
        name="out_proj",
    )(x, attn, y, bonus, g, lng, lnb, bd, w_out, n2g)


def _peer_q_kernel(hn_ref, wq_ref, keys_ref, st_ref):
    q = jnp.dot(hn_ref[...], wq_ref[...], preferred_element_type=F32)
    for hp in range(2 * PEER_HEADS):
        qs = q[:, hp * PEER_HALF:(hp + 1) * PEER_HALF]
        st_ref[hp] = _dot_nt(keys_ref[hp], qs)


def _peer_q(hn, wq, keys, tm):
    m = hn.shape[0]
    return pl.pallas_call(
        _peer_q_kernel,
        grid=(m // tm,),
        in_specs=[pl.BlockSpec((tm, D_MODEL), lambda i: (i, 0)),
                  pl.BlockSpec(wq.shape, lambda i: (0, 0)),
                  pl.BlockSpec(keys.shape, lambda i: (0, 0, 0))],
        out_specs=pl.BlockSpec((2 * PEER_HEADS, PEER_NKEYS, tm), lambda i: (0, 0, i)),
        out_shape=jax.ShapeDtypeStruct((2 * PEER_HEADS, PEER_NKEYS, m), F32),
        compiler_params=_params(("parallel",)),
        name="peer_q",
    )(hn, wq, keys)


def _top_desc(x, k, exact_ties):
    n = x.shape[0]
    rows = lax.broadcasted_iota(jnp.int32, x.shape, 0)
    rank = jnp.full(x.shape, float(k), F32)
    outs = []
    for i in range(k):
        m = jnp.max(x, axis=0, keepdims=True)
        outs.append(m)
        pick = x == m
        if exact_ties:
            pick = rows == jnp.min(jnp.where(pick, rows, n), axis=0, keepdims=True)
        x = jnp.where(pick, -jnp.inf, x)
        rank = jnp.where(pick, float(i), rank)
    gone = jnp.sum(jnp.where(x == -jnp.inf, 1.0, 0.0), axis=0, keepdims=True)
    return jnp.concatenate(outs, axis=0), rank, gone


def _peer_topk_kernel(s_ref, cnt1_ref, e1_ref, rank2_ref, e2_ref, *, n1):
    k = PEER_TOPK
    widths = [k // (a + 1) for a in range(k)]
    pad = -sum(widths) % 8

    def run(exact_ties):
        s1 = s_ref[0]
        s2 = s_ref[1]
        sv1, rank1, gone1 = _top_desc(s1, k, exact_ties)
        sv2, rank2, gone2 = _top_desc(s2, k, exact_ties)
        cand = jnp.concatenate([sv1[a:a + 1] + sv2[:widths[a]] for a in range(k)]
                               + [jnp.full((pad, s1.shape[1]), -jnp.inf, F32)], axis=0)
        best, pos, gone3 = _top_desc(cand, k, exact_ties)
        chosen = jnp.where(pos < float(k), 1.0, 0.0)
        z = jnp.sum(jnp.exp(best - best[0:1]), axis=0, keepdims=True)
        cnt1 = jnp.zeros(s1.shape, F32)
        row = 0
        for a in range(k):
            cnt_a = jnp.sum(chosen[row:row + widths[a]], axis=0, keepdims=True)
            cnt1 = jnp.where(rank1 == float(a), cnt_a, cnt1)
            row += widths[a]
        e1 = 0.5 * jnp.exp(s1 - sv1[0:1]) / z
        for t in range(PEER_NKEYS // n1):
            cnt1_ref[0, t] = cnt1[t * n1:(t + 1) * n1]
            e1_ref[0, t] = e1[t * n1:(t + 1) * n1]
        rank2_ref[0] = rank2
        e2_ref[0] = jnp.exp(s2 - sv2[0:1])
        return jnp.max(jnp.maximum(jnp.maximum(gone1, gone2), gone3 - float(pad)))

    most_gone = run(exact_ties=False)

    @pl.when(most_gone > float(k))
    def _():
        run(exact_ties=True)


def _peer_topk(st, tl, n1):
    m = st.shape[2]
    n_tiles = PEER_NKEYS // n1
    spec = pl.BlockSpec((1, PEER_NKEYS, tl), lambda h, i: (h, 0, i))
    shape = jax.ShapeDtypeStruct((PEER_HEADS, PEER_NKEYS, m), F32)
    tile_spec = pl.BlockSpec((1, n_tiles, n1, tl), lambda h, i: (h, 0, 0, i))
    tile_shape = jax.ShapeDtypeStruct((PEER_HEADS, n_tiles, n1, m), F32)
    kern = lambda *refs: _peer_topk_kernel(*refs, n1=n1)
    return pl.pallas_call(
        kern,
        grid=(PEER_HEADS, m // tl),
        in_specs=[pl.BlockSpec((2, PEER_NKEYS, tl), lambda h, i: (h, 0, i))],
        out_specs=[tile_spec, tile_spec, spec, spec],
        out_shape=[tile_shape, tile_shape, shape, shape],
        compiler_params=_params(("parallel", "parallel")),
        name="peer_topk",
    )(st)


def _peer_dense_kernel(u_ref, hnt_ref, vt_ref, rank2_ref, e2_ref, cnta_ref, cntb_ref,
                       e1a_ref, e1b_ref, yt_ref, act_ref, hpre_ref, *, n1):
    step = pl.program_id(1)
    last_step = pl.num_programs(1) - 1
    et = hpre_ref.shape[1]
    row_refs = ((cnta_ref, e1a_ref), (cntb_ref, e1b_ref))

    def gate_tile(half, jt, ib):
        prev = 1 - half
        cnt_ref, e1r_ref = row_refs[half]
        ls = slice(jt * LANES, (jt + 1) * LANES)
        rs = slice(ib * DENSE_ROWS, (ib + 1) * DENSE_ROWS)
        gates = [jnp.zeros((DENSE_ROWS, LANES), F32) for _ in range(n1)]
        for h in range(PEER_HEADS):
            r2 = rank2_ref[h, rs, ls]
            e2 = e2_ref[h, rs, ls]
            for a in range(n1):
                hit = r2 < cnt_ref[h, 0, a:a + 1, ls]
                gates[a] = gates[a] + jnp.where(hit, e2, 0.0) * e1r_ref[h, 0, a:a + 1, ls]
        for a in range(n1):
            ers = slice(a * PEER_NKEYS + ib * DENSE_ROWS, a * PEER_NKEYS + (ib + 1) * DENSE_ROWS)
            x = hpre_ref[prev, ers, ls]
            gelu2 = x * (1.0 + lax.erf(x * (0.5 ** 0.5)))
            act_ref[prev, ers, ls] = gelu2 * gates[a]

    tt = hpre_ref.shape[2]
    n_ib = PEER_NKEYS // DENSE_ROWS

    def run(first, last):
        for half in range(2):
            cur = half
            hs = slice(half * et, (half + 1) * et)
            gated = not (first and half == 0) and not (last and half == 1)
            for piece in range(tt // MXU_COLS):
                cs = slice(piece * MXU_COLS, (piece + 1) * MXU_COLS)
                tiles = [(jt, ib)
                         for jt in range(piece * MXU_COLS // LANES, (piece + 1) * MXU_COLS // LANES)
                         for ib in range(n_ib)]
                if not last:
                    hpre_ref[cur, :, cs] = jnp.dot(u_ref[hs, :], hnt_ref[:, cs],
                                                   preferred_element_type=F32)
                per_down = len(tiles) * DOWN_ROWS // D_MODEL
                for i, tile in enumerate(tiles):
                    if not first and i % per_down == per_down // 2:
                        ms = slice(i // per_down * DOWN_ROWS, (i // per_down + 1) * DOWN_ROWS)
                        yt_ref[ms, cs] += jnp.dot(vt_ref[ms, hs],
                                                  act_ref[cur, :, cs].astype(BF16),
                                                  preferred_element_type=F32)
                    if gated:
                        gate_tile(half, *tile)

    @pl.when(step == 0)
    def _():
        yt_ref[...] = jnp.zeros_like(yt_ref)
        run(first=True, last=False)

    @pl.when(jnp.logical_and(step > 0, step < last_step))
    def _():
        run(first=False, last=False)

    @pl.when(step == last_step)
    def _():
        run(first=False, last=True)


def _peer_dense(u, hnt, vt, cntr, e1r, rank2, e2, tt, n1):
    m = hnt.shape[1]
    et = n1 * PEER_NKEYS
    n_steps = PEER_NKEYS // (2 * n1)
    n_tiles = 2 * n_steps
    tok3 = lambda a: pl.BlockSpec((a.shape[0], a.shape[1], tt), lambda i, e: (0, 0, i))
    rows = lambda half: pl.BlockSpec(
        (PEER_HEADS, 1, n1, tt),
        lambda i, e: (0, jnp.clip(2 * e + half - 1, 0, n_tiles - 1), 0, i))
    kern = lambda *refs: _peer_dense_kernel(*refs, n1=n1)
    return pl.pallas_call(
        kern,
        grid=(m // tt, n_steps + 1),
        in_specs=[pl.BlockSpec((2 * et, D_MODEL), lambda i, e: (jnp.minimum(e, n_steps - 1), 0)),
                  pl.BlockSpec((D_MODEL, tt), lambda i, e: (0, i)),
                  pl.BlockSpec((D_MODEL, 2 * et), lambda i, e: (0, jnp.maximum(e - 1, 0))),
                  tok3(rank2), tok3(e2), rows(0), rows(1), rows(0), rows(1)],
        out_specs=pl.BlockSpec((D_MODEL, tt), lambda i, e: (0, i)),
        out_shape=jax.ShapeDtypeStruct((D_MODEL, m), F32),
        scratch_shapes=[pltpu.VMEM((2, et, tt), F32), pltpu.VMEM((2, et, tt), F32)],
        compiler_params=_params(("parallel", "arbitrary")),
        name="peer_dense",
    )(u, hnt, vt, rank2, e2, cntr, cntr, e1r, e1r)


def _transpose_cast_kernel(x_ref, o_ref):
    o_ref[...] = x_ref[...].T.astype(o_ref.dtype)


def _transpose_cast(x, tr):
    rows, cols = x.shape
    return pl.pallas_call(
        _transpose_cast_kernel,
        grid=(rows // tr,),
        in_specs=[pl.BlockSpec((tr, cols), lambda i: (i, 0))],
        out_specs=pl.BlockSpec((cols, tr), lambda i: (0, i)),
        out_shape=jax.ShapeDtypeStruct((cols, rows), BF16),
        compiler_params=_params(("parallel",)),
        name="transpose_cast",
    )(x)


def _final_kernel(h_ref, yt_ref, g_ref, o_ref):
    h = h_ref[...] + yt_ref[...].T
    inv = lax.rsqrt(jnp.mean(h * h, axis=-1, keepdims=True) + RMS_EPS)
    o_ref[...] = h * inv * g_ref[...]


def _final(h, yt, g, tm):
    m = h.shape[0]
    row = pl.BlockSpec((tm, D_MODEL), lambda i: (i, 0))
    return pl.pallas_call(
        _final_kernel,
        grid=(m // tm,),
        in_specs=[row, pl.BlockSpec((D_MODEL, tm), lambda i: (0, i)),
                  pl.BlockSpec((1, D_MODEL), lambda i: (0, 0))],
        out_specs=row,
        out_shape=jax.ShapeDtypeStruct((m, D_MODEL), F32),
        compiler_params=_params(("parallel",)),
        name="final_norm",
    )(h, yt, g)


def _row(a):
    return a.reshape(1, -1).astype(F32)


def _pad_rows(w, start, total):
    return jnp.zeros((total, w.shape[1]), BF16).at[start:start + w.shape[0]].set(w.astype(BF16))


def kernel(x, norm1_g, w_in, rwkv_mu, rwkv_w0, rwkv_w2, rwkv_a0, rwkv_a2, rwkv_g2, rwkv_k_k, rwkv_k_a, rwkv_r_k, rwkv_ln_g, rwkv_ln_b, w_out, norm2_g, peer_wq, peer_sub_keys, peer_u, peer_v, final_g):
    batch, seq, d = x.shape
    tokens = batch * seq
    w = RWKV_WIDTH
    xt = x.reshape(tokens, d)

    w_in_p = jnp.pad(w_in[0].astype(BF16), ((0, 0), (0, IN_COLS_PAD - IN_COLS)))
    proj = _norm_matmul(xt, _row(norm1_g[0]), w_in_p, tm=1024, tn=512)
    proj = proj.reshape(batch, seq, IN_COLS_PAD)

    attn = _moba(proj, batch, seq, heads=4)

    mu = rwkv_mu[0]
    mu_l = jnp.pad(mu[3 * w:], (0, LORA_PAD - LORA_COLS))
    hid = lax.broadcasted_iota(jnp.int32, (w, w), 0) // RWKV_HEAD_DIM
    bd = (hid == hid.T).astype(BF16)
    prep_w = [_row(mu[:w]), _row(mu[w:2 * w]), _row(mu[2 * w:3 * w]), _row(mu_l),
              _row(rwkv_w0[0]), _pad_rows(rwkv_w2[0], 0, LORA_PAD),
              _row(rwkv_a0[0]), _pad_rows(rwkv_a2[0], DECAY_LORA, LORA_PAD),
              _pad_rows(rwkv_g2[0], DECAY_LORA + AAA_LORA, LORA_PAD),
              _row(rwkv_k_k[0]), _row(rwkv_k_a[0]), _row(rwkv_r_k[0]), bd]
    r, lw, k2, v, kk, b, g, bonus = _rwkv_prep(proj, batch, seq, prep_w, tt=256)
    y = _rwkv_scan(r, lw, k2, v, kk, b, tb=256)

    flat = lambda a: a.reshape(tokens, a.shape[-1])
    h, hn, hnt = _out_proj(xt, flat(attn), flat(y), flat(bonus), flat(g), _row(rwkv_ln_g[0]),
                      _row(rwkv_ln_b[0]), bd, w_out[0].astype(BF16), _row(norm2_g[0]), tm=256)

    keys = peer_sub_keys[0].reshape(2 * PEER_HEADS, PEER_NKEYS, PEER_HALF).astype(BF16)
    st = _peer_q(hn, peer_wq[0].astype(BF16), keys, tm=512)
    cnt1, e1, rank2, e2 = _peer_topk(st, tl=512, n1=4)
    yt = _peer_dense(peer_u[0].astype(BF16), hnt, _transpose_cast(peer_v[0], tr=512),
                     cnt1, e1, rank2, e2, tt=512, n1=4)
    out = _final(h, yt, final_g.reshape(1, d), tm=512)
    return out.reshape(batch, seq, d)
```

```python
import itertools
import math

import jax
import jax.numpy as jnp
from jax import lax
from jax.experimental import pallas as pl
from jax.experimental.pallas import tpu as pltpu

F32 = jnp.float32
BF16 = jnp.bfloat16

D_MODEL = 2048
ATTN_HEADS = 8
ATTN_HEAD_DIM = 128
ATTN_WIDTH = ATTN_HEADS * ATTN_HEAD_DIM
MOBA_BLOCK = 256
MOBA_TOPK = 3

RWKV_HEAD_DIM = 64
RWKV_WIDTH = D_MODEL - ATTN_WIDTH
RWKV_HEADS = RWKV_WIDTH // RWKV_HEAD_DIM
DECAY_LORA = 96
AAA_LORA = 96
GATE_LORA = 256
LORA_COLS = DECAY_LORA + AAA_LORA + GATE_LORA
LORA_PAD = 512
RWKV_GN_EPS = RWKV_HEAD_DIM * 1e-5
IN_COLS = 3 * ATTN_WIDTH + 3 * RWKV_WIDTH + LORA_COLS
IN_COLS_PAD = 3 * ATTN_WIDTH + 3 * RWKV_WIDTH + LORA_PAD

PEER_HEADS = 8
PEER_NKEYS = 128
PEER_HALF = 128
PEER_TOPK = 16

RMS_EPS = 1e-6
NEG = -1e30

LANES = 128
SCAN_CHUNK = 64
SCAN_INV_BLOCK = 16
DOWN_ROWS = 2048
MXU_COLS = 256
DENSE_ROWS = 32
VMEM_LIMIT = 56 * 1024 * 1024

_NT = (((1,), (1,)), ((), ()))
_TN = (((0,), (0,)), ((), ()))


def _params(sem, flags=None):
    return pltpu.CompilerParams(dimension_semantics=sem, vmem_limit_bytes=VMEM_LIMIT, flags=flags)


def _dot(a, b):
    return jnp.dot(a.astype(BF16), b.astype(BF16), preferred_element_type=F32)


def _dot_nt(a, b):
    return lax.dot_general(a.astype(BF16), b.astype(BF16), _NT, preferred_element_type=F32)


def _dot_tn(a, b):
    return lax.dot_general(a.astype(BF16), b.astype(BF16), _TN, preferred_element_type=F32)


def _split3(x):
    hi = x.astype(BF16)
    r1 = x - hi.astype(F32)
    mid = r1.astype(BF16)
    lo = (r1 - mid.astype(F32)).astype(BF16)
    return hi, mid, lo


def _dot_exact_rhs(x, ones_bf16):
    hi = x.astype(BF16)
    lo = (x - hi.astype(F32)).astype(BF16)
    d = lambda p: jnp.dot(p, ones_bf16, preferred_element_type=F32)
    return d(hi) + d(lo)


def _dot_exact_lhs(ones_bf16, x):
    hi, mid, lo = _split3(x)
    d = lambda p: jnp.dot(ones_bf16, p, preferred_element_type=F32)
    return d(hi) + d(mid) + d(lo)


def _norm_matmul_kernel(x_ref, g_ref, w_ref, o_ref, xn_ref):
    @pl.when(pl.program_id(1) == 0)
    def _():
        x = x_ref[...]
        inv = lax.rsqrt(jnp.mean(x * x, axis=-1, keepdims=True) + RMS_EPS)
        xn_ref[...] = (x * inv * g_ref[...]).astype(BF16)

    o_ref[...] = jnp.dot(xn_ref[...], w_ref[...], preferred_element_type=F32)


def _norm_matmul(x, g, w, tm, tn):
    m, k = x.shape
    n = w.shape[1]
    return pl.pallas_call(
        _norm_matmul_kernel,
        grid=(m // tm, n // tn),
        in_specs=[pl.BlockSpec((tm, k), lambda i, j: (i, 0)),
                  pl.BlockSpec((1, k), lambda i, j: (0, 0)),
                  pl.BlockSpec((k, tn), lambda i, j: (0, j))],
        out_specs=pl.BlockSpec((tm, tn), lambda i, j: (i, j)),
        out_shape=jax.ShapeDtypeStruct((m, n), F32),
        scratch_shapes=[pltpu.VMEM((tm, k), BF16)],
        compiler_params=_params(("parallel", "arbitrary")),
        name="in_proj",
    )(x, g, w)


def _moba_kernel(q_ref, k_ref, v_ref, o_ref, kmean_ref, kb_ref, vt_ref, *, nb, heads):
    blk_len = MOBA_BLOCK
    hd = ATTN_HEAD_DIM
    hg = pl.program_id(1)
    qi = pl.program_id(2)
    hrange = range(heads)

    @pl.when(qi == 0)
    def _():
        for hh in hrange:
            hl = slice(hh * hd, (hh + 1) * hd)
            for j in range(nb):
                rows = slice(j * blk_len, (j + 1) * blk_len)
                kj = k_ref[0, rows, hl]
                kmean_ref[hh, j:j + 1, :] = jnp.mean(kj, axis=0, keepdims=True)
                kb_ref[hh, rows, :] = kj.astype(BF16)
                vt_ref[hh, :, rows] = v_ref[0, rows, hl].T.astype(BF16)

    log2e = math.log2(math.e)
    scale2 = (hd ** -0.5) * log2e
    kq = (lax.broadcasted_iota(jnp.int32, (blk_len, blk_len), 1)
          - lax.broadcasted_iota(jnp.int32, (blk_len, blk_len), 0))
    kqf = kq.astype(F32)
    blk = lax.broadcasted_iota(jnp.int32, (nb, blk_len), 0)
    past = blk < qi

    def select(hh):
        q = q_ref[0, :, hh * hd:(hh + 1) * hd]
        gate = lax.dot_general(kmean_ref[hh], q, _NT, precision=lax.Precision.HIGHEST,
                               preferred_element_type=F32)
        g = jnp.where(past, gate, NEG)
        selw = jnp.zeros(gate.shape, F32)
        for _ in range(min(MOBA_TOPK, nb)):
            m = jnp.max(g, axis=0, keepdims=True)
            first = jnp.min(jnp.where(g == m, blk, nb), axis=0, keepdims=True)
            pick = blk == first
            selw = jnp.where(pick, 1.0, selw)
            g = jnp.where(pick, -jnp.inf, g)
        return jnp.where(past, selw, 0.0), (q * scale2).astype(BF16)

    sel_q = [select(hh) for hh in hrange]
    selw = [x[0] for x in sel_q]
    qb = [x[1] for x in sel_q]
    slope2 = [jnp.exp2(jnp.zeros((1, 1), F32) - (hg * heads + hh + 1).astype(F32)) * log2e
              for hh in hrange]
    bias0 = [slope2[hh] * kqf for hh in hrange]

    def scores(j):
        r0 = pl.multiple_of(j * blk_len, blk_len)
        return [lax.dot_general(kb_ref[hh, pl.ds(r0, blk_len), :], qb[hh], _NT,
                                preferred_element_type=F32) - bias0[hh] for hh in hrange]

    def weighted_values(j, p):
        r0 = pl.multiple_of(j * blk_len, blk_len)
        return [jnp.dot(vt_ref[hh, :, pl.ds(r0, blk_len)], p[hh].astype(BF16),
                        preferred_element_type=F32) for hh in hrange]

    s = [jnp.where(kq >= 0, x, NEG) for x in scores(qi)]
    m = [jnp.max(x, axis=0, keepdims=True) for x in s]
    p = [jnp.exp2(s[hh] - m[hh]) for hh in hrange]
    l = [jnp.sum(x, axis=0, keepdims=True) for x in p]
    acc = weighted_values(qi, p)

    def body(j, carry):
        m, l, acc = carry
        off = ((qi - j) * blk_len).astype(F32)
        s = scores(j)
        selj = [jnp.max(jnp.where(blk == j, selw[hh], 0.0), axis=0, keepdims=True) > 0.0
                for hh in hrange]
        shift = [slope2[hh] * off for hh in hrange]
        top = [jnp.max(s[hh], axis=0, keepdims=True) - shift[hh] for hh in hrange]
        m_new = [jnp.maximum(m[hh], jnp.where(selj[hh], top[hh], NEG)) for hh in hrange]
        alpha = [jnp.exp2(m[hh] - m_new[hh]) for hh in hrange]
        p = [jnp.exp2(s[hh] - jnp.where(selj[hh], m_new[hh] + shift[hh], jnp.inf)) for hh in hrange]
        l = [alpha[hh] * l[hh] + jnp.sum(p[hh], axis=0, keepdims=True) for hh in hrange]
        pv = weighted_values(j, p)
        acc = [alpha[hh] * acc[hh] + pv[hh] for hh in hrange]
        return m_new, l, acc

    m, l, acc = lax.fori_loop(0, qi, body, (m, l, acc))
    for hh in hrange:
        o_ref[0, :, hh * hd:(hh + 1) * hd] = (acc[hh] / l[hh]).T


def _moba(proj, batch, seq, heads):
    nb = seq // MOBA_BLOCK
    hd = ATTN_HEAD_DIM
    hw = heads * hd
    groups = ATTN_HEADS // heads
    kern = lambda *refs: _moba_kernel(*refs, nb=nb, heads=heads)
    return pl.pallas_call(
        kern,
        grid=(batch, groups, nb),
        in_specs=[pl.BlockSpec((1, MOBA_BLOCK, hw), lambda b, h, i: (b, i, h)),
                  pl.BlockSpec((1, seq, hw), lambda b, h, i: (b, 0, groups + h)),
                  pl.BlockSpec((1, seq, hw), lambda b, h, i: (b, 0, 2 * groups + h))],
        out_specs=pl.BlockSpec((1, MOBA_BLOCK, hw), lambda b, h, i: (b, i, h)),
        out_shape=jax.ShapeDtypeStruct((batch, seq, ATTN_WIDTH), F32),
        scratch_shapes=[pltpu.VMEM((heads, nb, hd), F32),
                        pltpu.VMEM((heads, seq, hd), BF16),
                        pltpu.VMEM((heads, hd, seq), BF16)],
        compiler_params=_params(("parallel", "parallel", "arbitrary")),
        name="moba",
    )(proj, proj, proj)


def _rwkv_prep_kernel(r_ref, k_ref, v_ref, l_ref, rp_ref, kp_ref, vp_ref, lp_ref,
                      mu_r, mu_k, mu_v, mu_l, w0, w2p, a0, a2p, g2p, kk_w, ka_w, rk_w, bd_ref,
                      r_o, lw_o, k_o, v_o, kk_o, b_o, g_o, bonus_o):
    first = pl.program_id(1) == 0

    def shift(cur_ref, prev_ref, mu_ref):
        cur = cur_ref[0]
        prev_last = jnp.where(first, 0.0, prev_ref[0, 7:8, :])
        rows = lax.broadcasted_iota(jnp.int32, cur.shape, 0)
        prev = jnp.where(rows == 0, prev_last, pltpu.roll(cur, 1, axis=0))
        return cur + (prev - cur) * mu_ref[...]

    r = shift(r_ref, rp_ref, mu_r)
    k = shift(k_ref, kp_ref, mu_k)
    v = shift(v_ref, vp_ref, mu_v)
    lo = shift(l_ref, lp_ref, mu_l)

    bd = bd_ref[...]
    z = -(w0[...] + _dot(jnp.tanh(lo), w2p[...]))
    softplus = jnp.maximum(z, 0.0) + jnp.log1p(jnp.exp(-jnp.abs(z)))
    lw = -jnp.exp(-softplus - 0.5)
    a = jax.nn.sigmoid(a0[...] + _dot(lo, a2p[...]))
    g = _dot(jax.nn.sigmoid(lo), g2p[...])
    kk = k * kk_w[...]
    kk = kk / jnp.maximum(jnp.sqrt(_dot_exact_rhs(kk * kk, bd)), 1e-12)
    k2 = k * (1.0 + (a - 1.0) * ka_w[...])
    r_o[0] = r
    lw_o[0] = lw
    k_o[0] = k2
    v_o[0] = v
    kk_o[0] = kk
    b_o[0] = kk * a
    g_o[0] = g
    bonus_o[0] = _dot_exact_rhs(r * k2 * rk_w[...], bd) * v


def _rwkv_prep(proj, batch, seq, weights, tt):
    w = RWKV_WIDTH
    c0 = 3 * ATTN_WIDTH // w
    lc = (3 * ATTN_WIDTH + 3 * w) // LORA_PAD
    pb = tt // 8
    cur = lambda width, col: pl.BlockSpec((1, tt, width), lambda b, i: (b, i, col))
    prev = lambda width, col: pl.BlockSpec(
        (1, 8, width), lambda b, i: (b, jnp.maximum(i * pb - 1, 0), col))
    full = lambda a: pl.BlockSpec(a.shape, lambda b, i: (0,) * a.ndim)
    out = pl.BlockSpec((1, tt, w), lambda b, i: (b, i, 0))
    return pl.pallas_call(
        _rwkv_prep_kernel,
        grid=(batch, seq // tt),
        in_specs=[cur(w, c0), cur(w, c0 + 1), cur(w, c0 + 2), cur(LORA_PAD, lc),
                  prev(w, c0), prev(w, c0 + 1), prev(w, c0 + 2), prev(LORA_PAD, lc)]
                 + [full(a) for a in weights],
        out_specs=[out] * 8,
        out_shape=[jax.ShapeDtypeStruct((batch, seq, w), F32)] * 8,
        compiler_params=_params(("parallel", "arbitrary")),
        name="rwkv_prep",
    )(proj, proj, proj, proj, proj, proj, proj, proj, *weights)


def _scan_kernel(r_ref, lw_ref, k_ref, v_ref, kk_ref, b_ref, y_ref, s_ref, *, tb, pairs):
    c = SCAN_CHUNK
    c2 = 2 * c

    @pl.when(pl.program_id(1) == 0)
    def _():
        s_ref[...] = jnp.zeros_like(s_ref)

    head0 = lax.broadcasted_iota(jnp.int32, (c, LANES), 1) < RWKV_HEAD_DIM
    row = lax.broadcasted_iota(jnp.int32, (c2, c2), 0)
    col = lax.broadcasted_iota(jnp.int32, (c2, c2), 1)
    same_head = (row // c) == (col // c)
    strict = jnp.logical_and(same_head, col < row)
    incl = jnp.logical_and(same_head, col <= row)
    inv_blk = (row // SCAN_INV_BLOCK) == (col // SCAN_INV_BLOCK)
    eye = jnp.where(row == col, 1.0, 0.0).astype(F32)
    tri = jnp.where(lax.broadcasted_iota(jnp.int32, (c, c), 0)
                    >= lax.broadcasted_iota(jnp.int32, (c, c), 1), 1.0, 0.0).astype(BF16)

    def stack(x):
        return jnp.concatenate([jnp.where(head0, x, 0.0), jnp.where(head0, 0.0, x)], axis=0)

    def pair_chunk(t0, p):
        lanes = slice(p * LANES, (p + 1) * LANES)
        rows = pl.ds(t0, c)
        lw = lw_ref[0, rows, lanes]
        cum = _dot_exact_lhs(tri, lw)
        yield
        r = r_ref[0, rows, lanes]
        k = k_ref[0, rows, lanes]
        kk = kk_ref[0, rows, lanes]
        b = b_ref[0, rows, lanes]
        cum_end = cum[c - 1:c, :]
        e_neg = jnp.exp(-cum)
        e_end = jnp.exp(cum_end - cum)
        a_s = stack(-kk * jnp.exp(cum - lw))
        r_s = stack(r * jnp.exp(cum))
        b_s = stack(b * e_neg)
        k_s = stack(k * e_neg)
        bend_s = stack(b * e_end)
        kend_s = stack(k * e_end)
        v_s = stack(v_ref[0, rows, lanes])
        gram = _dot_nt(jnp.concatenate([a_s, r_s], axis=0),
                       jnp.concatenate([b_s, k_s], axis=0))
        kv = _dot_tn(v_s, kend_s)
        yield
        l_ab = jnp.where(strict, gram[:c2, :c2], 0.0)
        l_ak = jnp.where(strict, gram[:c2, c2:], 0.0)
        l_rb = jnp.where(incl, gram[c2:, :c2], 0.0)
        l_rk = jnp.where(incl, gram[c2:, c2:], 0.0)

        x = jnp.where(inv_blk, l_ab, 0.0)
        off = jnp.where(inv_blk, 0.0, l_ab)
        dinv = eye + x
        w1 = _dot(l_ak, v_s)
        y1 = _dot(l_rk, v_s)
        for _ in range(int(math.log2(SCAN_INV_BLOCK)) - 1):
            x = _dot(x, x)
            yield
            dinv = dinv + _dot(dinv, x)
            yield
        f = _dot(dinv, off)
        yield
        f2 = _dot(f, f)
        yield
        gm = eye + f
        gm = gm + _dot(gm, f2)
        yield
        t_inv = _dot(gm, dinv)
        yield
        ta = _dot(t_inv, jnp.concatenate([a_s, w1], axis=1))
        yield
        a_hat = ta[:, :LANES]
        v_hat = ta[:, LANES:]
        state = s_ref[p]
        m1 = _dot_nt(jnp.concatenate([a_hat, r_s], axis=0), state)
        yield
        u = m1[:c2] + v_hat
        y = m1[c2:] + y1 + _dot(l_rb, u)
        s_ref[p] = state * jnp.exp(cum_end) + kv + _dot_tn(u, bend_s)
        y_ref[0, rows, lanes] = y[:c] + y[c:]

    def chunk(ci, carry):
        t0 = pl.multiple_of(ci * c, c)
        for _ in itertools.zip_longest(*[pair_chunk(t0, p) for p in range(pairs)]):
            pass
        return carry

    lax.fori_loop(0, tb // c, chunk, 0)


def _rwkv_scan(r, lw, k, v, kk, b, tb):
    batch, seq, w = r.shape
    pairs = w // LANES
    spec = pl.BlockSpec((1, tb, w), lambda bi, t: (bi, t, 0))
    kern = lambda *refs: _scan_kernel(*refs, tb=tb, pairs=pairs)
    return pl.pallas_call(
        kern,
        grid=(batch, seq // tb),
        in_specs=[spec] * 6,
        out_specs=spec,
        out_shape=jax.ShapeDtypeStruct((batch, seq, w), F32),
        scratch_shapes=[pltpu.VMEM((pairs, LANES, LANES), F32)],
        compiler_params=_params(("parallel", "arbitrary")),
        name="rwkv_scan",
    )(r, lw, k, v, kk, b)


def _out_proj_kernel(x_ref, attn_ref, y_ref, bonus_ref, g_ref, lng, lnb, bd_ref, w_ref, n2g,
                     h_ref, hn_ref, hnt_ref):
    bd = bd_ref[...]
    y = y_ref[...]
    inv_n = 1.0 / RWKV_HEAD_DIM
    d = y - _dot(y, bd) * inv_n
    var = _dot(d * d, bd) * inv_n
    yn = d * lax.rsqrt(var + RWKV_GN_EPS) * lng[...] + lnb[...]
    rw = (yn + bonus_ref[...]) * g_ref[...]
    h = (x_ref[...] + _dot(attn_ref[...], w_ref[:ATTN_WIDTH, :])
         + _dot(rw, w_ref[ATTN_WIDTH:, :]))
    h_ref[...] = h
    inv = lax.rsqrt(jnp.mean(h * h, axis=-1, keepdims=True) + RMS_EPS)
    hn = h * inv * n2g[...]
    hn_ref[...] = hn.astype(BF16)
    hnt_ref[...] = hn.T.astype(BF16)


def _out_proj(x, attn, y, bonus, g, lng, lnb, bd, w_out, n2g, tm):
    m = x.shape[0]
    row = lambda width: pl.BlockSpec((tm, width), lambda i: (i, 0))
    full = lambda a: pl.BlockSpec(a.shape, lambda i: (0,) * a.ndim)
    return pl.pallas_call(
        _out_proj_kernel,
        grid=(m // tm,),
        in_specs=[row(D_MODEL), row(ATTN_WIDTH), row(RWKV_WIDTH), row(RWKV_WIDTH), row(RWKV_WIDTH),
                  full(lng), full(lnb), full(bd), full(w_out), full(n2g)],
        out_specs=[row(D_MODEL), row(D_MODEL), pl.BlockSpec((D_MODEL, tm), lambda i: (0, i))],
        out_shape=[jax.ShapeDtypeStruct((m, D_MODEL), F32),
                   jax.ShapeDtypeStruct((m, D_MODEL), BF16),
                   jax.ShapeDtypeStruct((D_MODEL, m), BF16)],
        compiler_params=_params(("parallel",)),
        name="out_proj",
    )(x, attn, y, bonus, g, lng, lnb, bd, w_out, n2g)


def _peer_q_kernel(hn_ref, wq_ref, keys_ref, st_ref):
    q = jnp.dot(hn_ref[...], wq_ref[...], preferred_element_type=F32)
    for hp in range(2 * PEER_HEADS):
        qs = q[:, hp * PEER_HALF:(hp + 1) * PEER_HALF]
        st_ref[hp] = _dot_nt(keys_ref[hp], qs)


def _peer_q(hn, wq, keys, tm):
    m = hn.shape[0]
    return pl.pallas_call(
        _peer_q_kernel,
        grid=(m // tm,),
        in_specs=[pl.BlockSpec((tm, D_MODEL), lambda i: (i, 0)),
                  pl.BlockSpec(wq.shape, lambda i: (0, 0)),
                  pl.BlockSpec(keys.shape, lambda i: (0, 0, 0))],
        out_specs=pl.BlockSpec((2 * PEER_HEADS, PEER_NKEYS, tm), lambda i: (0, 0, i)),
        out_shape=jax.ShapeDtypeStruct((2 * PEER_HEADS, PEER_NKEYS, m), F32),
        compiler_params=_params(("parallel",)),
        name="peer_q",
    )(hn, wq, keys)


def _top_desc(x, k, exact_ties):
    n = x.shape[0]
    rows = lax.broadcasted_iota(jnp.int32, x.shape, 0)
    rank = jnp.full(x.shape, float(k), F32)
    outs = []
    for i in range(k):
        m = jnp.max(x, axis=0, keepdims=True)
        outs.append(m)
        pick = x == m
        if exact_ties:
            pick = rows == jnp.min(jnp.where(pick, rows, n), axis=0, keepdims=True)
        x = jnp.where(pick, -jnp.inf, x)
        rank = jnp.where(pick, float(i), rank)
    gone = jnp.sum(jnp.where(x == -jnp.inf, 1.0, 0.0), axis=0, keepdims=True)
    return jnp.concatenate(outs, axis=0), rank, gone


def _peer_topk_kernel(s_ref, cnt1_ref, e1_ref, rank2_ref, e2_ref, *, n1):
    k = PEER_TOPK
    widths = [k // (a + 1) for a in range(k)]
    pad = -sum(widths) % 8

    def run(exact_ties):
        s1 = s_ref[0]
        s2 = s_ref[1]
        sv1, rank1, gone1 = _top_desc(s1, k, exact_ties)
        sv2, rank2, gone2 = _top_desc(s2, k, exact_ties)
        cand = jnp.concatenate([sv1[a:a + 1] + sv2[:widths[a]] for a in range(k)]
                               + [jnp.full((pad, s1.shape[1]), -jnp.inf, F32)], axis=0)
        best, pos, gone3 = _top_desc(cand, k, exact_ties)
        chosen = jnp.where(pos < float(k), 1.0, 0.0)
        z = jnp.sum(jnp.exp(best - best[0:1]), axis=0, keepdims=True)
        cnt1 = jnp.zeros(s1.shape, F32)
        row = 0
        for a in range(k):
            cnt_a = jnp.sum(chosen[row:row + widths[a]], axis=0, keepdims=True)
            cnt1 = jnp.where(rank1 == float(a), cnt_a, cnt1)
            row += widths[a]
        e1 = jnp.exp(s1 - sv1[0:1]) / z
        for t in range(PEER_NKEYS // n1):
            cnt1_ref[0, t] = cnt1[t * n1:(t + 1) * n1]
            e1_ref[0, t] = e1[t * n1:(t + 1) * n1]
        rank2_ref[0] = rank2
        e2_ref[0] = jnp.exp(s2 - sv2[0:1])
        return jnp.max(jnp.maximum(jnp.maximum(gone1, gone2), gone3 - float(pad)))

    most_gone = run(exact_ties=False)

    @pl.when(most_gone > float(k))
    def _():
        run(exact_ties=True)


def _peer_topk(st, tl, n1):
    m = st.shape[2]
    n_tiles = PEER_NKEYS // n1
    spec = pl.BlockSpec((1, PEER_NKEYS, tl), lambda h, i: (h, 0, i))
    shape = jax.ShapeDtypeStruct((PEER_HEADS, PEER_NKEYS, m), F32)
    tile_spec = pl.BlockSpec((1, n_tiles, n1, tl), lambda h, i: (h, 0, 0, i))
    tile_shape = jax.ShapeDtypeStruct((PEER_HEADS, n_tiles, n1, m), F32)
    kern = lambda *refs: _peer_topk_kernel(*refs, n1=n1)
    return pl.pallas_call(
        kern,
        grid=(PEER_HEADS, m // tl),
        in_specs=[pl.BlockSpec((2, PEER_NKEYS, tl), lambda h, i: (h, 0, i))],
        out_specs=[tile_spec, tile_spec, spec, spec],
        out_shape=[tile_shape, tile_shape, shape, shape],
        compiler_params=_params(("parallel", "parallel")),
        name="peer_topk",
    )(st)


def _peer_dense_kernel(u_ref, hnt_ref, vt_ref, rank2_ref, e2_ref, cnta_ref, cntb_ref,
                       e1a_ref, e1b_ref, yt_ref, act_ref, hpre_ref, *, n1):
    step = pl.program_id(1)
    last_step = pl.num_programs(1) - 1
    et = hpre_ref.shape[1]
    row_refs = ((cnta_ref, e1a_ref), (cntb_ref, e1b_ref))

    def gate_tile(half, jt, ib):
        prev = 1 - half
        cnt_ref, e1r_ref = row_refs[half]
        ls = slice(jt * LANES, (jt + 1) * LANES)
        rs = slice(ib * DENSE_ROWS, (ib + 1) * DENSE_ROWS)
        gates = [jnp.zeros((DENSE_ROWS, LANES), F32) for _ in range(n1)]
        for h in range(PEER_HEADS):
            r2 = rank2_ref[h, rs, ls]
            e2 = e2_ref[h, rs, ls]
            for a in range(n1):
                hit = r2 < cnt_ref[h, 0, a:a + 1, ls]
                gates[a] = gates[a] + jnp.where(hit, e2, 0.0) * e1r_ref[h, 0, a:a + 1, ls]
        for a in range(n1):
            ers = slice(a * PEER_NKEYS + ib * DENSE_ROWS, a * PEER_NKEYS + (ib + 1) * DENSE_ROWS)
            x = hpre_ref[prev, ers, ls]
            gelu = 0.5 * x * (1.0 + lax.erf(x * (0.5 ** 0.5)))
            act_ref[prev, ers, ls] = gelu * gates[a]

    tt = hpre_ref.shape[2]
    n_ib = PEER_NKEYS // DENSE_ROWS

    def run(first, last):
        for half in range(2):
            cur = half
            hs = slice(half * et, (half + 1) * et)
            gated = not (first and half == 0) and not (last and half == 1)
            for piece in range(tt // MXU_COLS):
                cs = slice(piece * MXU_COLS, (piece + 1) * MXU_COLS)
                tiles = [(jt, ib)
                         for jt in range(piece * MXU_COLS // LANES, (piece + 1) * MXU_COLS // LANES)
                         for ib in range(n_ib)]
                if not last:
                    hpre_ref[cur, :, cs] = jnp.dot(u_ref[hs, :], hnt_ref[:, cs],
                                                   preferred_element_type=F32)
                per_down = len(tiles) * DOWN_ROWS // D_MODEL
                for i, tile in enumerate(tiles):
                    if not first and i % per_down == per_down // 2:
                        ms = slice(i // per_down * DOWN_ROWS, (i // per_down + 1) * DOWN_ROWS)
                        yt_ref[ms, cs] += jnp.dot(vt_ref[ms, hs],
                                                  act_ref[cur, :, cs].astype(BF16),
                                                  preferred_element_type=F32)
                    if gated:
                        gate_tile(half, *tile)

    @pl.when(step == 0)
    def _():
        yt_ref[...] = jnp.zeros_like(yt_ref)
        run(first=True, last=False)

    @pl.when(jnp.logical_and(step > 0, step < last_step))
    def _():
        run(first=False, last=False)

    @pl.when(step == last_step)
    def _():
        run(first=False, last=True)


def _peer_dense(u, hnt, vt, cntr, e1r, rank2, e2, tt, n1):
    m = hnt.shape[1]
    et = n1 * PEER_NKEYS
    n_steps = PEER_NKEYS // (2 * n1)
    n_tiles = 2 * n_steps
    tok3 = lambda a: pl.BlockSpec((a.shape[0], a.shape[1], tt), lambda i, e: (0, 0, i))
    rows = lambda half: pl.BlockSpec(
        (PEER_HEADS, 1, n1, tt),
        lambda i, e: (0, jnp.clip(2 * e + half - 1, 0, n_tiles - 1), 0, i))
    kern = lambda *refs: _peer_dense_kernel(*refs, n1=n1)
    return pl.pallas_call(
        kern,
        grid=(m // tt, n_steps + 1),
        in_specs=[pl.BlockSpec((2 * et, D_MODEL), lambda i, e: (jnp.minimum(e, n_steps - 1), 0)),
                  pl.BlockSpec((D_MODEL, tt), lambda i, e: (0, i)),
                  pl.BlockSpec((D_MODEL, 2 * et), lambda i, e: (0, jnp.maximum(e - 1, 0))),
                  tok3(rank2), tok3(e2), rows(0), rows(1), rows(0), rows(1)],
        out_specs=pl.BlockSpec((D_MODEL, tt), lambda i, e: (0, i)),
        out_shape=jax.ShapeDtypeStruct((D_MODEL, m), F32),
        scratch_shapes=[pltpu.VMEM((2, et, tt), F32), pltpu.VMEM((2, et, tt), F32)],
        compiler_params=_params(("parallel", "arbitrary")),
        name="peer_dense",
    )(u, hnt, vt, rank2, e2, cntr, cntr, e1r, e1r)


def _transpose_cast_kernel(x_ref, o_ref):
    o_ref[...] = x_ref[...].T.astype(o_ref.dtype)


def _transpose_cast(x, tr):
    rows, cols = x.shape
    return pl.pallas_call(
        _transpose_cast_kernel,
        grid=(rows // tr,),
        in_specs=[pl.BlockSpec((tr, cols), lambda i: (i, 0))],
        out_specs=pl.BlockSpec((cols, tr), lambda i: (0, i)),
        out_shape=jax.ShapeDtypeStruct((cols, rows), BF16),
        compiler_params=_params(("parallel",)),
        name="transpose_cast",
    )(x)


def _final_kernel(h_ref, yt_ref, g_ref, o_ref):
    h = h_ref[...] + yt_ref[...].T
    inv = lax.rsqrt(jnp.mean(h * h, axis=-1, keepdims=True) + RMS_EPS)
    o_ref[...] = h * inv * g_ref[...]


def _final(h, yt, g, tm):
    m = h.shape[0]
    row = pl.BlockSpec((tm, D_MODEL), lambda i: (i, 0))
    return pl.pallas_call(
        _final_kernel,
        grid=(m // tm,),
        in_specs=[row, pl.BlockSpec((D_MODEL, tm), lambda i: (0, i)),
                  pl.BlockSpec((1, D_MODEL), lambda i: (0, 0))],
        out_specs=row,
        out_shape=jax.ShapeDtypeStruct((m, D_MODEL), F32),
        compiler_params=_params(("parallel",)),
        name="final_norm",
    )(h, yt, g)


def _row(a):
    return a.reshape(1, -1).astype(F32)


def _pad_rows(w, start, total):
    return jnp.zeros((total, w.shape[1]), BF16).at[start:start + w.shape[0]].set(w.astype(BF16))


def kernel(x, norm1_g, w_in, rwkv_mu, rwkv_w0, rwkv_w2, rwkv_a0, rwkv_a2, rwkv_g2, rwkv_k_k, rwkv_k_a, rwkv_r_k, rwkv_ln_g, rwkv_ln_b, w_out, norm2_g, peer_wq, peer_sub_keys, peer_u, peer_v, final_g):
    batch, seq, d = x.shape
    tokens = batch * seq
    w = RWKV_WIDTH
    xt = x.reshape(tokens, d)

    w_in_p = jnp.pad(w_in[0].astype(BF16), ((0, 0), (0, IN_COLS_PAD - IN_COLS)))
    proj = _norm_matmul(xt, _row(norm1_g[0]), w_in_p, tm=1024, tn=512)
    proj = proj.reshape(batch, seq, IN_COLS_PAD)

    attn = _moba(proj, batch, seq, heads=4)

    mu = rwkv_mu[0]
    mu_l = jnp.pad(mu[3 * w:], (0, LORA_PAD - LORA_COLS))
    hid = lax.broadcasted_iota(jnp.int32, (w, w), 0) // RWKV_HEAD_DIM
    bd = (hid == hid.T).astype(BF16)
    prep_w = [_row(mu[:w]), _row(mu[w:2 * w]), _row(mu[2 * w:3 * w]), _row(mu_l),
              _row(rwkv_w0[0]), _pad_rows(rwkv_w2[0], 0, LORA_PAD),
              _row(rwkv_a0[0]), _pad_rows(rwkv_a2[0], DECAY_LORA, LORA_PAD),
              _pad_rows(rwkv_g2[0], DECAY_LORA + AAA_LORA, LORA_PAD),
              _row(rwkv_k_k[0]), _row(rwkv_k_a[0]), _row(rwkv_r_k[0]), bd]
    r, lw, k2, v, kk, b, g, bonus = _rwkv_prep(proj, batch, seq, prep_w, tt=256)
    y = _rwkv_scan(r, lw, k2, v, kk, b, tb=256)

    flat = lambda a: a.reshape(tokens, a.shape[-1])
    h, hn, hnt = _out_proj(xt, flat(attn), flat(y), flat(bonus), flat(g), _row(rwkv_ln_g[0]),
                      _row(rwkv_ln_b[0]), bd, w_out[0].astype(BF16), _row(norm2_g[0]), tm=256)

    keys = peer_sub_keys[0].reshape(2 * PEER_HEADS, PEER_NKEYS, PEER_HALF).astype(BF16)
    st = _peer_q(hn, peer_wq[0].astype(BF16), keys, tm=512)
    cnt1, e1, rank2, e2 = _peer_topk(st, tl=512, n1=4)
    yt = _peer_dense(peer_u[0].astype(BF16), hnt, _transpose_cast(peer_v[0], tr=512),
                     cnt1, e1, rank2, e2, tt=512, n1=4)
    out = _final(h, yt, final_g.reshape(1, d), tm=512)
    return out.reshape(batch, seq, d)
```

```python
import itertools
import math

import jax
import jax.numpy as jnp
from jax import lax
from jax.experimental import pallas as pl
from jax.experimental.pallas import tpu as pltpu

F32 = jnp.float32
BF16 = jnp.bfloat16

D_MODEL = 2048
ATTN_HEADS = 8
ATTN_HEAD_DIM = 128
ATTN_WIDTH = ATTN_HEADS * ATTN_HEAD_DIM
MOBA_BLOCK = 256
MOBA_TOPK = 3

RWKV_HEAD_DIM = 64
RWKV_WIDTH = D_MODEL - ATTN_WIDTH
RWKV_HEADS = RWKV_WIDTH // RWKV_HEAD_DIM
DECAY_LORA = 96
AAA_LORA = 96
GATE_LORA = 256
LORA_COLS = DECAY_LORA + AAA_LORA + GATE_LORA
LORA_PAD = 512
RWKV_GN_EPS = RWKV_HEAD_DIM * 1e-5
IN_COLS = 3 * ATTN_WIDTH + 3 * RWKV_WIDTH + LORA_COLS
IN_COLS_PAD = 3 * ATTN_WIDTH + 3 * RWKV_WIDTH + LORA_PAD

PEER_HEADS = 8
PEER_NKEYS = 128
PEER_HALF = 128
PEER_TOPK = 16

RMS_EPS = 1e-6
NEG = -1e30

LANES = 128
SCAN_CHUNK = 64
SCAN_INV_BLOCK = 16
DOWN_ROWS = 2048
MXU_COLS = 256
DENSE_ROWS = 32
VMEM_LIMIT = 56 * 1024 * 1024

_NT = (((1,), (1,)), ((), ()))
_TN = (((0,), (0,)), ((), ()))


def _params(sem, flags=None):
    return pltpu.CompilerParams(dimension_semantics=sem, vmem_limit_bytes=VMEM_LIMIT, flags=flags)


def _dot(a, b):
    return jnp.dot(a.astype(BF16), b.astype(BF16), preferred_element_type=F32)


def _dot_nt(a, b):
    return lax.dot_general(a.astype(BF16), b.astype(BF16), _NT, preferred_element_type=F32)


def _dot_tn(a, b):
    return lax.dot_general(a.astype(BF16), b.astype(BF16), _TN, preferred_element_type=F32)


def _split3(x):
    hi = x.astype(BF16)
    r1 = x - hi.astype(F32)
    mid = r1.astype(BF16)
    lo = (r1 - mid.astype(F32)).astype(BF16)
    return hi, mid, lo


def _dot_exact_rhs(x, ones_bf16):
    hi = x.astype(BF16)
    lo = (x - hi.astype(F32)).astype(BF16)
    d = lambda p: jnp.dot(p, ones_bf16, preferred_element_type=F32)
    return d(hi) + d(lo)


def _dot_exact_lhs(ones_bf16, x):
    hi, mid, lo = _split3(x)
    d = lambda p: jnp.dot(ones_bf16, p, preferred_element_type=F32)
    return d(hi) + d(mid) + d(lo)


def _norm_matmul_kernel(x_ref, g_ref, w_ref, o_ref, xn_ref):
    @pl.when(pl.program_id(1) == 0)
    def _():
        x = x_ref[...]
        inv = lax.rsqrt(jnp.mean(x * x, axis=-1, keepdims=True) + RMS_EPS)
        xn_ref[...] = (x * inv * g_ref[...]).astype(BF16)

    o_ref[...] = jnp.dot(xn_ref[...], w_ref[...], preferred_element_type=F32)


def _norm_matmul(x, g, w, tm, tn):
    m, k = x.shape
    n = w.shape[1]
    return pl.pallas_call(
        _norm_matmul_kernel,
        grid=(m // tm, n // tn),
        in_specs=[pl.BlockSpec((tm, k), lambda i, j: (i, 0)),
                  pl.BlockSpec((1, k), lambda i, j: (0, 0)),
                  pl.BlockSpec((k, tn), lambda i, j: (0, j))],
        out_specs=pl.BlockSpec((tm, tn), lambda i, j: (i, j)),
        out_shape=jax.ShapeDtypeStruct((m, n), F32),
        scratch_shapes=[pltpu.VMEM((tm, k), BF16)],
        compiler_params=_params(("parallel", "arbitrary")),
        name="in_proj",
    )(x, g, w)


def _moba_kernel(q_ref, k_ref, v_ref, o_ref, kmean_ref, kb_ref, vt_ref, *, nb, heads):
    blk_len = MOBA_BLOCK
    hd = ATTN_HEAD_DIM
    hg = pl.program_id(1)
    qi = pl.program_id(2)
    hrange = range(heads)

    @pl.when(qi == 0)
    def _():
        for hh in hrange:
            hl = slice(hh * hd, (hh + 1) * hd)
            for j in range(nb):
                rows = slice(j * blk_len, (j + 1) * blk_len)
                kj = k_ref[0, rows, hl]
                kmean_ref[hh, j:j + 1, :] = jnp.mean(kj, axis=0, keepdims=True)
                kb_ref[hh, rows, :] = kj.astype(BF16)
                vt_ref[hh, :, rows] = v_ref[0, rows, hl].T.astype(BF16)

    log2e = math.log2(math.e)
    scale2 = (hd ** -0.5) * log2e
    kq = (lax.broadcasted_iota(jnp.int32, (blk_len, blk_len), 1)
          - lax.broadcasted_iota(jnp.int32, (blk_len, blk_len), 0))
    kqf = kq.astype(F32)
    blk = lax.broadcasted_iota(jnp.int32, (nb, blk_len), 0)
    past = blk < qi

    def select(hh):
        q = q_ref[0, :, hh * hd:(hh + 1) * hd]
        gate = lax.dot_general(kmean_ref[hh], q, _NT, precision=lax.Precision.HIGHEST,
                               preferred_element_type=F32)
        g = jnp.where(past, gate, NEG)
        selw = jnp.zeros(gate.shape, F32)
        for _ in range(min(MOBA_TOPK, nb)):
            m = jnp.max(g, axis=0, keepdims=True)
            first = jnp.min(jnp.where(g == m, blk, nb), axis=0, keepdims=True)
            pick = blk == first
            selw = jnp.where(pick, 1.0, selw)
            g = jnp.where(pick, -jnp.inf, g)
        return jnp.where(past, selw, 0.0), (q * scale2).astype(BF16)

    sel_q = [select(hh) for hh in hrange]
    selw = [x[0] for x in sel_q]
    qb = [x[1] for x in sel_q]
    slope2 = [jnp.exp2(jnp.zeros((1, 1), F32) - (hg * heads + hh + 1).astype(F32)) * log2e
              for hh in hrange]
    bias0 = [slope2[hh] * kqf for hh in hrange]

    def scores(j):
        r0 = pl.multiple_of(j * blk_len, blk_len)
        return [lax.dot_general(kb_ref[hh, pl.ds(r0, blk_len), :], qb[hh], _NT,
                                preferred_element_type=F32) - bias0[hh] for hh in hrange]

    def weighted_values(j, p):
        r0 = pl.multiple_of(j * blk_len, blk_len)
        return [jnp.dot(vt_ref[hh, :, pl.ds(r0, blk_len)], p[hh].astype(BF16),
                        preferred_element_type=F32) for hh in hrange]

    s = [jnp.where(kq >= 0, x, NEG) for x in scores(qi)]
    m = [jnp.max(x, axis=0, keepdims=True) for x in s]
    p = [jnp.exp2(s[hh] - m[hh]) for hh in hrange]
    l = [jnp.sum(x, axis=0, keepdims=True) for x in p]
    acc = weighted_values(qi, p)

    def body(j, carry):
        m, l, acc = carry
        off = ((qi - j) * blk_len).astype(F32)
        s = scores(j)
        selj = [jnp.max(jnp.where(blk == j, selw[hh], 0.0), axis=0, keepdims=True) > 0.0
                for hh in hrange]
        shift = [slope2[hh] * off for hh in hrange]
        top = [jnp.max(s[hh], axis=0, keepdims=True) - shift[hh] for hh in hrange]
        m_new = [jnp.maximum(m[hh], jnp.where(selj[hh], top[hh], NEG)) for hh in hrange]
        alpha = [jnp.exp2(m[hh] - m_new[hh]) for hh in hrange]
        p = [jnp.exp2(s[hh] - jnp.where(selj[hh], m_new[hh] + shift[hh], jnp.inf)) for hh in hrange]
        l = [alpha[hh] * l[hh] + jnp.sum(p[hh], axis=0, keepdims=True) for hh in hrange]
        pv = weighted_values(j, p)
        acc = [alpha[hh] * acc[hh] + pv[hh] for hh in hrange]
        return m_new, l, acc

    m, l, acc = lax.fori_loop(0, qi, body, (m, l, acc))
    for hh in hrange:
        o_ref[0, :, hh * hd:(hh + 1) * hd] = (acc[hh] / l[hh]).T


def _moba(proj, batch, seq, heads):
    nb = seq // MOBA_BLOCK
    hd = ATTN_HEAD_DIM
    hw = heads * hd
    groups = ATTN_HEADS // heads
    kern = lambda *refs: _moba_kernel(*refs, nb=nb, heads=heads)
    return pl.pallas_call(
        kern,
        grid=(batch, groups, nb),
        in_specs=[pl.BlockSpec((1, MOBA_BLOCK, hw), lambda b, h, i: (b, i, h)),
                  pl.BlockSpec((1, seq, hw), lambda b, h, i: (b, 0, groups + h)),
                  pl.BlockSpec((1, seq, hw), lambda b, h, i: (b, 0, 2 * groups + h))],
        out_specs=pl.BlockSpec((1, MOBA_BLOCK, hw), lambda b, h, i: (b, i, h)),
        out_shape=jax.ShapeDtypeStruct((batch, seq, ATTN_WIDTH), F32),
        scratch_shapes=[pltpu.VMEM((heads, nb, hd), F32),
                        pltpu.VMEM((heads, seq, hd), BF16),
                        pltpu.VMEM((heads, hd, seq), BF16)],
        compiler_params=_params(("parallel", "parallel", "arbitrary")),
        name="moba",
    )(proj, proj, proj)


def _rwkv_prep_kernel(r_ref, k_ref, v_ref, l_ref, rp_ref, kp_ref, vp_ref, lp_ref,
                      mu_r, mu_k, mu_v, mu_l, w0, w2p, a0, a2p, g2p, kk_w, ka_w, rk_w, bd_ref,
                      r_o, lw_o, k_o, v_o, kk_o, b_o, g_o, bonus_o):
    first = pl.program_id(1) == 0

    def shift(cur_ref, prev_ref, mu_ref):
        cur = cur_ref[0]
        prev_last = jnp.where(first, 0.0, prev_ref[0, 7:8, :])
        rows = lax.broadcasted_iota(jnp.int32, cur.shape, 0)
        prev = jnp.where(rows == 0, prev_last, pltpu.roll(cur, 1, axis=0))
        return cur + (prev - cur) * mu_ref[...]

    r = shift(r_ref, rp_ref, mu_r)
    k = shift(k_ref, kp_ref, mu_k)
    v = shift(v_ref, vp_ref, mu_v)
    lo = shift(l_ref, lp_ref, mu_l)

    bd = bd_ref[...]
    z = -(w0[...] + _dot(jnp.tanh(lo), w2p[...]))
    softplus = jnp.maximum(z, 0.0) + jnp.log1p(jnp.exp(-jnp.abs(z)))
    lw = -jnp.exp(-softplus - 0.5)
    a = jax.nn.sigmoid(a0[...] + _dot(lo, a2p[...]))
    g = _dot(jax.nn.sigmoid(lo), g2p[...])
    kk = k * kk_w[...]
    kk = kk / jnp.maximum(jnp.sqrt(_dot_exact_rhs(kk * kk, bd)), 1e-12)
    k2 = k * (1.0 + (a - 1.0) * ka_w[...])
    r_o[0] = r
    lw_o[0] = lw
    k_o[0] = k2
    v_o[0] = v
    kk_o[0] = kk
    b_o[0] = kk * a
    g_o[0] = g
    bonus_o[0] = _dot_exact_rhs(r * k2 * rk_w[...], bd) * v


def _rwkv_prep(proj, batch, seq, weights, tt):
    w = RWKV_WIDTH
    c0 = 3 * ATTN_WIDTH // w
    lc = (3 * ATTN_WIDTH + 3 * w) // LORA_PAD
    pb = tt // 8
    cur = lambda width, col: pl.BlockSpec((1, tt, width), lambda b, i: (b, i, col))
    prev = lambda width, col: pl.BlockSpec(
        (1, 8, width), lambda b, i: (b, jnp.maximum(i * pb - 1, 0), col))
    full = lambda a: pl.BlockSpec(a.shape, lambda b, i: (0,) * a.ndim)
    out = pl.BlockSpec((1, tt, w), lambda b, i: (b, i, 0))
    return pl.pallas_call(
        _rwkv_prep_kernel,
        grid=(batch, seq // tt),
        in_specs=[cur(w, c0), cur(w, c0 + 1), cur(w, c0 + 2), cur(LORA_PAD, lc),
                  prev(w, c0), prev(w, c0 + 1), prev(w, c0 + 2), prev(LORA_PAD, lc)]
                 + [full(a) for a in weights],
        out_specs=[out] * 8,
        out_shape=[jax.ShapeDtypeStruct((batch, seq, w), F32)] * 8,
        compiler_params=_params(("parallel", "arbitrary")),
        name="rwkv_prep",
    )(proj, proj, proj, proj, proj, proj, proj, proj, *weights)


def _scan_kernel(r_ref, lw_ref, k_ref, v_ref, kk_ref, b_ref, y_ref, s_ref, *, tb, pairs):
    c = SCAN_CHUNK
    c2 = 2 * c

    @pl.when(pl.program_id(1) == 0)
    def _():
        s_ref[...] = jnp.zeros_like(s_ref)

    head0 = lax.broadcasted_iota(jnp.int32, (c, LANES), 1) < RWKV_HEAD_DIM
    row = lax.broadcasted_iota(jnp.int32, (c2, c2), 0)
    col = lax.broadcasted_iota(jnp.int32, (c2, c2), 1)
    same_head = (row // c) == (col // c)
    strict = jnp.logical_and(same_head, col < row)
    incl = jnp.logical_and(same_head, col <= row)
    inv_blk = (row // SCAN_INV_BLOCK) == (col // SCAN_INV_BLOCK)
    eye = jnp.where(row == col, 1.0, 0.0).astype(F32)
    tri = jnp.where(lax.broadcasted_iota(jnp.int32, (c, c), 0)
                    >= lax.broadcasted_iota(jnp.int32, (c, c), 1), 1.0, 0.0).astype(BF16)

    def stack(x):
        return jnp.concatenate([jnp.where(head0, x, 0.0), jnp.where(head0, 0.0, x)], axis=0)

    def pair_chunk(t0, p):
        lanes = slice(p * LANES, (p + 1) * LANES)
        rows = pl.ds(t0, c)
        lw = lw_ref[0, rows, lanes]
        cum = _dot_exact_lhs(tri, lw)
        yield
        r = r_ref[0, rows, lanes]
        k = k_ref[0, rows, lanes]
        kk = kk_ref[0, rows, lanes]
        b = b_ref[0, rows, lanes]
        cum_end = cum[c - 1:c, :]
        e_neg = jnp.exp(-cum)
        e_end = jnp.exp(cum_end - cum)
        a_s = stack(-kk * jnp.exp(cum - lw))
        r_s = stack(r * jnp.exp(cum))
        b_s = stack(b * e_neg)
        k_s = stack(k * e_neg)
        bend_s = stack(b * e_end)
        kend_s = stack(k * e_end)
        v_s = stack(v_ref[0, rows, lanes])
        gram = _dot_nt(jnp.concatenate([a_s, r_s], axis=0),
                       jnp.concatenate([b_s, k_s], axis=0))
        kv = _dot_tn(v_s, kend_s)
        yield
        l_ab = jnp.where(strict, gram[:c2, :c2], 0.0)
        l_ak = jnp.where(strict, gram[:c2, c2:], 0.0)
        l_rb = jnp.where(incl, gram[c2:, :c2], 0.0)
        l_rk = jnp.where(incl, gram[c2:, c2:], 0.0)

        x = jnp.where(inv_blk, l_ab, 0.0)
        off = jnp.where(inv_blk, 0.0, l_ab)
        dinv = eye + x
        w1 = _dot(l_ak, v_s)
        y1 = _dot(l_rk, v_s)
        for _ in range(int(math.log2(SCAN_INV_BLOCK)) - 1):
            x = _dot(x, x)
            yield
            dinv = dinv + _dot(dinv, x)
            yield
        f = _dot(dinv, off)
        yield
        f2 = _dot(f, f)
        yield
        gm = eye + f
        gm = gm + _dot(gm, f2)
        yield
        t_inv = _dot(gm, dinv)
        yield
        ta = _dot(t_inv, jnp.concatenate([a_s, w1], axis=1))
        yield
        a_hat = ta[:, :LANES]
        v_hat = ta[:, LANES:]
        state = s_ref[p]
        m1 = _dot_nt(jnp.concatenate([a_hat, r_s], axis=0), state)
        yield
        u = m1[:c2] + v_hat
        y = m1[c2:] + y1 + _dot(l_rb, u)
        s_ref[p] = state * jnp.exp(cum_end) + kv + _dot_tn(u, bend_s)
        y_ref[0, rows, lanes] = y[:c] + y[c:]

    def chunk(ci, carry):
        t0 = pl.multiple_of(ci * c, c)
        for _ in itertools.zip_longest(*[pair_chunk(t0, p) for p in range(pairs)]):
            pass
        return carry

    lax.fori_loop(0, tb // c, chunk, 0)


def _rwkv_scan(r, lw, k, v, kk, b, tb):
    batch, seq, w = r.shape
    pairs = w // LANES
    spec = pl.BlockSpec((1, tb, w), lambda bi, t: (bi, t, 0))
    kern = lambda *refs: _scan_kernel(*refs, tb=tb, pairs=pairs)
    return pl.pallas_call(
        kern,
        grid=(batch, seq // tb),
        in_specs=[spec] * 6,
        out_specs=spec,
        out_shape=jax.ShapeDtypeStruct((batch, seq, w), F32),
        scratch_shapes=[pltpu.VMEM((pairs, LANES, LANES), F32)],
        compiler_params=_params(("parallel", "arbitrary")),
        name="rwkv_scan",
    )(r, lw, k, v, kk, b)


def _out_proj_kernel(x_ref, attn_ref, y_ref, bonus_ref, g_ref, lng, lnb, bd_ref, w_ref, n2g,
                     h_ref, hn_ref, hnt_ref):
    bd = bd_ref[...]
    y = y_ref[...]
    inv_n = 1.0 / RWKV_HEAD_DIM
    d = y - _dot(y, bd) * inv_n
    var = _dot(d * d, bd) * inv_n
    yn = d * lax.rsqrt(var + RWKV_GN_EPS) * lng[...] + lnb[...]
    rw = (yn + bonus_ref[...]) * g_ref[...]
    h = (x_ref[...] + _dot(attn_ref[...], w_ref[:ATTN_WIDTH, :])
         + _dot(rw, w_ref[ATTN_WIDTH:, :]))
    h_ref[...] = h
    inv = lax.rsqrt(jnp.mean(h * h, axis=-1, keepdims=True) + RMS_EPS)
    hn = h * inv * n2g[...]
    hn_ref[...] = hn.astype(BF16)
    hnt_ref[...] = hn.T.astype(BF16)


def _out_proj(x, attn, y, bonus, g, lng, lnb, bd, w_out, n2g, tm):
    m = x.shape[0]
    row = lambda width: pl.BlockSpec((tm, width), lambda i: (i, 0))
    full = lambda a: pl.BlockSpec(a.shape, lambda i: (0,) * a.ndim)
    return pl.pallas_call(
        _out_proj_kernel,
        grid=(m // tm,),
        in_specs=[row(D_MODEL), row(ATTN_WIDTH), row(RWKV_WIDTH), row(RWKV_WIDTH), row(RWKV_WIDTH),
                  full(lng), full(lnb), full(bd), full(w_out), full(n2g)],
        out_specs=[row(D_MODEL), row(D_MODEL), pl.BlockSpec((D_MODEL, tm), lambda i: (0, i))],
        out_shape=[jax.ShapeDtypeStruct((m, D_MODEL), F32),
                   jax.ShapeDtypeStruct((m, D_MODEL), BF16),
                   jax.ShapeDtypeStruct((D_MODEL, m), BF16)],
        compiler_params=_params(("parallel",)),
        name="out_proj",
    )(x, attn, y, bonus, g, lng, lnb, bd, w_out, n2g)


def _peer_q_kernel(hn_ref, wq_ref, keys_ref, st_ref):
    q = jnp.dot(hn_ref[...], wq_ref[...], preferred_element_type=F32)
    for hp in range(2 * PEER_HEADS):
        qs = q[:, hp * PEER_HALF:(hp + 1) * PEER_HALF]
        st_ref[hp] = _dot_nt(keys_ref[hp], qs)


def _peer_q(hn, wq, keys, tm):
    m = hn.shape[0]
    return pl.pallas_call(
        _peer_q_kernel,
        grid=(m // tm,),
        in_specs=[pl.BlockSpec((tm, D_MODEL), lambda i: (i, 0)),
                  pl.BlockSpec(wq.shape, lambda i: (0, 0)),
                  pl.BlockSpec(keys.shape, lambda i: (0, 0, 0))],
        out_specs=pl.BlockSpec((2 * PEER_HEADS, PEER_NKEYS, tm), lambda i: (0, 0, i)),
        out_shape=jax.ShapeDtypeStruct((2 * PEER_HEADS, PEER_NKEYS, m), F32),
        compiler_params=_params(("parallel",)),
        name="peer_q",
    )(hn, wq, keys)


def _top_desc(x, k, exact_ties):
    n = x.shape[0]
    rows = lax.broadcasted_iota(jnp.int32, x.shape, 0)
    rank = jnp.full(x.shape, float(k), F32)
    outs = []
    for i in range(k):
        m = jnp.max(x, axis=0, keepdims=True)
        outs.append(m)
        pick = x == m
        if exact_ties:
            pick = rows == jnp.min(jnp.where(pick, rows, n), axis=0, keepdims=True)
        x = jnp.where(pick, -jnp.inf, x)
        rank = jnp.where(pick, float(i), rank)
    gone = jnp.sum(jnp.where(x == -jnp.inf, 1.0, 0.0), axis=0, keepdims=True)
    return jnp.concatenate(outs, axis=0), rank, gone


def _peer_topk_kernel(s_ref, cnt1_ref, e1_ref, rank2_ref, e2_ref, *, n1):
    k = PEER_TOPK
    widths = [k // (a + 1) for a in range(k)]
    pad = -sum(widths) % 8

    def run(exact_ties):
        s1 = s_ref[0]
        s2 = s_ref[1]
        sv1, rank1, gone1 = _top_desc(s1, k, exact_ties)
        sv2, rank2, gone2 = _top_desc(s2, k, exact_ties)
        cand = jnp.concatenate([sv1[a:a + 1] + sv2[:widths[a]] for a in range(k)]
                               + [jnp.full((pad, s1.shape[1]), -jnp.inf, F32)], axis=0)
        best, pos, gone3 = _top_desc(cand, k, exact_ties)
        chosen = jnp.where(pos < float(k), 1.0, 0.0)
        z = jnp.sum(jnp.exp(best - best[0:1]), axis=0, keepdims=True)
        cnt1 = jnp.zeros(s1.shape, F32)
        row = 0
        for a in range(k):
            cnt_a = jnp.sum(chosen[row:row + widths[a]], axis=0, keepdims=True)
            cnt1 = jnp.where(rank1 == float(a), cnt_a, cnt1)
            row += widths[a]
        e1 = jnp.exp(s1 - sv1[0:1]) / z
        for t in range(PEER_NKEYS // n1):
            cnt1_ref[0, t] = cnt1[t * n1:(t + 1) * n1]
            e1_ref[0, t] = e1[t * n1:(t + 1) * n1]
        rank2_ref[0] = rank2
        e2_ref[0] = jnp.exp(s2 - sv2[0:1])
        return jnp.max(jnp.maximum(jnp.maximum(gone1, gone2), gone3 - float(pad)))

    most_gone = run(exact_ties=False)

    @pl.when(most_gone > float(k))
    def _():
        run(exact_ties=True)


def _peer_topk(st, tl, n1):
    m = st.shape[2]
    n_tiles = PEER_NKEYS // n1
    spec = pl.BlockSpec((1, PEER_NKEYS, tl), lambda h, i: (h, 0, i))
    shape = jax.ShapeDtypeStruct((PEER_HEADS, PEER_NKEYS, m), F32)
    tile_spec = pl.BlockSpec((1, n_tiles, n1, tl), lambda h, i: (h, 0, 0, i))
    tile_shape = jax.ShapeDtypeStruct((PEER_HEADS, n_tiles, n1, m), F32)
    kern = lambda *refs: _peer_topk_kernel(*refs, n1=n1)
    return pl.pallas_call(
        kern,
        grid=(PEER_HEADS, m // tl),
        in_specs=[pl.BlockSpec((2, PEER_NKEYS, tl), lambda h, i: (h, 0, i))],
        out_specs=[tile_spec, tile_spec, spec, spec],
        out_shape=[tile_shape, tile_shape, shape, shape],
        compiler_params=_params(("parallel", "parallel")),
        name="peer_topk",
    )(st)


def _peer_dense_kernel(u_ref, hnt_ref, vt_ref, rank2_ref, e2_ref, cnta_ref, cntb_ref,
                       e1a_ref, e1b_ref, h_ref, fg_ref, out_ref, act_ref, hpre_ref, yt_ref, *, n1):
    step = pl.program_id(1)
    last_step = pl.num_programs(1) - 1
    et = hpre_ref.shape[1]
    row_refs = ((cnta_ref, e1a_ref), (cntb_ref, e1b_ref))

    def gate_tile(half, jt, ib):
        prev = 1 - half
        cnt_ref, e1r_ref = row_refs[half]
        ls = slice(jt * LANES, (jt + 1) * LANES)
        rs = slice(ib * DENSE_ROWS, (ib + 1) * DENSE_ROWS)
        gates = [jnp.zeros((DENSE_ROWS, LANES), F32) for _ in range(n1)]
        for h in range(PEER_HEADS):
            r2 = rank2_ref[h, rs, ls]
            e2 = e2_ref[h, rs, ls]
            for a in range(n1):
                hit = r2 < cnt_ref[h, 0, a:a + 1, ls]
                gates[a] = gates[a] + jnp.where(hit, e2, 0.0) * e1r_ref[h, 0, a:a + 1, ls]
        for a in range(n1):
            ers = slice(a * PEER_NKEYS + ib * DENSE_ROWS, a * PEER_NKEYS + (ib + 1) * DENSE_ROWS)
            x = hpre_ref[prev, ers, ls]
            gelu = 0.5 * x * (1.0 + lax.erf(x * (0.5 ** 0.5)))
            act_ref[prev, ers, ls] = gelu * gates[a]

    tt = hpre_ref.shape[2]
    n_ib = PEER_NKEYS // DENSE_ROWS

    def run(first, last):
        for half in range(2):
            cur = half
            hs = slice(half * et, (half + 1) * et)
            gated = not (first and half == 0) and not (last and half == 1)
            for piece in range(tt // MXU_COLS):
                cs = slice(piece * MXU_COLS, (piece + 1) * MXU_COLS)
                tiles = [(jt, ib)
                         for jt in range(piece * MXU_COLS // LANES, (piece + 1) * MXU_COLS // LANES)
                         for ib in range(n_ib)]
                if not last:
                    hpre_ref[cur, :, cs] = jnp.dot(u_ref[hs, :], hnt_ref[:, cs],
                                                   preferred_element_type=F32)
                per_down = len(tiles) * DOWN_ROWS // D_MODEL
                for i, tile in enumerate(tiles):
                    if not first and i % per_down == per_down // 2:
                        ms = slice(i // per_down * DOWN_ROWS, (i // per_down + 1) * DOWN_ROWS)
                        yt_ref[ms, cs] += jnp.dot(vt_ref[ms, hs],
                                                  act_ref[cur, :, cs].astype(BF16),
                                                  preferred_element_type=F32)
                    if gated:
                        gate_tile(half, *tile)

    @pl.when(step == 0)
    def _():
        yt_ref[...] = jnp.zeros_like(yt_ref)
        run(first=True, last=False)

    @pl.when(jnp.logical_and(step > 0, step < last_step))
    def _():
        run(first=False, last=False)

    @pl.when(step == last_step)
    def _():
        run(first=False, last=True)
        h = h_ref[...] + yt_ref[...].T
        inv = lax.rsqrt(jnp.mean(h * h, axis=-1, keepdims=True) + RMS_EPS)
        out_ref[...] = h * inv * fg_ref[...]


def _peer_dense(u, hnt, vt, cntr, e1r, rank2, e2, h, final_g, tt, n1):
    m = hnt.shape[1]
    et = n1 * PEER_NKEYS
    n_steps = PEER_NKEYS // (2 * n1)
    n_tiles = 2 * n_steps
    once = pl.Buffered(buffer_count=1)
    tok3 = lambda a: pl.BlockSpec((a.shape[0], a.shape[1], tt), lambda i, e: (0, 0, i),
                                  pipeline_mode=once)
    rows = lambda half: pl.BlockSpec(
        (PEER_HEADS, 1, n1, tt),
        lambda i, e: (0, jnp.clip(2 * e + half - 1, 0, n_tiles - 1), 0, i))
    kern = lambda *refs: _peer_dense_kernel(*refs, n1=n1)
    return pl.pallas_call(
        kern,
        grid=(m // tt, n_steps + 1),
        in_specs=[pl.BlockSpec((2 * et, D_MODEL), lambda i, e: (jnp.minimum(e, n_steps - 1), 0)),
                  pl.BlockSpec((D_MODEL, tt), lambda i, e: (0, i)),
                  pl.BlockSpec((D_MODEL, 2 * et), lambda i, e: (0, jnp.maximum(e - 1, 0))),
                  tok3(rank2), tok3(e2), rows(0), rows(1), rows(0), rows(1),
                  pl.BlockSpec((tt, D_MODEL), lambda i, e: (i, 0), pipeline_mode=once),
                  pl.BlockSpec((1, D_MODEL), lambda i, e: (0, 0))],
        out_specs=pl.BlockSpec((tt, D_MODEL), lambda i, e: (i, 0)),
        out_shape=jax.ShapeDtypeStruct((m, D_MODEL), F32),
        scratch_shapes=[pltpu.VMEM((2, et, tt), F32), pltpu.VMEM((2, et, tt), F32),
                        pltpu.VMEM((D_MODEL, tt), F32)],
        compiler_params=_params(("parallel", "arbitrary")),
        name="peer_dense",
    )(u, hnt, vt, rank2, e2, cntr, cntr, e1r, e1r, h, final_g)


def _transpose_cast_kernel(x_ref, o_ref):
    o_ref[...] = x_ref[...].T.astype(o_ref.dtype)


def _transpose_cast(x, tr):
    rows, cols = x.shape
    return pl.pallas_call(
        _transpose_cast_kernel,
        grid=(rows // tr,),
        in_specs=[pl.BlockSpec((tr, cols), lambda i: (i, 0))],
        out_specs=pl.BlockSpec((cols, tr), lambda i: (0, i)),
        out_shape=jax.ShapeDtypeStruct((cols, rows), BF16),
        compiler_params=_params(("parallel",)),
        name="transpose_cast",
    )(x)


def _row(a):
    return a.reshape(1, -1).astype(F32)


def _pad_rows(w, start, total):
    return jnp.zeros((total, w.shape[1]), BF16).at[start:start + w.shape[0]].set(w.astype(BF16))


def kernel(x, norm1_g, w_in, rwkv_mu, rwkv_w0, rwkv_w2, rwkv_a0, rwkv_a2, rwkv_g2, rwkv_k_k, rwkv_k_a, rwkv_r_k, rwkv_ln_g, rwkv_ln_b, w_out, norm2_g, peer_wq, peer_sub_keys, peer_u, peer_v, final_g):
    batch, seq, d = x.shape
    tokens = batch * seq
    w = RWKV_WIDTH
    xt = x.reshape(tokens, d)

    w_in_p = jnp.pad(w_in[0].astype(BF16), ((0, 0), (0, IN_COLS_PAD - IN_COLS)))
    proj = _norm_matmul(xt, _row(norm1_g[0]), w_in_p, tm=1024, tn=512)
    proj = proj.reshape(batch, seq, IN_COLS_PAD)

    attn = _moba(proj, batch, seq, heads=4)

    mu = rwkv_mu[0]
    mu_l = jnp.pad(mu[3 * w:], (0, LORA_PAD - LORA_COLS))
    hid = lax.broadcasted_iota(jnp.int32, (w, w), 0) // RWKV_HEAD_DIM
    bd = (hid == hid.T).astype(BF16)
    prep_w = [_row(mu[:w]), _row(mu[w:2 * w]), _row(mu[2 * w:3 * w]), _row(mu_l),
              _row(rwkv_w0[0]), _pad_rows(rwkv_w2[0], 0, LORA_PAD),
              _row(rwkv_a0[0]), _pad_rows(rwkv_a2[0], DECAY_LORA, LORA_PAD),
              _pad_rows(rwkv_g2[0], DECAY_LORA + AAA_LORA, LORA_PAD),
              _row(rwkv_k_k[0]), _row(rwkv_k_a[0]), _row(rwkv_r_k[0]), bd]
    r, lw, k2, v, kk, b, g, bonus = _rwkv_prep(proj, batch, seq, prep_w, tt=256)
    y = _rwkv_scan(r, lw, k2, v, kk, b, tb=256)

    flat = lambda a: a.reshape(tokens, a.shape[-1])
    h, hn, hnt = _out_proj(xt, flat(attn), flat(y), flat(bonus), flat(g), _row(rwkv_ln_g[0]),
                      _row(rwkv_ln_b[0]), bd, w_out[0].astype(BF16), _row(norm2_g[0]), tm=256)

    keys = peer_sub_keys[0].reshape(2 * PEER_HEADS, PEER_NKEYS, PEER_HALF).astype(BF16)
    st = _peer_q(hn, peer_wq[0].astype(BF16), keys, tm=512)
    cnt1, e1, rank2, e2 = _peer_topk(st, tl=512, n1=4)
    out = _peer_dense(peer_u[0].astype(BF16), hnt, _transpose_cast(peer_v[0], tr=512),
                      cnt1, e1, rank2, e2, h, final_g.reshape(1, d), tt=512, n1=4)
    return out.reshape(batch, seq, d)
```

```python
import itertools
import math

import jax
import jax.numpy as jnp
from jax import lax
from jax.experimental import pallas as pl
from jax.experimental.pallas import tpu as pltpu

F32 = jnp.float32
BF16 = jnp.bfloat16

D_MODEL = 2048
ATTN_HEADS = 8
ATTN_HEAD_DIM = 128
ATTN_WIDTH = ATTN_HEADS * ATTN_HEAD_DIM
MOBA_BLOCK = 256
MOBA_TOPK = 3

RWKV_HEAD_DIM = 64
RWKV_WIDTH = D_MODEL - ATTN_WIDTH
RWKV_HEADS = RWKV_WIDTH // RWKV_HEAD_DIM
DECAY_LORA = 96
AAA_LORA = 96
GATE_LORA = 256
LORA_COLS = DECAY_LORA + AAA_LORA + GATE_LORA
LORA_PAD = 512
RWKV_GN_EPS = RWKV_HEAD_DIM * 1e-5
IN_COLS = 3 * ATTN_WIDTH + 3 * RWKV_WIDTH + LORA_COLS
IN_COLS_PAD = 3 * ATTN_WIDTH + 3 * RWKV_WIDTH + LORA_PAD

PEER_HEADS = 8
PEER_NKEYS = 128
PEER_HALF = 128
PEER_TOPK = 16

RMS_EPS = 1e-6
NEG = -1e30

LANES = 128
SCAN_CHUNK = 64
SCAN_INV_BLOCK = 16
DOWN_ROWS = 2048
MXU_COLS = 256
DENSE_ROWS = 32
VMEM_LIMIT = 56 * 1024 * 1024
DENSE_VMEM_LIMIT = 58 * 1024 * 1024

_NT = (((1,), (1,)), ((), ()))
_TN = (((0,), (0,)), ((), ()))


def _params(sem, flags=None):
    return pltpu.CompilerParams(dimension_semantics=sem, vmem_limit_bytes=VMEM_LIMIT, flags=flags)


def _dot(a, b):
    return jnp.dot(a.astype(BF16), b.astype(BF16), preferred_element_type=F32)


def _dot_nt(a, b):
    return lax.dot_general(a.astype(BF16), b.astype(BF16), _NT, preferred_element_type=F32)


def _dot_tn(a, b):
    return lax.dot_general(a.astype(BF16), b.astype(BF16), _TN, preferred_element_type=F32)


def _split3(x):
    hi = x.astype(BF16)
    r1 = x - hi.astype(F32)
    mid = r1.astype(BF16)
    lo = (r1 - mid.astype(F32)).astype(BF16)
    return hi, mid, lo


def _dot_exact_rhs(x, ones_bf16):
    hi = x.astype(BF16)
    lo = (x - hi.astype(F32)).astype(BF16)
    d = lambda p: jnp.dot(p, ones_bf16, preferred_element_type=F32)
    return d(hi) + d(lo)


def _dot_exact_lhs(ones_bf16, x):
    hi, mid, lo = _split3(x)
    d = lambda p: jnp.dot(ones_bf16, p, preferred_element_type=F32)
    return d(hi) + d(mid) + d(lo)


def _norm_matmul_kernel(x_ref, g_ref, w_ref, o_ref, xn_ref):
    @pl.when(pl.program_id(1) == 0)
    def _():
        x = x_ref[...]
        inv = lax.rsqrt(jnp.mean(x * x, axis=-1, keepdims=True) + RMS_EPS)
        xn_ref[...] = (x * inv * g_ref[...]).astype(BF16)

    o_ref[...] = jnp.dot(xn_ref[...], w_ref[...], preferred_element_type=F32)


def _norm_matmul(x, g, w, tm, tn):
    m, k = x.shape
    n = w.shape[1]
    return pl.pallas_call(
        _norm_matmul_kernel,
        grid=(m // tm, n // tn),
        in_specs=[pl.BlockSpec((tm, k), lambda i, j: (i, 0)),
                  pl.BlockSpec((1, k), lambda i, j: (0, 0)),
                  pl.BlockSpec((k, tn), lambda i, j: (0, j))],
        out_specs=pl.BlockSpec((tm, tn), lambda i, j: (i, j)),
        out_shape=jax.ShapeDtypeStruct((m, n), F32),
        scratch_shapes=[pltpu.VMEM((tm, k), BF16)],
        compiler_params=_params(("parallel", "arbitrary")),
        name="in_proj",
    )(x, g, w)


def _moba_kernel(q_ref, k_ref, v_ref, o_ref, kmean_ref, kb_ref, vt_ref, *, nb, heads):
    blk_len = MOBA_BLOCK
    hd = ATTN_HEAD_DIM
    hg = pl.program_id(1)
    qi = pl.program_id(2)
    hrange = range(heads)

    @pl.when(qi == 0)
    def _():
        for hh in hrange:
            hl = slice(hh * hd, (hh + 1) * hd)
            for j in range(nb):
                rows = slice(j * blk_len, (j + 1) * blk_len)
                kj = k_ref[0, rows, hl]
                kmean_ref[hh, j:j + 1, :] = jnp.mean(kj, axis=0, keepdims=True)
                kb_ref[hh, rows, :] = kj.astype(BF16)
                vt_ref[hh, :, rows] = v_ref[0, rows, hl].T.astype(BF16)

    log2e = math.log2(math.e)
    scale2 = (hd ** -0.5) * log2e
    kq = (lax.broadcasted_iota(jnp.int32, (blk_len, blk_len), 1)
          - lax.broadcasted_iota(jnp.int32, (blk_len, blk_len), 0))
    kqf = kq.astype(F32)
    blk = lax.broadcasted_iota(jnp.int32, (nb, blk_len), 0)
    past = blk < qi

    def select(hh):
        q = q_ref[0, :, hh * hd:(hh + 1) * hd]
        gate = lax.dot_general(kmean_ref[hh], q, _NT, precision=lax.Precision.HIGHEST,
                               preferred_element_type=F32)
        g = jnp.where(past, gate, NEG)
        selw = jnp.zeros(gate.shape, F32)
        for _ in range(min(MOBA_TOPK, nb)):
            m = jnp.max(g, axis=0, keepdims=True)
            first = jnp.min(jnp.where(g == m, blk, nb), axis=0, keepdims=True)
            pick = blk == first
            selw = jnp.where(pick, 1.0, selw)
            g = jnp.where(pick, -jnp.inf, g)
        return jnp.where(past, selw, 0.0), (q * scale2).astype(BF16)

    sel_q = [select(hh) for hh in hrange]
    selw = [x[0] for x in sel_q]
    qb = [x[1] for x in sel_q]
    slope2 = [jnp.exp2(jnp.zeros((1, 1), F32) - (hg * heads + hh + 1).astype(F32)) * log2e
              for hh in hrange]
    bias0 = [slope2[hh] * kqf for hh in hrange]

    def scores(j):
        r0 = pl.multiple_of(j * blk_len, blk_len)
        return [lax.dot_general(kb_ref[hh, pl.ds(r0, blk_len), :], qb[hh], _NT,
                                preferred_element_type=F32) - bias0[hh] for hh in hrange]

    def weighted_values(j, p):
        r0 = pl.multiple_of(j * blk_len, blk_len)
        return [jnp.dot(vt_ref[hh, :, pl.ds(r0, blk_len)], p[hh].astype(BF16),
                        preferred_element_type=F32) for hh in hrange]

    s = [jnp.where(kq >= 0, x, NEG) for x in scores(qi)]
    m = [jnp.max(x, axis=0, keepdims=True) for x in s]
    p = [jnp.exp2(s[hh] - m[hh]) for hh in hrange]
    l = [jnp.sum(x, axis=0, keepdims=True) for x in p]
    acc = weighted_values(qi, p)

    def body(j, carry):
        m, l, acc = carry
        off = ((qi - j) * blk_len).astype(F32)
        s = scores(j)
        selj = [jnp.max(jnp.where(blk == j, selw[hh], 0.0), axis=0, keepdims=True) > 0.0
                for hh in hrange]
        shift = [slope2[hh] * off for hh in hrange]
        top = [jnp.max(s[hh], axis=0, keepdims=True) - shift[hh] for hh in hrange]
        m_new = [jnp.maximum(m[hh], jnp.where(selj[hh], top[hh], NEG)) for hh in hrange]
        alpha = [jnp.exp2(m[hh] - m_new[hh]) for hh in hrange]
        p = [jnp.exp2(s[hh] - jnp.where(selj[hh], m_new[hh] + shift[hh], jnp.inf)) for hh in hrange]
        l = [alpha[hh] * l[hh] + jnp.sum(p[hh], axis=0, keepdims=True) for hh in hrange]
        pv = weighted_values(j, p)
        acc = [alpha[hh] * acc[hh] + pv[hh] for hh in hrange]
        return m_new, l, acc

    m, l, acc = lax.fori_loop(0, qi, body, (m, l, acc))
    for hh in hrange:
        o_ref[0, :, hh * hd:(hh + 1) * hd] = (acc[hh] / l[hh]).T


def _moba(proj, batch, seq, heads):
    nb = seq // MOBA_BLOCK
    hd = ATTN_HEAD_DIM
    hw = heads * hd
    groups = ATTN_HEADS // heads
    kern = lambda *refs: _moba_kernel(*refs, nb=nb, heads=heads)
    return pl.pallas_call(
        kern,
        grid=(batch, groups, nb),
        in_specs=[pl.BlockSpec((1, MOBA_BLOCK, hw), lambda b, h, i: (b, i, h)),
                  pl.BlockSpec((1, seq, hw), lambda b, h, i: (b, 0, groups + h)),
                  pl.BlockSpec((1, seq, hw), lambda b, h, i: (b, 0, 2 * groups + h))],
        out_specs=pl.BlockSpec((1, MOBA_BLOCK, hw), lambda b, h, i: (b, i, h)),
        out_shape=jax.ShapeDtypeStruct((batch, seq, ATTN_WIDTH), F32),
        scratch_shapes=[pltpu.VMEM((heads, nb, hd), F32),
                        pltpu.VMEM((heads, seq, hd), BF16),
                        pltpu.VMEM((heads, hd, seq), BF16)],
        compiler_params=_params(("parallel", "parallel", "arbitrary")),
        name="moba",
    )(proj, proj, proj)


def _rwkv_prep_kernel(r_ref, k_ref, v_ref, l_ref, rp_ref, kp_ref, vp_ref, lp_ref,
                      mu_r, mu_k, mu_v, mu_l, w0, w2p, a0, a2p, g2p, kk_w, ka_w, rk_w, bd_ref,
                      r_o, lw_o, k_o, v_o, kk_o, b_o, g_o, bonus_o):
    first = pl.program_id(1) == 0

    def shift(cur_ref, prev_ref, mu_ref):
        cur = cur_ref[0]
        prev_last = jnp.where(first, 0.0, prev_ref[0, 7:8, :])
        rows = lax.broadcasted_iota(jnp.int32, cur.shape, 0)
        prev = jnp.where(rows == 0, prev_last, pltpu.roll(cur, 1, axis=0))
        return cur + (prev - cur) * mu_ref[...]

    r = shift(r_ref, rp_ref, mu_r)
    k = shift(k_ref, kp_ref, mu_k)
    v = shift(v_ref, vp_ref, mu_v)
    lo = shift(l_ref, lp_ref, mu_l)

    bd = bd_ref[...]
    z = -(w0[...] + _dot(jnp.tanh(lo), w2p[...]))
    softplus = jnp.maximum(z, 0.0) + jnp.log1p(jnp.exp(-jnp.abs(z)))
    lw = -jnp.exp(-softplus - 0.5)
    a = jax.nn.sigmoid(a0[...] + _dot(lo, a2p[...]))
    g = _dot(jax.nn.sigmoid(lo), g2p[...])
    kk = k * kk_w[...]
    kk = kk / jnp.maximum(jnp.sqrt(_dot_exact_rhs(kk * kk, bd)), 1e-12)
    k2 = k * (1.0 + (a - 1.0) * ka_w[...])
    r_o[0] = r
    lw_o[0] = lw
    k_o[0] = k2
    v_o[0] = v
    kk_o[0] = kk
    b_o[0] = kk * a
    g_o[0] = g
    bonus_o[0] = _dot_exact_rhs(r * k2 * rk_w[...], bd) * v


def _rwkv_prep(proj, batch, seq, weights, tt):
    w = RWKV_WIDTH
    c0 = 3 * ATTN_WIDTH // w
    lc = (3 * ATTN_WIDTH + 3 * w) // LORA_PAD
    pb = tt // 8
    cur = lambda width, col: pl.BlockSpec((1, tt, width), lambda b, i: (b, i, col))
    prev = lambda width, col: pl.BlockSpec(
        (1, 8, width), lambda b, i: (b, jnp.maximum(i * pb - 1, 0), col))
    full = lambda a: pl.BlockSpec(a.shape, lambda b, i: (0,) * a.ndim)
    out = pl.BlockSpec((1, tt, w), lambda b, i: (b, i, 0))
    return pl.pallas_call(
        _rwkv_prep_kernel,
        grid=(batch, seq // tt),
        in_specs=[cur(w, c0), cur(w, c0 + 1), cur(w, c0 + 2), cur(LORA_PAD, lc),
                  prev(w, c0), prev(w, c0 + 1), prev(w, c0 + 2), prev(LORA_PAD, lc)]
                 + [full(a) for a in weights],
        out_specs=[out] * 8,
        out_shape=[jax.ShapeDtypeStruct((batch, seq, w), F32)] * 8,
        compiler_params=_params(("parallel", "arbitrary")),
        name="rwkv_prep",
    )(proj, proj, proj, proj, proj, proj, proj, proj, *weights)


def _scan_kernel(r_ref, lw_ref, k_ref, v_ref, kk_ref, b_ref, y_ref, s_ref, *, tb, pairs):
    c = SCAN_CHUNK
    c2 = 2 * c

    @pl.when(pl.program_id(1) == 0)
    def _():
        s_ref[...] = jnp.zeros_like(s_ref)

    head0 = lax.broadcasted_iota(jnp.int32, (c, LANES), 1) < RWKV_HEAD_DIM
    row = lax.broadcasted_iota(jnp.int32, (c2, c2), 0)
    col = lax.broadcasted_iota(jnp.int32, (c2, c2), 1)
    same_head = (row // c) == (col // c)
    strict = jnp.logical_and(same_head, col < row)
    incl = jnp.logical_and(same_head, col <= row)
    inv_blk = (row // SCAN_INV_BLOCK) == (col // SCAN_INV_BLOCK)
    eye = jnp.where(row == col, 1.0, 0.0).astype(F32)
    tri = jnp.where(lax.broadcasted_iota(jnp.int32, (c, c), 0)
                    >= lax.broadcasted_iota(jnp.int32, (c, c), 1), 1.0, 0.0).astype(BF16)

    def stack(x):
        return jnp.concatenate([jnp.where(head0, x, 0.0), jnp.where(head0, 0.0, x)], axis=0)

    def pair_chunk(t0, p):
        lanes = slice(p * LANES, (p + 1) * LANES)
        rows = pl.ds(t0, c)
        lw = lw_ref[0, rows, lanes]
        cum = _dot_exact_lhs(tri, lw)
        yield
        r = r_ref[0, rows, lanes]
        k = k_ref[0, rows, lanes]
        kk = kk_ref[0, rows, lanes]
        b = b_ref[0, rows, lanes]
        cum_end = cum[c - 1:c, :]
        e_neg = jnp.exp(-cum)
        e_end = jnp.exp(cum_end - cum)
        a_s = stack(-kk * jnp.exp(cum - lw))
        r_s = stack(r * jnp.exp(cum))
        b_s = stack(b * e_neg)
        k_s = stack(k * e_neg)
        bend_s = stack(b * e_end)
        kend_s = stack(k * e_end)
        v_s = stack(v_ref[0, rows, lanes])
        gram = _dot_nt(jnp.concatenate([a_s, r_s], axis=0),
                       jnp.concatenate([b_s, k_s], axis=0))
        kv = _dot_tn(v_s, kend_s)
        yield
        l_ab = jnp.where(strict, gram[:c2, :c2], 0.0)
        l_ak = jnp.where(strict, gram[:c2, c2:], 0.0)
        l_rb = jnp.where(incl, gram[c2:, :c2], 0.0)
        l_rk = jnp.where(incl, gram[c2:, c2:], 0.0)

        x = jnp.where(inv_blk, l_ab, 0.0)
        off = jnp.where(inv_blk, 0.0, l_ab)
        dinv = eye + x
        w1 = _dot(l_ak, v_s)
        y1 = _dot(l_rk, v_s)
        for _ in range(int(math.log2(SCAN_INV_BLOCK)) - 1):
            x = _dot(x, x)
            yield
            dinv = dinv + _dot(dinv, x)
            yield
        f = _dot(dinv, off)
        yield
        f2 = _dot(f, f)
        yield
        gm = eye + f
        gm = gm + _dot(gm, f2)
        yield
        t_inv = _dot(gm, dinv)
        yield
        ta = _dot(t_inv, jnp.concatenate([a_s, w1], axis=1))
        yield
        a_hat = ta[:, :LANES]
        v_hat = ta[:, LANES:]
        state = s_ref[p]
        m1 = _dot_nt(jnp.concatenate([a_hat, r_s], axis=0), state)
        yield
        u = m1[:c2] + v_hat
        y = m1[c2:] + y1 + _dot(l_rb, u)
        s_ref[p] = state * jnp.exp(cum_end) + kv + _dot_tn(u, bend_s)
        y_ref[0, rows, lanes] = y[:c] + y[c:]

    def chunk(ci, carry):
        t0 = pl.multiple_of(ci * c, c)
        for _ in itertools.zip_longest(*[pair_chunk(t0, p) for p in range(pairs)]):
            pass
        return carry

    lax.fori_loop(0, tb // c, chunk, 0)


def _rwkv_scan(r, lw, k, v, kk, b, tb):
    batch, seq, w = r.shape
    pairs = w // LANES
    spec = pl.BlockSpec((1, tb, w), lambda bi, t: (bi, t, 0))
    kern = lambda *refs: _scan_kernel(*refs, tb=tb, pairs=pairs)
    return pl.pallas_call(
        kern,
        grid=(batch, seq // tb),
        in_specs=[spec] * 6,
        out_specs=spec,
        out_shape=jax.ShapeDtypeStruct((batch, seq, w), F32),
        scratch_shapes=[pltpu.VMEM((pairs, LANES, LANES), F32)],
        compiler_params=_params(("parallel", "arbitrary")),
        name="rwkv_scan",
    )(r, lw, k, v, kk, b)


def _out_proj_kernel(x_ref, attn_ref, y_ref, bonus_ref, g_ref, lng, lnb, bd_ref, w_ref, n2g,
                     h_ref, hn_ref, hnt_ref):
    bd = bd_ref[...]
    y = y_ref[...]
    inv_n = 1.0 / RWKV_HEAD_DIM
    d = y - _dot(y, bd) * inv_n
    var = _dot(d * d, bd) * inv_n
    yn = d * lax.rsqrt(var + RWKV_GN_EPS) * lng[...] + lnb[...]
    rw = (yn + bonus_ref[...]) * g_ref[...]
    h = (x_ref[...] + _dot(attn_ref[...], w_ref[:ATTN_WIDTH, :])
         + _dot(rw, w_ref[ATTN_WIDTH:, :]))
    h_ref[...] = h
    inv = lax.rsqrt(jnp.mean(h * h, axis=-1, keepdims=True) + RMS_EPS)
    hn = h * inv * n2g[...]
    hn_ref[...] = hn.astype(BF16)
    hnt_ref[...] = hn.T.astype(BF16)


def _out_proj(x, attn, y, bonus, g, lng, lnb, bd, w_out, n2g, tm):
    m = x.shape[0]
    row = lambda width: pl.BlockSpec((tm, width), lambda i: (i, 0))
    full = lambda a: pl.BlockSpec(a.shape, lambda i: (0,) * a.ndim)
    return pl.pallas_call(
        _out_proj_kernel,
        grid=(m // tm,),
        in_specs=[row(D_MODEL), row(ATTN_WIDTH), row(RWKV_WIDTH), row(RWKV_WIDTH), row(RWKV_WIDTH),
                  full(lng), full(lnb), full(bd), full(w_out), full(n2g)],
        out_specs=[row(D_MODEL), row(D_MODEL), pl.BlockSpec((D_MODEL, tm), lambda i: (0, i))],
        out_shape=[jax.ShapeDtypeStruct((m, D_MODEL), F32),
                   jax.ShapeDtypeStruct((m, D_MODEL), BF16),
                   jax.ShapeDtypeStruct((D_MODEL, m), BF16)],
        compiler_params=_params(("parallel",)),
        name="out_proj",
    )(x, attn, y, bonus, g, lng, lnb, bd, w_out, n2g)


def _peer_q_kernel(hn_ref, wq_ref, keys_ref, st_ref):
    q = jnp.dot(hn_ref[...], wq_ref[...], preferred_element_type=F32)
    for hp in range(2 * PEER_HEADS):
        qs = q[:, hp * PEER_HALF:(hp + 1) * PEER_HALF]
        st_ref[hp] = _dot_nt(keys_ref[hp], qs)


def _peer_q(hn, wq, keys, tm):
    m = hn.shape[0]
    return pl.pallas_call(
        _peer_q_kernel,
        grid=(m // tm,),
        in_specs=[pl.BlockSpec((tm, D_MODEL), lambda i: (i, 0)),
                  pl.BlockSpec(wq.shape, lambda i: (0, 0)),
                  pl.BlockSpec(keys.shape, lambda i: (0, 0, 0))],
        out_specs=pl.BlockSpec((2 * PEER_HEADS, PEER_NKEYS, tm), lambda i: (0, 0, i)),
        out_shape=jax.ShapeDtypeStruct((2 * PEER_HEADS, PEER_NKEYS, m), F32),
        compiler_params=_params(("parallel",)),
        name="peer_q",
    )(hn, wq, keys)


def _top_desc(x, k, exact_ties):
    n = x.shape[0]
    rows = lax.broadcasted_iota(jnp.int32, x.shape, 0)
    rank = jnp.full(x.shape, float(k), F32)
    outs = []
    for i in range(k):
        m = jnp.max(x, axis=0, keepdims=True)
        outs.append(m)
        pick = x == m
        if exact_ties:
            pick = rows == jnp.min(jnp.where(pick, rows, n), axis=0, keepdims=True)
        x = jnp.where(pick, -jnp.inf, x)
        rank = jnp.where(pick, float(i), rank)
    gone = jnp.sum(jnp.where(x == -jnp.inf, 1.0, 0.0), axis=0, keepdims=True)
    return jnp.concatenate(outs, axis=0), rank, gone


def _peer_topk_kernel(s_ref, cnt1_ref, e1_ref, rank2_ref, e2_ref, *, n1):
    k = PEER_TOPK
    widths = [k // (a + 1) for a in range(k)]
    pad = -sum(widths) % 8

    def run(exact_ties):
        s1 = s_ref[0]
        s2 = s_ref[1]
        sv1, rank1, gone1 = _top_desc(s1, k, exact_ties)
        sv2, rank2, gone2 = _top_desc(s2, k, exact_ties)
        cand = jnp.concatenate([sv1[a:a + 1] + sv2[:widths[a]] for a in range(k)]
                               + [jnp.full((pad, s1.shape[1]), -jnp.inf, F32)], axis=0)
        best, pos, gone3 = _top_desc(cand, k, exact_ties)
        chosen = jnp.where(pos < float(k), 1.0, 0.0)
        z = jnp.sum(jnp.exp(best - best[0:1]), axis=0, keepdims=True)
        cnt1 = jnp.zeros(s1.shape, F32)
        row = 0
        for a in range(k):
            cnt_a = jnp.sum(chosen[row:row + widths[a]], axis=0, keepdims=True)
            cnt1 = jnp.where(rank1 == float(a), cnt_a, cnt1)
            row += widths[a]
        e1 = jnp.exp(s1 - sv1[0:1]) / z
        for t in range(PEER_NKEYS // n1):
            cnt1_ref[0, t] = cnt1[t * n1:(t + 1) * n1]
            e1_ref[0, t] = e1[t * n1:(t + 1) * n1]
        rank2_ref[0] = rank2
        e2_ref[0] = jnp.exp(s2 - sv2[0:1])
        return jnp.max(jnp.maximum(jnp.maximum(gone1, gone2), gone3 - float(pad)))

    most_gone = run(exact_ties=False)

    @pl.when(most_gone > float(k))
    def _():
        run(exact_ties=True)


def _peer_topk(st, tl, n1):
    m = st.shape[2]
    n_tiles = PEER_NKEYS // n1
    spec = pl.BlockSpec((1, PEER_NKEYS, tl), lambda h, i: (h, 0, i))
    shape = jax.ShapeDtypeStruct((PEER_HEADS, PEER_NKEYS, m), F32)
    tile_spec = pl.BlockSpec((1, n_tiles, n1, tl), lambda h, i: (h, 0, 0, i))
    tile_shape = jax.ShapeDtypeStruct((PEER_HEADS, n_tiles, n1, m), F32)
    kern = lambda *refs: _peer_topk_kernel(*refs, n1=n1)
    return pl.pallas_call(
        kern,
        grid=(PEER_HEADS, m // tl),
        in_specs=[pl.BlockSpec((2, PEER_NKEYS, tl), lambda h, i: (h, 0, i))],
        out_specs=[tile_spec, tile_spec, spec, spec],
        out_shape=[tile_shape, tile_shape, shape, shape],
        compiler_params=_params(("parallel", "parallel")),
        name="peer_topk",
    )(st)


def _peer_dense_kernel(u_ref, hnt_ref, vt_ref, rank2_ref, e2_ref, cnta_ref, cntb_ref,
                       e1a_ref, e1b_ref, h_ref, fg_ref, out_ref, act_ref, hpre_ref, yt_ref, *, n1):
    step = pl.program_id(1)
    last_step = pl.num_programs(1) - 1
    et = hpre_ref.shape[1]
    row_refs = ((cnta_ref, e1a_ref), (cntb_ref, e1b_ref))

    def gate_tile(half, jt, ib):
        prev = 1 - half
        cnt_ref, e1r_ref = row_refs[half]
        ls = slice(jt * LANES, (jt + 1) * LANES)
        rs = slice(ib * DENSE_ROWS, (ib + 1) * DENSE_ROWS)
        gates = [jnp.zeros((DENSE_ROWS, LANES), F32) for _ in range(n1)]
        for h in range(PEER_HEADS):
            r2 = rank2_ref[h, rs, ls]
            e2 = e2_ref[h, rs, ls]
            for a in range(n1):
                hit = r2 < cnt_ref[h, 0, a:a + 1, ls]
                gates[a] = gates[a] + jnp.where(hit, e2, 0.0) * e1r_ref[h, 0, a:a + 1, ls]
        for a in range(n1):
            ers = slice(a * PEER_NKEYS + ib * DENSE_ROWS, a * PEER_NKEYS + (ib + 1) * DENSE_ROWS)
            x = hpre_ref[prev, ers, ls]
            gelu = 0.5 * x * (1.0 + lax.erf(x * (0.5 ** 0.5)))
            act_ref[prev, ers, ls] = gelu * gates[a]

    tt = hpre_ref.shape[2]
    n_ib = PEER_NKEYS // DENSE_ROWS

    def run(first, last):
        for half in range(2):
            cur = half
            hs = slice(half * et, (half + 1) * et)
            gated = not (first and half == 0) and not (last and half == 1)
            for piece in range(tt // MXU_COLS):
                cs = slice(piece * MXU_COLS, (piece + 1) * MXU_COLS)
                tiles = [(jt, ib)
                         for jt in range(piece * MXU_COLS // LANES, (piece + 1) * MXU_COLS // LANES)
                         for ib in range(n_ib)]
                if not last:
                    hpre_ref[cur, :, cs] = jnp.dot(u_ref[hs, :], hnt_ref[:, cs],
                                                   preferred_element_type=F32)
                per_down = len(tiles) * DOWN_ROWS // D_MODEL
                for i, tile in enumerate(tiles):
                    if not first and i % per_down == per_down // 2:
                        ms = slice(i // per_down * DOWN_ROWS, (i // per_down + 1) * DOWN_ROWS)
                        yt_ref[ms, cs] += jnp.dot(vt_ref[ms, hs],
                                                  act_ref[cur, :, cs].astype(BF16),
                                                  preferred_element_type=F32)
                    if gated:
                        gate_tile(half, *tile)

    @pl.when(step == 0)
    def _():
        yt_ref[...] = jnp.zeros_like(yt_ref)
        run(first=True, last=False)

    @pl.when(jnp.logical_and(step > 0, step < last_step))
    def _():
        run(first=False, last=False)

    @pl.when(step == last_step)
    def _():
        run(first=False, last=True)
        h = h_ref[...] + yt_ref[...].T
        inv = lax.rsqrt(jnp.mean(h * h, axis=-1, keepdims=True) + RMS_EPS)
        out_ref[...] = h * inv * fg_ref[...]


def _peer_dense(u, hnt, vt, cntr, e1r, rank2, e2, h, final_g, tt, n1):
    m = hnt.shape[1]
    et = n1 * PEER_NKEYS
    n_steps = PEER_NKEYS // (2 * n1)
    n_tiles = 2 * n_steps
    tok3 = lambda a: pl.BlockSpec((a.shape[0], a.shape[1], tt), lambda i, e: (0, 0, i))
    rows = lambda half: pl.BlockSpec(
        (PEER_HEADS, 1, n1, tt),
        lambda i, e: (0, jnp.clip(2 * e + half - 1, 0, n_tiles - 1), 0, i))
    kern = lambda *refs: _peer_dense_kernel(*refs, n1=n1)
    return pl.pallas_call(
        kern,
        grid=(m // tt, n_steps + 1),
        in_specs=[pl.BlockSpec((2 * et, D_MODEL), lambda i, e: (jnp.minimum(e, n_steps - 1), 0)),
                  pl.BlockSpec((D_MODEL, tt), lambda i, e: (0, i)),
                  pl.BlockSpec((D_MODEL, 2 * et), lambda i, e: (0, jnp.maximum(e - 1, 0))),
                  tok3(rank2), tok3(e2), rows(0), rows(1), rows(0), rows(1),
                  pl.BlockSpec((tt, D_MODEL), lambda i, e: (i, 0)),
                  pl.BlockSpec((1, D_MODEL), lambda i, e: (0, 0))],
        out_specs=pl.BlockSpec((tt, D_MODEL), lambda i, e: (i, 0)),
        out_shape=jax.ShapeDtypeStruct((m, D_MODEL), F32),
        scratch_shapes=[pltpu.VMEM((2, et, tt), F32), pltpu.VMEM((2, et, tt), F32),
                        pltpu.VMEM((D_MODEL, tt), F32)],
        compiler_params=pltpu.CompilerParams(dimension_semantics=("parallel", "arbitrary"),
                                             vmem_limit_bytes=DENSE_VMEM_LIMIT),
        name="peer_dense",
    )(u, hnt, vt, rank2, e2, cntr, cntr, e1r, e1r, h, final_g)


def _transpose_cast_kernel(x_ref, o_ref):
    o_ref[...] = x_ref[...].T.astype(o_ref.dtype)


def _transpose_cast(x, tr):
    rows, cols = x.shape
    return pl.pallas_call(
        _transpose_cast_kernel,
        grid=(rows // tr,),
        in_specs=[pl.BlockSpec((tr, cols), lambda i: (i, 0))],
        out_specs=pl.BlockSpec((cols, tr), lambda i: (0, i)),
        out_shape=jax.ShapeDtypeStruct((cols, rows), BF16),
        compiler_params=_params(("parallel",)),
        name="transpose_cast",
    )(x)


def _row(a):
    return a.reshape(1, -1).astype(F32)


def _pad_rows(w, start, total):
    return jnp.zeros((total, w.shape[1]), BF16).at[start:start + w.shape[0]].set(w.astype(BF16))


def kernel(x, norm1_g, w_in, rwkv_mu, rwkv_w0, rwkv_w2, rwkv_a0, rwkv_a2, rwkv_g2, rwkv_k_k, rwkv_k_a, rwkv_r_k, rwkv_ln_g, rwkv_ln_b, w_out, norm2_g, peer_wq, peer_sub_keys, peer_u, peer_v, final_g):
    batch, seq, d = x.shape
    tokens = batch * seq
    w = RWKV_WIDTH
    xt = x.reshape(tokens, d)

    w_in_p = jnp.pad(w_in[0].astype(BF16), ((0, 0), (0, IN_COLS_PAD - IN_COLS)))
    proj = _norm_matmul(xt, _row(norm1_g[0]), w_in_p, tm=1024, tn=512)
    proj = proj.reshape(batch, seq, IN_COLS_PAD)

    attn = _moba(proj, batch, seq, heads=4)

    mu = rwkv_mu[0]
    mu_l = jnp.pad(mu[3 * w:], (0, LORA_PAD - LORA_COLS))
    hid = lax.broadcasted_iota(jnp.int32, (w, w), 0) // RWKV_HEAD_DIM
    bd = (hid == hid.T).astype(BF16)
    prep_w = [_row(mu[:w]), _row(mu[w:2 * w]), _row(mu[2 * w:3 * w]), _row(mu_l),
              _row(rwkv_w0[0]), _pad_rows(rwkv_w2[0], 0, LORA_PAD),
              _row(rwkv_a0[0]), _pad_rows(rwkv_a2[0], DECAY_LORA, LORA_PAD),
              _pad_rows(rwkv_g2[0], DECAY_LORA + AAA_LORA, LORA_PAD),
              _row(rwkv_k_k[0]), _row(rwkv_k_a[0]), _row(rwkv_r_k[0]), bd]
    r, lw, k2, v, kk, b, g, bonus = _rwkv_prep(proj, batch, seq, prep_w, tt=256)
    y = _rwkv_scan(r, lw, k2, v, kk, b, tb=256)

    flat = lambda a: a.reshape(tokens, a.shape[-1])
    h, hn, hnt = _out_proj(xt, flat(attn), flat(y), flat(bonus), flat(g), _row(rwkv_ln_g[0]),
                      _row(rwkv_ln_b[0]), bd, w_out[0].astype(BF16), _row(norm2_g[0]), tm=256)

    keys = peer_sub_keys[0].reshape(2 * PEER_HEADS, PEER_NKEYS, PEER_HALF).astype(BF16)
    st = _peer_q(hn, peer_wq[0].astype(BF16), keys, tm=512)
    cnt1, e1, rank2, e2 = _peer_topk(st, tl=512, n1=4)
    out = _peer_dense(peer_u[0].astype(BF16), hnt, _transpose_cast(peer_v[0], tr=512),
                      cnt1, e1, rank2, e2, h, final_g.reshape(1, d), tt=512, n1=4)
    return out.reshape(batch, seq, d)
```

```python
import itertools
import math

import jax
import jax.numpy as jnp
from jax import lax
from jax.experimental import pallas as pl
from jax.experimental.pallas import tpu as pltpu

F32 = jnp.float32
BF16 = jnp.bfloat16

D_MODEL = 2048
ATTN_HEADS = 8
ATTN_HEAD_DIM = 128
ATTN_WIDTH = ATTN_HEADS * ATTN_HEAD_DIM
MOBA_BLOCK = 256
MOBA_TOPK = 3

RWKV_HEAD_DIM = 64
RWKV_WIDTH = D_MODEL - ATTN_WIDTH
RWKV_HEADS = RWKV_WIDTH // RWKV_HEAD_DIM
DECAY_LORA = 96
AAA_LORA = 96
GATE_LORA = 256
LORA_COLS = DECAY_LORA + AAA_LORA + GATE_LORA
LORA_PAD = 512
RWKV_GN_EPS = RWKV_HEAD_DIM * 1e-5
IN_COLS = 3 * ATTN_WIDTH + 3 * RWKV_WIDTH + LORA_COLS
IN_COLS_PAD = 3 * ATTN_WIDTH + 3 * RWKV_WIDTH + LORA_PAD

PEER_HEADS = 8
PEER_NKEYS = 128
PEER_HALF = 128
PEER_TOPK = 16

RMS_EPS = 1e-6
NEG = -1e30

LANES = 128
SCAN_CHUNK = 64
SCAN_INV_BLOCK = 16
DOWN_ROWS = 2048
MXU_COLS = 256
DENSE_ROWS = 32
VMEM_LIMIT = 56 * 1024 * 1024
DENSE_VMEM_LIMIT = 58 * 1024 * 1024

_NT = (((1,), (1,)), ((), ()))
_TN = (((0,), (0,)), ((), ()))


def _params(sem, flags=None):
    return pltpu.CompilerParams(dimension_semantics=sem, vmem_limit_bytes=VMEM_LIMIT, flags=flags)


def _dot(a, b):
    return jnp.dot(a.astype(BF16), b.astype(BF16), preferred_element_type=F32)


def _dot_nt(a, b):
    return lax.dot_general(a.astype(BF16), b.astype(BF16), _NT, preferred_element_type=F32)


def _dot_tn(a, b):
    return lax.dot_general(a.astype(BF16), b.astype(BF16), _TN, preferred_element_type=F32)


def _split3(x):
    hi = x.astype(BF16)
    r1 = x - hi.astype(F32)
    mid = r1.astype(BF16)
    lo = (r1 - mid.astype(F32)).astype(BF16)
    return hi, mid, lo


def _dot_exact_rhs(x, ones_bf16):
    hi = x.astype(BF16)
    lo = (x - hi.astype(F32)).astype(BF16)
    d = lambda p: jnp.dot(p, ones_bf16, preferred_element_type=F32)
    return d(hi) + d(lo)


def _dot_exact_lhs(ones_bf16, x):
    hi, mid, lo = _split3(x)
    d = lambda p: jnp.dot(ones_bf16, p, preferred_element_type=F32)
    return d(hi) + d(mid) + d(lo)


def _norm_matmul_kernel(x_ref, g_ref, w_ref, attn_ref, rwkv_ref, xn_ref, *, attn_tiles):
    j = pl.program_id(1)

    @pl.when(j == 0)
    def _():
        x = x_ref[...]
        inv = lax.rsqrt(jnp.mean(x * x, axis=-1, keepdims=True) + RMS_EPS)
        xn_ref[...] = (x * inv * g_ref[...]).astype(BF16)

    acc = jnp.dot(xn_ref[...], w_ref[...], preferred_element_type=F32)

    @pl.when(j < attn_tiles)
    def _():
        attn_ref[...] = acc.astype(BF16)

    @pl.when(j >= attn_tiles)
    def _():
        rwkv_ref[...] = acc


def _norm_matmul(x, g, w, n_attn, tm, tn):
    m, k = x.shape
    n = w.shape[1]
    attn_tiles = n_attn // tn
    kern = lambda *refs: _norm_matmul_kernel(*refs, attn_tiles=attn_tiles)
    return pl.pallas_call(
        kern,
        grid=(m // tm, n // tn),
        in_specs=[pl.BlockSpec((tm, k), lambda i, j: (i, 0)),
                  pl.BlockSpec((1, k), lambda i, j: (0, 0)),
                  pl.BlockSpec((k, tn), lambda i, j: (0, j))],
        out_specs=[pl.BlockSpec((tm, tn), lambda i, j: (i, jnp.minimum(j, attn_tiles - 1))),
                   pl.BlockSpec((tm, tn), lambda i, j: (i, jnp.maximum(j - attn_tiles, 0)))],
        out_shape=[jax.ShapeDtypeStruct((m, n_attn), BF16),
                   jax.ShapeDtypeStruct((m, n - n_attn), F32)],
        scratch_shapes=[pltpu.VMEM((tm, k), BF16)],
        compiler_params=_params(("parallel", "arbitrary")),
        name="in_proj",
    )(x, g, w)


def _moba_kernel(q_ref, k_ref, v_ref, o_ref, kmean_ref, vt_ref, *, nb, heads):
    blk_len = MOBA_BLOCK
    hd = ATTN_HEAD_DIM
    hg = pl.program_id(1)
    qi = pl.program_id(2)
    hrange = range(heads)

    @pl.when(qi == 0)
    def _():
        for hh in hrange:
            hl = slice(hh * hd, (hh + 1) * hd)
            for j in range(nb):
                rows = slice(j * blk_len, (j + 1) * blk_len)
                kj = k_ref[0, rows, hl].astype(F32)
                kmean_ref[hh, j:j + 1, :] = jnp.mean(kj, axis=0, keepdims=True)
                vt_ref[hh, :, rows] = v_ref[0, rows, hl].astype(F32).T.astype(BF16)

    log2e = math.log2(math.e)
    scale2 = (hd ** -0.5) * log2e
    kq = (lax.broadcasted_iota(jnp.int32, (blk_len, blk_len), 1)
          - lax.broadcasted_iota(jnp.int32, (blk_len, blk_len), 0))
    kqf = kq.astype(F32)
    blk = lax.broadcasted_iota(jnp.int32, (nb, blk_len), 0)
    past = blk < qi

    def select(hh):
        q = q_ref[0, :, hh * hd:(hh + 1) * hd].astype(F32)
        gate = lax.dot_general(kmean_ref[hh], q, _NT, precision=lax.Precision.HIGHEST,
                               preferred_element_type=F32)
        g = jnp.where(past, gate, NEG)
        selw = jnp.zeros(gate.shape, F32)
        for _ in range(min(MOBA_TOPK, nb)):
            m = jnp.max(g, axis=0, keepdims=True)
            first = jnp.min(jnp.where(g == m, blk, nb), axis=0, keepdims=True)
            pick = blk == first
            selw = jnp.where(pick, 1.0, selw)
            g = jnp.where(pick, -jnp.inf, g)
        return jnp.where(past, selw, 0.0), (q * scale2).astype(BF16)

    sel_q = [select(hh) for hh in hrange]
    selw = [x[0] for x in sel_q]
    qb = [x[1] for x in sel_q]
    slope2 = [jnp.exp2(jnp.zeros((1, 1), F32) - (hg * heads + hh + 1).astype(F32)) * log2e
              for hh in hrange]
    bias0 = [slope2[hh] * kqf for hh in hrange]

    def scores(j):
        r0 = pl.multiple_of(j * blk_len, blk_len)
        return [lax.dot_general(k_ref[0, pl.ds(r0, blk_len), hh * hd:(hh + 1) * hd], qb[hh], _NT,
                                preferred_element_type=F32) - bias0[hh] for hh in hrange]

    def weighted_values(j, p):
        r0 = pl.multiple_of(j * blk_len, blk_len)
        return [jnp.dot(vt_ref[hh, :, pl.ds(r0, blk_len)], p[hh].astype(BF16),
                        preferred_element_type=F32) for hh in hrange]

    s = [jnp.where(kq >= 0, x, NEG) for x in scores(qi)]
    m = [jnp.max(x, axis=0, keepdims=True) for x in s]
    p = [jnp.exp2(s[hh] - m[hh]) for hh in hrange]
    l = [jnp.sum(x, axis=0, keepdims=True) for x in p]
    acc = weighted_values(qi, p)

    def body(j, carry):
        m, l, acc = carry
        off = ((qi - j) * blk_len).astype(F32)
        s = scores(j)
        selj = [jnp.max(jnp.where(blk == j, selw[hh], 0.0), axis=0, keepdims=True) > 0.0
                for hh in hrange]
        shift = [slope2[hh] * off for hh in hrange]
        top = [jnp.max(s[hh], axis=0, keepdims=True) - shift[hh] for hh in hrange]
        m_new = [jnp.maximum(m[hh], jnp.where(selj[hh], top[hh], NEG)) for hh in hrange]
        alpha = [jnp.exp2(m[hh] - m_new[hh]) for hh in hrange]
        p = [jnp.exp2(s[hh] - jnp.where(selj[hh], m_new[hh] + shift[hh], jnp.inf)) for hh in hrange]
        l = [alpha[hh] * l[hh] + jnp.sum(p[hh], axis=0, keepdims=True) for hh in hrange]
        pv = weighted_values(j, p)
        acc = [alpha[hh] * acc[hh] + pv[hh] for hh in hrange]
        return m_new, l, acc

    m, l, acc = lax.fori_loop(0, qi, body, (m, l, acc))
    for hh in hrange:
        o_ref[0, :, hh * hd:(hh + 1) * hd] = (acc[hh] / l[hh]).T


def _moba(qkv, batch, seq, heads):
    nb = seq // MOBA_BLOCK
    hd = ATTN_HEAD_DIM
    hw = heads * hd
    groups = ATTN_HEADS // heads
    kern = lambda *refs: _moba_kernel(*refs, nb=nb, heads=heads)
    return pl.pallas_call(
        kern,
        grid=(batch, groups, nb),
        in_specs=[pl.BlockSpec((1, MOBA_BLOCK, hw), lambda b, h, i: (b, i, h)),
                  pl.BlockSpec((1, seq, hw), lambda b, h, i: (b, 0, groups + h)),
                  pl.BlockSpec((1, seq, hw), lambda b, h, i: (b, 0, 2 * groups + h))],
        out_specs=pl.BlockSpec((1, MOBA_BLOCK, hw), lambda b, h, i: (b, i, h)),
        out_shape=jax.ShapeDtypeStruct((batch, seq, ATTN_WIDTH), F32),
        scratch_shapes=[pltpu.VMEM((heads, nb, hd), F32),
                        pltpu.VMEM((heads, hd, seq), BF16)],
        compiler_params=_params(("parallel", "parallel", "arbitrary")),
        name="moba",
    )(qkv, qkv, qkv)


def _rwkv_prep_kernel(r_ref, k_ref, v_ref, l_ref, rp_ref, kp_ref, vp_ref, lp_ref,
                      mu_r, mu_k, mu_v, mu_l, w0, w2p, a0, a2p, g2p, kk_w, ka_w, rk_w, bd_ref,
                      r_o, lw_o, k_o, v_o, kk_o, b_o, g_o, bonus_o):
    first = pl.program_id(1) == 0

    def shift(cur_ref, prev_ref, mu_ref):
        cur = cur_ref[0]
        prev_last = jnp.where(first, 0.0, prev_ref[0, 7:8, :])
        rows = lax.broadcasted_iota(jnp.int32, cur.shape, 0)
        prev = jnp.where(rows == 0, prev_last, pltpu.roll(cur, 1, axis=0))
        return cur + (prev - cur) * mu_ref[...]

    r = shift(r_ref, rp_ref, mu_r)
    k = shift(k_ref, kp_ref, mu_k)
    v = shift(v_ref, vp_ref, mu_v)
    lo = shift(l_ref, lp_ref, mu_l)

    bd = bd_ref[...]
    z = -(w0[...] + _dot(jnp.tanh(lo), w2p[...]))
    softplus = jnp.maximum(z, 0.0) + jnp.log1p(jnp.exp(-jnp.abs(z)))
    lw = -jnp.exp(-softplus - 0.5)
    a = jax.nn.sigmoid(a0[...] + _dot(lo, a2p[...]))
    g = _dot(jax.nn.sigmoid(lo), g2p[...])
    kk = k * kk_w[...]
    kk = kk / jnp.maximum(jnp.sqrt(_dot_exact_rhs(kk * kk, bd)), 1e-12)
    k2 = k * (1.0 + (a - 1.0) * ka_w[...])
    r_o[0] = r
    lw_o[0] = lw
    k_o[0] = k2
    v_o[0] = v
    kk_o[0] = kk
    b_o[0] = kk * a
    g_o[0] = g
    bonus_o[0] = _dot_exact_rhs(r * k2 * rk_w[...], bd) * v


def _rwkv_prep(proj, batch, seq, weights, tt):
    w = RWKV_WIDTH
    c0 = 0
    lc = 3 * w // LORA_PAD
    pb = tt // 8
    cur = lambda width, col: pl.BlockSpec((1, tt, width), lambda b, i: (b, i, col))
    prev = lambda width, col: pl.BlockSpec(
        (1, 8, width), lambda b, i: (b, jnp.maximum(i * pb - 1, 0), col))
    full = lambda a: pl.BlockSpec(a.shape, lambda b, i: (0,) * a.ndim)
    out = pl.BlockSpec((1, tt, w), lambda b, i: (b, i, 0))
    return pl.pallas_call(
        _rwkv_prep_kernel,
        grid=(batch, seq // tt),
        in_specs=[cur(w, c0), cur(w, c0 + 1), cur(w, c0 + 2), cur(LORA_PAD, lc),
                  prev(w, c0), prev(w, c0 + 1), prev(w, c0 + 2), prev(LORA_PAD, lc)]
                 + [full(a) for a in weights],
        out_specs=[out] * 8,
        out_shape=[jax.ShapeDtypeStruct((batch, seq, w), F32)] * 8,
        compiler_params=_params(("parallel", "arbitrary")),
        name="rwkv_prep",
    )(proj, proj, proj, proj, proj, proj, proj, proj, *weights)


def _scan_kernel(r_ref, lw_ref, k_ref, v_ref, kk_ref, b_ref, y_ref, s_ref, *, tb, pairs):
    c = SCAN_CHUNK
    c2 = 2 * c

    @pl.when(pl.program_id(1) == 0)
    def _():
        s_ref[...] = jnp.zeros_like(s_ref)

    head0 = lax.broadcasted_iota(jnp.int32, (c, LANES), 1) < RWKV_HEAD_DIM
    row = lax.broadcasted_iota(jnp.int32, (c2, c2), 0)
    col = lax.broadcasted_iota(jnp.int32, (c2, c2), 1)
    same_head = (row // c) == (col // c)
    strict = jnp.logical_and(same_head, col < row)
    incl = jnp.logical_and(same_head, col <= row)
    inv_blk = (row // SCAN_INV_BLOCK) == (col // SCAN_INV_BLOCK)
    eye = jnp.where(row == col, 1.0, 0.0).astype(F32)
    tri = jnp.where(lax.broadcasted_iota(jnp.int32, (c, c), 0)
                    >= lax.broadcasted_iota(jnp.int32, (c, c), 1), 1.0, 0.0).astype(BF16)

    def stack(x):
        return jnp.concatenate([jnp.where(head0, x, 0.0), jnp.where(head0, 0.0, x)], axis=0)

    def pair_chunk(t0, p):
        lanes = slice(p * LANES, (p + 1) * LANES)
        rows = pl.ds(t0, c)
        lw = lw_ref[0, rows, lanes]
        cum = _dot_exact_lhs(tri, lw)
        yield
        r = r_ref[0, rows, lanes]
        k = k_ref[0, rows, lanes]
        kk = kk_ref[0, rows, lanes]
        b = b_ref[0, rows, lanes]
        cum_end = cum[c - 1:c, :]
        e_neg = jnp.exp(-cum)
        e_end = jnp.exp(cum_end - cum)
        a_s = stack(-kk * jnp.exp(cum - lw))
        r_s = stack(r * jnp.exp(cum))
        b_s = stack(b * e_neg)
        k_s = stack(k * e_neg)
        bend_s = stack(b * e_end)
        kend_s = stack(k * e_end)
        v_s = stack(v_ref[0, rows, lanes])
        gram = _dot_nt(jnp.concatenate([a_s, r_s], axis=0),
                       jnp.concatenate([b_s, k_s], axis=0))
        kv = _dot_tn(v_s, kend_s)
        yield
        l_ab = jnp.where(strict, gram[:c2, :c2], 0.0)
        l_ak = jnp.where(strict, gram[:c2, c2:], 0.0)
        l_rb = jnp.where(incl, gram[c2:, :c2], 0.0)
        l_rk = jnp.where(incl, gram[c2:, c2:], 0.0)

        x = jnp.where(inv_blk, l_ab, 0.0)
        off = jnp.where(inv_blk, 0.0, l_ab)
        dinv = eye + x
        w1 = _dot(l_ak, v_s)
        y1 = _dot(l_rk, v_s)
        for _ in range(int(math.log2(SCAN_INV_BLOCK)) - 1):
            x = _dot(x, x)
            yield
            dinv = dinv + _dot(dinv, x)
            yield
        f = _dot(dinv, off)
        yield
        f2 = _dot(f, f)
        yield
        gm = eye + f
        gm = gm + _dot(gm, f2)
        yield
        t_inv = _dot(gm, dinv)
        yield
        ta = _dot(t_inv, jnp.concatenate([a_s, w1], axis=1))
        yield
        a_hat = ta[:, :LANES]
        v_hat = ta[:, LANES:]
        state = s_ref[p]
        m1 = _dot_nt(jnp.concatenate([a_hat, r_s], axis=0), state)
        yield
        u = m1[:c2] + v_hat
        y = m1[c2:] + y1 + _dot(l_rb, u)
        s_ref[p] = state * jnp.exp(cum_end) + kv + _dot_tn(u, bend_s)
        y_ref[0, rows, lanes] = y[:c] + y[c:]

    def chunk(ci, carry):
        t0 = pl.multiple_of(ci * c, c)
        for _ in itertools.zip_longest(*[pair_chunk(t0, p) for p in range(pairs)]):
            pass
        return carry

    lax.fori_loop(0, tb // c, chunk, 0)


def _rwkv_scan(r, lw, k, v, kk, b, tb):
    batch, seq, w = r.shape
    pairs = w // LANES
    spec = pl.BlockSpec((1, tb, w), lambda bi, t: (bi, t, 0))
    kern = lambda *refs: _scan_kernel(*refs, tb=tb, pairs=pairs)
    return pl.pallas_call(
        kern,
        grid=(batch, seq // tb),
        in_specs=[spec] * 6,
        out_specs=spec,
        out_shape=jax.ShapeDtypeStruct((batch, seq, w), F32),
        scratch_shapes=[pltpu.VMEM((pairs, LANES, LANES), F32)],
        compiler_params=_params(("parallel", "arbitrary")),
        name="rwkv_scan",
    )(r, lw, k, v, kk, b)


def _out_proj_kernel(x_ref, attn_ref, y_ref, bonus_ref, g_ref, lng, lnb, bd_ref, w_ref, n2g,
                     h_ref, hn_ref, hnt_ref):
    bd = bd_ref[...]
    y = y_ref[...]
    inv_n = 1.0 / RWKV_HEAD_DIM
    d = y - _dot(y, bd) * inv_n
    var = _dot(d * d, bd) * inv_n
    yn = d * lax.rsqrt(var + RWKV_GN_EPS) * lng[...] + lnb[...]
    rw = (yn + bonus_ref[...]) * g_ref[...]
    h = (x_ref[...] + _dot(attn_ref[...], w_ref[:ATTN_WIDTH, :])
         + _dot(rw, w_ref[ATTN_WIDTH:, :]))
    h_ref[...] = h
    inv = lax.rsqrt(jnp.mean(h * h, axis=-1, keepdims=True) + RMS_EPS)
    hn = h * inv * n2g[...]
    hn_ref[...] = hn.astype(BF16)
    hnt_ref[...] = hn.T.astype(BF16)


def _out_proj(x, attn, y, bonus, g, lng, lnb, bd, w_out, n2g, tm):
    m = x.shape[0]
    row = lambda width: pl.BlockSpec((tm, width), lambda i: (i, 0))
    full = lambda a: pl.BlockSpec(a.shape, lambda i: (0,) * a.ndim)
    return pl.pallas_call(
        _out_proj_kernel,
        grid=(m // tm,),
        in_specs=[row(D_MODEL), row(ATTN_WIDTH), row(RWKV_WIDTH), row(RWKV_WIDTH), row(RWKV_WIDTH),
                  full(lng), full(lnb), full(bd), full(w_out), full(n2g)],
        out_specs=[row(D_MODEL), row(D_MODEL), pl.BlockSpec((D_MODEL, tm), lambda i: (0, i))],
        out_shape=[jax.ShapeDtypeStruct((m, D_MODEL), F32),
                   jax.ShapeDtypeStruct((m, D_MODEL), BF16),
                   jax.ShapeDtypeStruct((D_MODEL, m), BF16)],
        compiler_params=_params(("parallel",)),
        name="out_proj",
    )(x, attn, y, bonus, g, lng, lnb, bd, w_out, n2g)


def _peer_q_kernel(hn_ref, wq_ref, keys_ref, st_ref):
    q = jnp.dot(hn_ref[...], wq_ref[...], preferred_element_type=F32)
    for hp in range(2 * PEER_HEADS):
        qs = q[:, hp * PEER_HALF:(hp + 1) * PEER_HALF]
        st_ref[hp] = _dot_nt(keys_ref[hp], qs)


def _peer_q(hn, wq, keys, tm):
    m = hn.shape[0]
    return pl.pallas_call(
        _peer_q_kernel,
        grid=(m // tm,),
        in_specs=[pl.BlockSpec((tm, D_MODEL), lambda i: (i, 0)),
                  pl.BlockSpec(wq.shape, lambda i: (0, 0)),
                  pl.BlockSpec(keys.shape, lambda i: (0, 0, 0))],
        out_specs=pl.BlockSpec((2 * PEER_HEADS, PEER_NKEYS, tm), lambda i: (0, 0, i)),
        out_shape=jax.ShapeDtypeStruct((2 * PEER_HEADS, PEER_NKEYS, m), F32),
        compiler_params=_params(("parallel",)),
        name="peer_q",
    )(hn, wq, keys)


def _top_desc(x, k, exact_ties):
    n = x.shape[0]
    rows = lax.broadcasted_iota(jnp.int32, x.shape, 0)
    rank = jnp.full(x.shape, float(k), F32)
    outs = []
    for i in range(k):
        m = jnp.max(x, axis=0, keepdims=True)
        outs.append(m)
        pick = x == m
        if exact_ties:
            pick = rows == jnp.min(jnp.where(pick, rows, n), axis=0, keepdims=True)
        x = jnp.where(pick, -jnp.inf, x)
        rank = jnp.where(pick, float(i), rank)
    gone = jnp.sum(jnp.where(x == -jnp.inf, 1.0, 0.0), axis=0, keepdims=True)
    return jnp.concatenate(outs, axis=0), rank, gone


def _peer_topk_kernel(s_ref, cnt1_ref, e1_ref, rank2_ref, e2_ref, *, n1):
    k = PEER_TOPK
    widths = [k // (a + 1) for a in range(k)]
    pad = -sum(widths) % 8

    def run(exact_ties):
        s1 = s_ref[0]
        s2 = s_ref[1]
        sv1, rank1, gone1 = _top_desc(s1, k, exact_ties)
        sv2, rank2, gone2 = _top_desc(s2, k, exact_ties)
        cand = jnp.concatenate([sv1[a:a + 1] + sv2[:widths[a]] for a in range(k)]
                               + [jnp.full((pad, s1.shape[1]), -jnp.inf, F32)], axis=0)
        best, pos, gone3 = _top_desc(cand, k, exact_ties)
        chosen = jnp.where(pos < float(k), 1.0, 0.0)
        z = jnp.sum(jnp.exp(best - best[0:1]), axis=0, keepdims=True)
        cnt1 = jnp.zeros(s1.shape, F32)
        row = 0
        for a in range(k):
            cnt_a = jnp.sum(chosen[row:row + widths[a]], axis=0, keepdims=True)
            cnt1 = jnp.where(rank1 == float(a), cnt_a, cnt1)
            row += widths[a]
        e1 = jnp.exp(s1 - sv1[0:1]) / z
        for t in range(PEER_NKEYS // n1):
            cnt1_ref[0, t] = cnt1[t * n1:(t + 1) * n1]
            e1_ref[0, t] = e1[t * n1:(t + 1) * n1]
        rank2_ref[0] = rank2
        e2_ref[0] = jnp.exp(s2 - sv2[0:1])
        return jnp.max(jnp.maximum(jnp.maximum(gone1, gone2), gone3 - float(pad)))

    most_gone = run(exact_ties=False)

    @pl.when(most_gone > float(k))
    def _():
        run(exact_ties=True)


def _peer_topk(st, tl, n1):
    m = st.shape[2]
    n_tiles = PEER_NKEYS // n1
    spec = pl.BlockSpec((1, PEER_NKEYS, tl), lambda h, i: (h, 0, i))
    shape = jax.ShapeDtypeStruct((PEER_HEADS, PEER_NKEYS, m), F32)
    tile_spec = pl.BlockSpec((1, n_tiles, n1, tl), lambda h, i: (h, 0, 0, i))
    tile_shape = jax.ShapeDtypeStruct((PEER_HEADS, n_tiles, n1, m), F32)
    kern = lambda *refs: _peer_topk_kernel(*refs, n1=n1)
    return pl.pallas_call(
        kern,
        grid=(PEER_HEADS, m // tl),
        in_specs=[pl.BlockSpec((2, PEER_NKEYS, tl), lambda h, i: (h, 0, i))],
        out_specs=[tile_spec, tile_spec, spec, spec],
        out_shape=[tile_shape, tile_shape, shape, shape],
        compiler_params=_params(("parallel", "parallel")),
        name="peer_topk",
    )(st)


def _peer_dense_kernel(u_ref, hnt_ref, vt_ref, rank2_ref, e2_ref, cnta_ref, cntb_ref,
                       e1a_ref, e1b_ref, h_ref, fg_ref, out_ref, act_ref, hpre_ref, yt_ref, *, n1):
    step = pl.program_id(1)
    last_step = pl.num_programs(1) - 1
    et = hpre_ref.shape[1]
    row_refs = ((cnta_ref, e1a_ref), (cntb_ref, e1b_ref))

    def gate_tile(half, jt, ib):
        prev = 1 - half
        cnt_ref, e1r_ref = row_refs[half]
        ls = slice(jt * LANES, (jt + 1) * LANES)
        rs = slice(ib * DENSE_ROWS, (ib + 1) * DENSE_ROWS)
        gates = [jnp.zeros((DENSE_ROWS, LANES), F32) for _ in range(n1)]
        for h in range(PEER_HEADS):
            r2 = rank2_ref[h, rs, ls]
            e2 = e2_ref[h, rs, ls]
            for a in range(n1):
                hit = r2 < cnt_ref[h, 0, a:a + 1, ls]
                gates[a] = gates[a] + jnp.where(hit, e2, 0.0) * e1r_ref[h, 0, a:a + 1, ls]
        for a in range(n1):
            ers = slice(a * PEER_NKEYS + ib * DENSE_ROWS, a * PEER_NKEYS + (ib + 1) * DENSE_ROWS)
            x = hpre_ref[prev, ers, ls]
            gelu = 0.5 * x * (1.0 + lax.erf(x * (0.5 ** 0.5)))
            act_ref[prev, ers, ls] = gelu * gates[a]

    tt = hpre_ref.shape[2]
    n_ib = PEER_NKEYS // DENSE_ROWS

    def run(first, last):
        for half in range(2):
            cur = half
            hs = slice(half * et, (half + 1) * et)
            gated = not (first and half == 0) and not (last and half == 1)
            for piece in range(tt // MXU_COLS):
                cs = slice(piece * MXU_COLS, (piece + 1) * MXU_COLS)
                tiles = [(jt, ib)
                         for jt in range(piece * MXU_COLS // LANES, (piece + 1) * MXU_COLS // LANES)
                         for ib in range(n_ib)]
                if not last:
                    hpre_ref[cur, :, cs] = jnp.dot(u_ref[hs, :], hnt_ref[:, cs],
                                                   preferred_element_type=F32)
                per_down = len(tiles) * DOWN_ROWS // D_MODEL
                for i, tile in enumerate(tiles):
                    if not first and i % per_down == per_down // 2:
                        ms = slice(i // per_down * DOWN_ROWS, (i // per_down + 1) * DOWN_ROWS)
                        yt_ref[ms, cs] += jnp.dot(vt_ref[ms, hs],
                                                  act_ref[cur, :, cs].astype(BF16),
                                                  preferred_element_type=F32)
                    if gated:
                        gate_tile(half, *tile)

    @pl.when(step == 0)
    def _():
        yt_ref[...] = jnp.zeros_like(yt_ref)
        run(first=True, last=False)

    @pl.when(jnp.logical_and(step > 0, step < last_step))
    def _():
        run(first=False, last=False)

    @pl.when(step == last_step)
    def _():
        run(first=False, last=True)
        h = h_ref[...] + yt_ref[...].T
        inv = lax.rsqrt(jnp.mean(h * h, axis=-1, keepdims=True) + RMS_EPS)
        out_ref[...] = h * inv * fg_ref[...]


def _peer_dense(u, hnt, vt, cntr, e1r, rank2, e2, h, final_g, tt, n1):
    m = hnt.shape[1]
    et = n1 * PEER_NKEYS
    n_steps = PEER_NKEYS // (2 * n1)
    n_tiles = 2 * n_steps
    tok3 = lambda a: pl.BlockSpec((a.shape[0], a.shape[1], tt), lambda i, e: (0, 0, i))
    rows = lambda half: pl.BlockSpec(
        (PEER_HEADS, 1, n1, tt),
        lambda i, e: (0, jnp.clip(2 * e + half - 1, 0, n_tiles - 1), 0, i))
    kern = lambda *refs: _peer_dense_kernel(*refs, n1=n1)
    return pl.pallas_call(
        kern,
        grid=(m // tt, n_steps + 1),
        in_specs=[pl.BlockSpec((2 * et, D_MODEL), lambda i, e: (jnp.minimum(e, n_steps - 1), 0)),
                  pl.BlockSpec((D_MODEL, tt), lambda i, e: (0, i)),
                  pl.BlockSpec((D_MODEL, 2 * et), lambda i, e: (0, jnp.maximum(e - 1, 0))),
                  tok3(rank2), tok3(e2), rows(0), rows(1), rows(0), rows(1),
                  pl.BlockSpec((tt, D_MODEL), lambda i, e: (i, 0)),
                  pl.BlockSpec((1, D_MODEL), lambda i, e: (0, 0))],
        out_specs=pl.BlockSpec((tt, D_MODEL), lambda i, e: (i, 0)),
        out_shape=jax.ShapeDtypeStruct((m, D_MODEL), F32),
        scratch_shapes=[pltpu.VMEM((2, et, tt), F32), pltpu.VMEM((2, et, tt), F32),
                        pltpu.VMEM((D_MODEL, tt), F32)],
        compiler_params=pltpu.CompilerParams(dimension_semantics=("parallel", "arbitrary"),
                                             vmem_limit_bytes=DENSE_VMEM_LIMIT),
        name="peer_dense",
    )(u, hnt, vt, rank2, e2, cntr, cntr, e1r, e1r, h, final_g)


def _transpose_cast_kernel(x_ref, o_ref):
    o_ref[...] = x_ref[...].T.astype(o_ref.dtype)


def _transpose_cast(x, tr):
    rows, cols = x.shape
    return pl.pallas_call(
        _transpose_cast_kernel,
        grid=(rows // tr,),
        in_specs=[pl.BlockSpec((tr, cols), lambda i: (i, 0))],
        out_specs=pl.BlockSpec((cols, tr), lambda i: (0, i)),
        out_shape=jax.ShapeDtypeStruct((cols, rows), BF16),
        compiler_params=_params(("parallel",)),
        name="transpose_cast",
    )(x)


def _row(a):
    return a.reshape(1, -1).astype(F32)


def _pad_rows(w, start, total):
    return jnp.zeros((total, w.shape[1]), BF16).at[start:start + w.shape[0]].set(w.astype(BF16))


def kernel(x, norm1_g, w_in, rwkv_mu, rwkv_w0, rwkv_w2, rwkv_a0, rwkv_a2, rwkv_g2, rwkv_k_k, rwkv_k_a, rwkv_r_k, rwkv_ln_g, rwkv_ln_b, w_out, norm2_g, peer_wq, peer_sub_keys, peer_u, peer_v, final_g):
    batch, seq, d = x.shape
    tokens = batch * seq
    w = RWKV_WIDTH
    xt = x.reshape(tokens, d)

    w_in_p = jnp.pad(w_in[0].astype(BF16), ((0, 0), (0, IN_COLS_PAD - IN_COLS)))
    qkv, proj = _norm_matmul(xt, _row(norm1_g[0]), w_in_p, 3 * ATTN_WIDTH, tm=1024, tn=512)
    proj = proj.reshape(batch, seq, IN_COLS_PAD - 3 * ATTN_WIDTH)

    attn = _moba(qkv.reshape(batch, seq, 3 * ATTN_WIDTH), batch, seq, heads=8)

    mu = rwkv_mu[0]
    mu_l = jnp.pad(mu[3 * w:], (0, LORA_PAD - LORA_COLS))
    hid = lax.broadcasted_iota(jnp.int32, (w, w), 0) // RWKV_HEAD_DIM
    bd = (hid == hid.T).astype(BF16)
    prep_w = [_row(mu[:w]), _row(mu[w:2 * w]), _row(mu[2 * w:3 * w]), _row(mu_l),
              _row(rwkv_w0[0]), _pad_rows(rwkv_w2[0], 0, LORA_PAD),
              _row(rwkv_a0[0]), _pad_rows(rwkv_a2[0], DECAY_LORA, LORA_PAD),
              _pad_rows(rwkv_g2[0], DECAY_LORA + AAA_LORA, LORA_PAD),
              _row(rwkv_k_k[0]), _row(rwkv_k_a[0]), _row(rwkv_r_k[0]), bd]
    r, lw, k2, v, kk, b, g, bonus = _rwkv_prep(proj, batch, seq, prep_w, tt=256)
    y = _rwkv_scan(r, lw, k2, v, kk, b, tb=256)

    flat = lambda a: a.reshape(tokens, a.shape[-1])
    h, hn, hnt = _out_proj(xt, flat(attn), flat(y), flat(bonus), flat(g), _row(rwkv_ln_g[0]),
                      _row(rwkv_ln_b[0]), bd, w_out[0].astype(BF16), _row(norm2_g[0]), tm=256)

    keys = peer_sub_keys[0].reshape(2 * PEER_HEADS, PEER_NKEYS, PEER_HALF).astype(BF16)
    st = _peer_q(hn, peer_wq[0].astype(BF16), keys, tm=512)
    cnt1, e1, rank2, e2 = _peer_topk(st, tl=512, n1=4)
    out = _peer_dense(peer_u[0].astype(BF16), hnt, _transpose_cast(peer_v[0], tr=512),
                      cnt1, e1, rank2, e2, h, final_g.reshape(1, d), tt=512, n1=4)
    return out.reshape(batch, seq, d)
```

```python
import itertools
import math

import jax
import jax.numpy as jnp
from jax import lax
from jax.experimental import pallas as pl
from jax.experimental.pallas import tpu as pltpu

F32 = jnp.float32
BF16 = jnp.bfloat16

D_MODEL = 2048
ATTN_HEADS = 8
ATTN_HEAD_DIM = 128
ATTN_WIDTH = ATTN_HEADS * ATTN_HEAD_DIM
MOBA_BLOCK = 256
MOBA_TOPK = 3

RWKV_HEAD_DIM = 64
RWKV_WIDTH = D_MODEL - ATTN_WIDTH
RWKV_HEADS = RWKV_WIDTH // RWKV_HEAD_DIM
DECAY_LORA = 96
AAA_LORA = 96
GATE_LORA = 256
LORA_COLS = DECAY_LORA + AAA_LORA + GATE_LORA
LORA_PAD = 512
RWKV_GN_EPS = RWKV_HEAD_DIM * 1e-5
IN_COLS = 3 * ATTN_WIDTH + 3 * RWKV_WIDTH + LORA_COLS
IN_COLS_PAD = 3 * ATTN_WIDTH + 3 * RWKV_WIDTH + LORA_PAD

PEER_HEADS = 8
PEER_NKEYS = 128
PEER_HALF = 128
PEER_TOPK = 16

RMS_EPS = 1e-6
NEG = -1e30

LANES = 128
SCAN_CHUNK = 64
SCAN_INV_BLOCK = 16
DOWN_ROWS = 2048
MXU_COLS = 256
DENSE_ROWS = 32
VMEM_LIMIT = 56 * 1024 * 1024
DENSE_VMEM_LIMIT = 58 * 1024 * 1024

_NT = (((1,), (1,)), ((), ()))
_TN = (((0,), (0,)), ((), ()))


def _params(sem, vmem_limit=VMEM_LIMIT):
    return pltpu.CompilerParams(dimension_semantics=sem, vmem_limit_bytes=vmem_limit)


def _dot(a, b):
    return jnp.dot(a.astype(BF16), b.astype(BF16), preferred_element_type=F32)


def _dot_nt(a, b):
    return lax.dot_general(a.astype(BF16), b.astype(BF16), _NT, preferred_element_type=F32)


def _dot_tn(a, b):
    return lax.dot_general(a.astype(BF16), b.astype(BF16), _TN, preferred_element_type=F32)


def _split3(x):
    hi = x.astype(BF16)
    r1 = x - hi.astype(F32)
    mid = r1.astype(BF16)
    lo = (r1 - mid.astype(F32)).astype(BF16)
    return hi, mid, lo


def _dot_2piece_rhs(x, ones_bf16):
    hi = x.astype(BF16)
    lo = (x - hi.astype(F32)).astype(BF16)
    d = lambda p: jnp.dot(p, ones_bf16, preferred_element_type=F32)
    return d(hi) + d(lo)


def _head_sums(x, gather, scatter):
    return _dot_2piece_rhs(_dot_2piece_rhs(x, gather), scatter)


def _dot_3piece_lhs(ones_bf16, x):
    hi, mid, lo = _split3(x)
    d = lambda p: jnp.dot(ones_bf16, p, preferred_element_type=F32)
    return d(hi) + d(mid) + d(lo)


def _norm_matmul_kernel(x_ref, g_ref, w_ref, attn_ref, rwkv_ref, xn_ref, *, attn_tiles):
    j = pl.program_id(1)

    @pl.when(j == 0)
    def _():
        x = x_ref[...]
        inv = lax.rsqrt(jnp.mean(x * x, axis=-1, keepdims=True) + RMS_EPS)
        xn_ref[...] = (x * inv * g_ref[...]).astype(BF16)

    acc = jnp.dot(xn_ref[...], w_ref[...], preferred_element_type=F32)

    @pl.when(j < attn_tiles)
    def _():
        attn_ref[...] = acc.astype(BF16)

    @pl.when(j >= attn_tiles)
    def _():
        rwkv_ref[...] = acc


def _norm_matmul(x, g, w, n_attn, tm, tn):
    m, k = x.shape
    n = w.shape[1]
    attn_tiles = n_attn // tn
    kern = lambda *refs: _norm_matmul_kernel(*refs, attn_tiles=attn_tiles)
    return pl.pallas_call(
        kern,
        grid=(m // tm, n // tn),
        in_specs=[pl.BlockSpec((tm, k), lambda i, j: (i, 0)),
                  pl.BlockSpec((1, k), lambda i, j: (0, 0)),
                  pl.BlockSpec((k, tn), lambda i, j: (0, j))],
        out_specs=[pl.BlockSpec((tm, tn), lambda i, j: (i, jnp.minimum(j, attn_tiles - 1))),
                   pl.BlockSpec((tm, tn), lambda i, j: (i, jnp.maximum(j - attn_tiles, 0)))],
        out_shape=[jax.ShapeDtypeStruct((m, n_attn), BF16),
                   jax.ShapeDtypeStruct((m, n - n_attn), F32)],
        scratch_shapes=[pltpu.VMEM((tm, k), BF16)],
        compiler_params=_params(("parallel", "arbitrary")),
        name="in_proj",
    )(x, g, w)


def _moba_kernel(q_ref, k_ref, v_ref, o_ref, kmean_ref, vt_ref, *, nb, heads):
    blk_len = MOBA_BLOCK
    hd = ATTN_HEAD_DIM
    hg = pl.program_id(1)
    qi = pl.program_id(2)
    hrange = range(heads)

    @pl.when(qi == 0)
    def _():
        for hh in hrange:
            hl = slice(hh * hd, (hh + 1) * hd)
            for j in range(nb):
                rows = slice(j * blk_len, (j + 1) * blk_len)
                kj = k_ref[0, rows, hl].astype(F32)
                kmean_ref[hh, j:j + 1, :] = jnp.mean(kj, axis=0, keepdims=True)
                vt_ref[hh, :, rows] = v_ref[0, rows, hl].astype(F32).T.astype(BF16)

    log2e = math.log2(math.e)
    scale2 = (hd ** -0.5) * log2e
    kq = (lax.broadcasted_iota(jnp.int32, (blk_len, blk_len), 1)
          - lax.broadcasted_iota(jnp.int32, (blk_len, blk_len), 0))
    kqf = kq.astype(F32)
    blk = lax.broadcasted_iota(jnp.int32, (nb, blk_len), 0)
    past = blk < qi

    def select(hh):
        q = q_ref[0, :, hh * hd:(hh + 1) * hd].astype(F32)
        gate = lax.dot_general(kmean_ref[hh], q, _NT, precision=lax.Precision.HIGHEST,
                               preferred_element_type=F32)
        g = jnp.where(past, gate, NEG)
        selw = jnp.zeros(gate.shape, F32)
        for _ in range(min(MOBA_TOPK, nb)):
            m = jnp.max(g, axis=0, keepdims=True)
            first = jnp.min(jnp.where(g == m, blk, nb), axis=0, keepdims=True)
            pick = blk == first
            selw = jnp.where(pick, 1.0, selw)
            g = jnp.where(pick, -jnp.inf, g)
        return jnp.where(past, selw, 0.0), (q * scale2).astype(BF16)

    sel_q = [select(hh) for hh in hrange]
    selw = [x[0] for x in sel_q]
    qb = [x[1] for x in sel_q]
    slope2 = [jnp.exp2(jnp.zeros((1, 1), F32) - (hg * heads + hh + 1).astype(F32)) * log2e
              for hh in hrange]
    bias0 = [slope2[hh] * kqf for hh in hrange]

    def scores(j):
        r0 = pl.multiple_of(j * blk_len, blk_len)
        return [lax.dot_general(k_ref[0, pl.ds(r0, blk_len), hh * hd:(hh + 1) * hd], qb[hh], _NT,
                                preferred_element_type=F32) - bias0[hh] for hh in hrange]

    def weighted_values(j, p):
        r0 = pl.multiple_of(j * blk_len, blk_len)
        return [jnp.dot(vt_ref[hh, :, pl.ds(r0, blk_len)], p[hh].astype(BF16),
                        preferred_element_type=F32) for hh in hrange]

    s = [jnp.where(kq >= 0, x, NEG) for x in scores(qi)]
    m = [jnp.max(x, axis=0, keepdims=True) for x in s]
    p = [jnp.exp2(s[hh] - m[hh]) for hh in hrange]
    l = [jnp.sum(x, axis=0, keepdims=True) for x in p]
    acc = weighted_values(qi, p)

    def body(j, carry):
        m, l, acc = carry
        off = ((qi - j) * blk_len).astype(F32)
        s = scores(j)
        selj = [jnp.max(jnp.where(blk == j, selw[hh], 0.0), axis=0, keepdims=True) > 0.0
                for hh in hrange]
        shift = [slope2[hh] * off for hh in hrange]
        top = [jnp.max(s[hh], axis=0, keepdims=True) - shift[hh] for hh in hrange]
        m_new = [jnp.maximum(m[hh], jnp.where(selj[hh], top[hh], NEG)) for hh in hrange]
        alpha = [jnp.exp2(m[hh] - m_new[hh]) for hh in hrange]
        p = [jnp.exp2(s[hh] - jnp.where(selj[hh], m_new[hh] + shift[hh], jnp.inf)) for hh in hrange]
        l = [alpha[hh] * l[hh] + jnp.sum(p[hh], axis=0, keepdims=True) for hh in hrange]
        pv = weighted_values(j, p)
        acc = [alpha[hh] * acc[hh] + pv[hh] for hh in hrange]
        return m_new, l, acc

    m, l, acc = lax.fori_loop(0, qi, body, (m, l, acc))
    for hh in hrange:
        o_ref[0, :, hh * hd:(hh + 1) * hd] = (acc[hh] / l[hh]).T


def _moba(qkv, batch, seq, heads):
    nb = seq // MOBA_BLOCK
    hd = ATTN_HEAD_DIM
    hw = heads * hd
    groups = ATTN_HEADS // heads
    kern = lambda *refs: _moba_kernel(*refs, nb=nb, heads=heads)
    return pl.pallas_call(
        kern,
        grid=(batch, groups, nb),
        in_specs=[pl.BlockSpec((1, MOBA_BLOCK, hw), lambda b, h, i: (b, i, h)),
                  pl.BlockSpec((1, seq, hw), lambda b, h, i: (b, 0, groups + h)),
                  pl.BlockSpec((1, seq, hw), lambda b, h, i: (b, 0, 2 * groups + h))],
        out_specs=pl.BlockSpec((1, MOBA_BLOCK, hw), lambda b, h, i: (b, i, h)),
        out_shape=jax.ShapeDtypeStruct((batch, seq, ATTN_WIDTH), F32),
        scratch_shapes=[pltpu.VMEM((heads, nb, hd), F32),
                        pltpu.VMEM((heads, hd, seq), BF16)],
        compiler_params=_params(("parallel", "parallel", "arbitrary")),
        name="moba",
    )(qkv, qkv, qkv)


def _rwkv_prep_kernel(r_ref, k_ref, v_ref, l_ref, rp_ref, kp_ref, vp_ref, lp_ref,
                      mu_r, mu_k, mu_v, mu_l, w0, w2p, a0, a2p, g2p, kk_w, ka_w, rk_w, hg_ref, hs_ref,
                      r_o, lw_o, k_o, v_o, kk_o, b_o, g_o, bonus_o):
    first = pl.program_id(1) == 0

    def shift(cur_ref, prev_ref, mu_ref):
        cur = cur_ref[0]
        prev_last = jnp.where(first, 0.0, prev_ref[0, 7:8, :])
        rows = lax.broadcasted_iota(jnp.int32, cur.shape, 0)
        prev = jnp.where(rows == 0, prev_last, pltpu.roll(cur, 1, axis=0))
        return cur + (prev - cur) * mu_ref[...]

    r = shift(r_ref, rp_ref, mu_r)
    k = shift(k_ref, kp_ref, mu_k)
    v = shift(v_ref, vp_ref, mu_v)
    lo = shift(l_ref, lp_ref, mu_l)

    gather = hg_ref[...]
    scatter = hs_ref[...]
    z = -(w0[...] + _dot(jnp.tanh(lo), w2p[...]))
    softplus = jnp.maximum(z, 0.0) + jnp.log1p(jnp.exp(-jnp.abs(z)))
    lw = -jnp.exp(-softplus - 0.5)
    a = jax.nn.sigmoid(a0[...] + _dot(lo, a2p[...]))
    g = _dot(jax.nn.sigmoid(lo), g2p[...])
    kk = k * kk_w[...]
    kk = kk / jnp.maximum(jnp.sqrt(_head_sums(kk * kk, gather, scatter)), 1e-12)
    k2 = k * (1.0 + (a - 1.0) * ka_w[...])
    r_o[0] = r
    lw_o[0] = lw
    k_o[0] = k2
    v_o[0] = v
    kk_o[0] = kk
    b_o[0] = kk * a
    g_o[0] = g
    bonus_o[0] = _head_sums(r * k2 * rk_w[...], gather, scatter) * v


def _rwkv_prep(proj, batch, seq, weights, tt):
    w = RWKV_WIDTH
    c0 = 0
    lc = 3 * w // LORA_PAD
    pb = tt // 8
    cur = lambda width, col: pl.BlockSpec((1, tt, width), lambda b, i: (b, i, col))
    prev = lambda width, col: pl.BlockSpec(
        (1, 8, width), lambda b, i: (b, jnp.maximum(i * pb - 1, 0), col))
    full = lambda a: pl.BlockSpec(a.shape, lambda b, i: (0,) * a.ndim)
    out = pl.BlockSpec((1, tt, w), lambda b, i: (b, i, 0))
    return pl.pallas_call(
        _rwkv_prep_kernel,
        grid=(batch, seq // tt),
        in_specs=[cur(w, c0), cur(w, c0 + 1), cur(w, c0 + 2), cur(LORA_PAD, lc),
                  prev(w, c0), prev(w, c0 + 1), prev(w, c0 + 2), prev(LORA_PAD, lc)]
                 + [full(a) for a in weights],
        out_specs=[out] * 8,
        out_shape=[jax.ShapeDtypeStruct((batch, seq, w), F32)] * 8,
        compiler_params=_params(("parallel", "arbitrary")),
        name="rwkv_prep",
    )(proj, proj, proj, proj, proj, proj, proj, proj, *weights)


def _scan_kernel(r_ref, lw_ref, k_ref, v_ref, kk_ref, b_ref, y_ref, s_ref, *, tb, pairs):
    c = SCAN_CHUNK
    c2 = 2 * c

    @pl.when(pl.program_id(1) == 0)
    def _():
        s_ref[...] = jnp.zeros_like(s_ref)

    head0 = lax.broadcasted_iota(jnp.int32, (c, LANES), 1) < RWKV_HEAD_DIM
    row = lax.broadcasted_iota(jnp.int32, (c2, c2), 0)
    col = lax.broadcasted_iota(jnp.int32, (c2, c2), 1)
    same_head = (row // c) == (col // c)
    strict = jnp.logical_and(same_head, col < row)
    incl = jnp.logical_and(same_head, col <= row)
    inv_blk = (row // SCAN_INV_BLOCK) == (col // SCAN_INV_BLOCK)
    eye = jnp.where(row == col, 1.0, 0.0).astype(F32)
    tri = jnp.where(lax.broadcasted_iota(jnp.int32, (c, c), 0)
                    >= lax.broadcasted_iota(jnp.int32, (c, c), 1), 1.0, 0.0).astype(BF16)

    def stack(x):
        return jnp.concatenate([jnp.where(head0, x, 0.0), jnp.where(head0, 0.0, x)], axis=0)

    def pair_chunk(t0, p):
        lanes = slice(p * LANES, (p + 1) * LANES)
        rows = pl.ds(t0, c)
        lw = lw_ref[0, rows, lanes]
        cum = _dot_3piece_lhs(tri, lw)
        yield
        r = r_ref[0, rows, lanes]
        k = k_ref[0, rows, lanes]
        kk = kk_ref[0, rows, lanes]
        b = b_ref[0, rows, lanes]
        cum_end = cum[c - 1:c, :]
        e_neg = jnp.exp(-cum)
        e_end = jnp.exp(cum_end - cum)
        a_s = stack(-kk * jnp.exp(cum - lw))
        r_s = stack(r * jnp.exp(cum))
        b_s = stack(b * e_neg)
        k_s = stack(k * e_neg)
        bend_s = stack(b * e_end)
        kend_s = stack(k * e_end)
        v_s = stack(v_ref[0, rows, lanes])
        gram = _dot_nt(jnp.concatenate([a_s, r_s], axis=0),
                       jnp.concatenate([b_s, k_s], axis=0))
        kv = _dot_tn(v_s, kend_s)
        yield
        l_ab = jnp.where(strict, gram[:c2, :c2], 0.0)
        l_ak = jnp.where(strict, gram[:c2, c2:], 0.0)
        l_rb = jnp.where(incl, gram[c2:, :c2], 0.0)
        l_rk = jnp.where(incl, gram[c2:, c2:], 0.0)

        x = jnp.where(inv_blk, l_ab, 0.0)
        off = jnp.where(inv_blk, 0.0, l_ab)
        dinv = eye + x
        w1 = _dot(l_ak, v_s)
        y1 = _dot(l_rk, v_s)
        for _ in range(int(math.log2(SCAN_INV_BLOCK)) - 1):
            x = _dot(x, x)
            yield
            dinv = dinv + _dot(dinv, x)
            yield
        f = _dot(dinv, off)
        yield
        f2 = _dot(f, f)
        yield
        gm = eye + f
        gm = gm + _dot(gm, f2)
        yield
        t_inv = _dot(gm, dinv)
        yield
        ta = _dot(t_inv, jnp.concatenate([a_s, w1], axis=1))
        yield
        a_hat = ta[:, :LANES]
        v_hat = ta[:, LANES:]
        state = s_ref[p]
        m1 = _dot_nt(jnp.concatenate([a_hat, r_s], axis=0), state)
        yield
        u = m1[:c2] + v_hat
        y = m1[c2:] + y1 + _dot(l_rb, u)
        s_ref[p] = state * jnp.exp(cum_end) + kv + _dot_tn(u, bend_s)
        y_ref[0, rows, lanes] = y[:c] + y[c:]

    def chunk(ci, carry):
        t0 = pl.multiple_of(ci * c, c)
        for _ in itertools.zip_longest(*[pair_chunk(t0, p) for p in range(pairs)]):
            pass
        return carry

    lax.fori_loop(0, tb // c, chunk, 0)


def _rwkv_scan(r, lw, k, v, kk, b, tb):
    batch, seq, w = r.shape
    pairs = w // LANES
    spec = pl.BlockSpec((1, tb, w), lambda bi, t: (bi, t, 0))
    kern = lambda *refs: _scan_kernel(*refs, tb=tb, pairs=pairs)
    return pl.pallas_call(
        kern,
        grid=(batch, seq // tb),
        in_specs=[spec] * 6,
        out_specs=spec,
        out_shape=jax.ShapeDtypeStruct((batch, seq, w), F32),
        scratch_shapes=[pltpu.VMEM((pairs, LANES, LANES), F32)],
        compiler_params=_params(("parallel", "arbitrary")),
        name="rwkv_scan",
    )(r, lw, k, v, kk, b)


def _out_proj_kernel(x_ref, attn_ref, y_ref, bonus_ref, g_ref, lng, lnb, bd_ref, w_ref, n2g,
                     h_ref, hn_ref, hnt_ref):
    bd = bd_ref[...]
    y = y_ref[...]
    inv_n = 1.0 / RWKV_HEAD_DIM
    d = y - _dot(y, bd) * inv_n
    var = _dot(d * d, bd) * inv_n
    yn = d * lax.rsqrt(var + RWKV_GN_EPS) * lng[...] + lnb[...]
    rw = (yn + bonus_ref[...]) * g_ref[...]
    h = (x_ref[...] + _dot(attn_ref[...], w_ref[:ATTN_WIDTH, :])
         + _dot(rw, w_ref[ATTN_WIDTH:, :]))
    h_ref[...] = h
    inv = lax.rsqrt(jnp.mean(h * h, axis=-1, keepdims=True) + RMS_EPS)
    hn = h * inv * n2g[...]
    hn_ref[...] = hn.astype(BF16)
    hnt_ref[...] = hn.T.astype(BF16)


def _out_proj(x, attn, y, bonus, g, lng, lnb, bd, w_out, n2g, tm):
    m = x.shape[0]
    row = lambda width: pl.BlockSpec((tm, width), lambda i: (i, 0))
    full = lambda a: pl.BlockSpec(a.shape, lambda i: (0,) * a.ndim)
    return pl.pallas_call(
        _out_proj_kernel,
        grid=(m // tm,),
        in_specs=[row(D_MODEL), row(ATTN_WIDTH), row(RWKV_WIDTH), row(RWKV_WIDTH), row(RWKV_WIDTH),
                  full(lng), full(lnb), full(bd), full(w_out), full(n2g)],
        out_specs=[row(D_MODEL), row(D_MODEL), pl.BlockSpec((D_MODEL, tm), lambda i: (0, i))],
        out_shape=[jax.ShapeDtypeStruct((m, D_MODEL), F32),
                   jax.ShapeDtypeStruct((m, D_MODEL), BF16),
                   jax.ShapeDtypeStruct((D_MODEL, m), BF16)],
        compiler_params=_params(("parallel",)),
        name="out_proj",
    )(x, attn, y, bonus, g, lng, lnb, bd, w_out, n2g)


def _peer_q_kernel(hn_ref, wq_ref, keys_ref, st_ref):
    q = jnp.dot(hn_ref[...], wq_ref[...], preferred_element_type=F32)
    for hp in range(2 * PEER_HEADS):
        qs = q[:, hp * PEER_HALF:(hp + 1) * PEER_HALF]
        st_ref[hp] = _dot_nt(keys_ref[hp], qs)


def _peer_q(hn, wq, keys, tm):
    m = hn.shape[0]
    return pl.pallas_call(
        _peer_q_kernel,
        grid=(m // tm,),
        in_specs=[pl.BlockSpec((tm, D_MODEL), lambda i: (i, 0)),
                  pl.BlockSpec(wq.shape, lambda i: (0, 0)),
                  pl.BlockSpec(keys.shape, lambda i: (0, 0, 0))],
        out_specs=pl.BlockSpec((2 * PEER_HEADS, PEER_NKEYS, tm), lambda i: (0, 0, i)),
        out_shape=jax.ShapeDtypeStruct((2 * PEER_HEADS, PEER_NKEYS, m), F32),
        compiler_params=_params(("parallel",)),
        name="peer_q",
    )(hn, wq, keys)


def _top_desc(x, k, exact_ties, want_rank=True):
    n = x.shape[0]
    rows = lax.broadcasted_iota(jnp.int32, x.shape, 0)
    rank = jnp.full(x.shape, float(k), F32) if want_rank else None
    outs = []
    for i in range(k):
        m = jnp.max(x, axis=0, keepdims=True)
        outs.append(m)
        pick = x == m
        if exact_ties:
            pick = rows == jnp.min(jnp.where(pick, rows, n), axis=0, keepdims=True)
        x = jnp.where(pick, -jnp.inf, x)
        if want_rank:
            rank = jnp.where(pick, float(i), rank)
    gone = jnp.sum(jnp.where(x == -jnp.inf, 1.0, 0.0), axis=0, keepdims=True)
    return jnp.concatenate(outs, axis=0), rank, gone


def _peer_topk_kernel(s_ref, cnt1_ref, e1_ref, rank2_ref, e2_ref, *, n1):
    k = PEER_TOPK
    widths = [k // (a + 1) for a in range(k)]
    pad = -sum(widths) % 8

    def run(exact_ties):
        s1 = s_ref[0]
        s2 = s_ref[1]
        sv1, rank1, gone1 = _top_desc(s1, k, exact_ties, want_rank=exact_ties)
        sv2, rank2, gone2 = _top_desc(s2, k, exact_ties)
        cand = jnp.concatenate([sv1[a:a + 1] + sv2[:widths[a]] for a in range(k)]
                               + [jnp.full((pad, s1.shape[1]), -jnp.inf, F32)], axis=0)
        best, pos, gone3 = _top_desc(cand, k, exact_ties)
        chosen = jnp.where(pos < float(k), 1.0, 0.0)
        z = jnp.sum(jnp.exp(best - best[0:1]), axis=0, keepdims=True)
        cnt1 = jnp.zeros(s1.shape, F32)
        row = 0
        for a in range(k):
            cnt_a = jnp.sum(chosen[row:row + widths[a]], axis=0, keepdims=True)
            at_a = rank1 == float(a) if exact_ties else s1 == sv1[a:a + 1]
            cnt1 = jnp.where(at_a, cnt_a, cnt1)
            row += widths[a]
        e1 = jnp.exp(s1 - sv1[0:1]) / z
        for t in range(PEER_NKEYS // n1):
            cnt1_ref[0, t] = cnt1[t * n1:(t + 1) * n1]
            e1_ref[0, t] = e1[t * n1:(t + 1) * n1]
        rank2_ref[0] = rank2
        e2_ref[0] = jnp.exp(s2 - sv2[0:1])
        return jnp.max(jnp.maximum(jnp.maximum(gone1, gone2), gone3 - float(pad)))

    most_gone = run(exact_ties=False)

    @pl.when(most_gone > float(k))
    def _():
        run(exact_ties=True)


def _peer_topk(st, tl, n1):
    m = st.shape[2]
    n_tiles = PEER_NKEYS // n1
    spec = pl.BlockSpec((1, PEER_NKEYS, tl), lambda h, i: (h, 0, i))
    shape = jax.ShapeDtypeStruct((PEER_HEADS, PEER_NKEYS, m), F32)
    tile_spec = pl.BlockSpec((1, n_tiles, n1, tl), lambda h, i: (h, 0, 0, i))
    tile_shape = jax.ShapeDtypeStruct((PEER_HEADS, n_tiles, n1, m), F32)
    kern = lambda *refs: _peer_topk_kernel(*refs, n1=n1)
    return pl.pallas_call(
        kern,
        grid=(PEER_HEADS, m // tl),
        in_specs=[pl.BlockSpec((2, PEER_NKEYS, tl), lambda h, i: (h, 0, i))],
        out_specs=[tile_spec, tile_spec, spec, spec],
        out_shape=[tile_shape, tile_shape, shape, shape],
        compiler_params=_params(("parallel", "parallel")),
        name="peer_topk",
    )(st)


def _peer_dense_kernel(u_ref, hnt_ref, vt_ref, rank2_ref, e2_ref, cnta_ref, cntb_ref,
                       e1a_ref, e1b_ref, h_ref, fg_ref, out_ref, act_ref, hpre_ref, yt_ref, *, n1):
    step = pl.program_id(1)
    last_step = pl.num_programs(1) - 1
    et = hpre_ref.shape[1]
    row_refs = ((cnta_ref, e1a_ref), (cntb_ref, e1b_ref))

    def gate_tile(half, jt, ib):
        prev = 1 - half
        cnt_ref, e1r_ref = row_refs[half]
        ls = slice(jt * LANES, (jt + 1) * LANES)
        rs = slice(ib * DENSE_ROWS, (ib + 1) * DENSE_ROWS)
        gates = [jnp.zeros((DENSE_ROWS, LANES), F32) for _ in range(n1)]
        for h in range(PEER_HEADS):
            r2 = rank2_ref[h, rs, ls]
            e2 = e2_ref[h, rs, ls]
            for a in range(n1):
                hit = r2 < cnt_ref[h, 0, a:a + 1, ls]
                gates[a] = gates[a] + jnp.where(hit, e2, 0.0) * e1r_ref[h, 0, a:a + 1, ls]
        for a in range(n1):
            ers = slice(a * PEER_NKEYS + ib * DENSE_ROWS, a * PEER_NKEYS + (ib + 1) * DENSE_ROWS)
            x = hpre_ref[prev, ers, ls]
            gelu = 0.5 * x * (1.0 + lax.erf(x * (0.5 ** 0.5)))
            act_ref[prev, ers, ls] = gelu * gates[a]

    tt = hpre_ref.shape[2]
    n_ib = PEER_NKEYS // DENSE_ROWS

    def run(first, last):
        for half in range(2):
            cur = half
            hs = slice(half * et, (half + 1) * et)
            gated = not (first and half == 0) and not (last and half == 1)
            for piece in range(tt // MXU_COLS):
                cs = slice(piece * MXU_COLS, (piece + 1) * MXU_COLS)
                tiles = [(jt, ib)
                         for jt in range(piece * MXU_COLS // LANES, (piece + 1) * MXU_COLS // LANES)
                         for ib in range(n_ib)]
                if not last:
                    hpre_ref[cur, :, cs] = jnp.dot(u_ref[hs, :], hnt_ref[:, cs],
                                                   preferred_element_type=F32)
                per_down = len(tiles) * DOWN_ROWS // D_MODEL
                for i, tile in enumerate(tiles):
                    if not first and i % per_down == per_down // 2:
                        ms = slice(i // per_down * DOWN_ROWS, (i // per_down + 1) * DOWN_ROWS)
                        yt_ref[ms, cs] += jnp.dot(vt_ref[ms, hs],
                                                  act_ref[cur, :, cs].astype(BF16),
                                                  preferred_element_type=F32)
                    if gated:
                        gate_tile(half, *tile)

    @pl.when(step == 0)
    def _():
        yt_ref[...] = jnp.zeros_like(yt_ref)
        run(first=True, last=False)

    @pl.when(jnp.logical_and(step > 0, step < last_step))
    def _():
        run(first=False, last=False)

    @pl.when(step == last_step)
    def _():
        run(first=False, last=True)
        h = h_ref[...] + yt_ref[...].T
        inv = lax.rsqrt(jnp.mean(h * h, axis=-1, keepdims=True) + RMS_EPS)
        out_ref[...] = h * inv * fg_ref[...]


def _peer_dense(u, hnt, vt, cntr, e1r, rank2, e2, h, final_g, tt, n1):
    m = hnt.shape[1]
    et = n1 * PEER_NKEYS
    n_steps = PEER_NKEYS // (2 * n1)
    n_tiles = 2 * n_steps
    tok3 = lambda a: pl.BlockSpec((a.shape[0], a.shape[1], tt), lambda i, e: (0, 0, i))
    rows = lambda half: pl.BlockSpec(
        (PEER_HEADS, 1, n1, tt),
        lambda i, e: (0, jnp.clip(2 * e + half - 1, 0, n_tiles - 1), 0, i))
    kern = lambda *refs: _peer_dense_kernel(*refs, n1=n1)
    return pl.pallas_call(
        kern,
        grid=(m // tt, n_steps + 1),
        in_specs=[pl.BlockSpec((2 * et, D_MODEL), lambda i, e: (jnp.minimum(e, n_steps - 1), 0)),
                  pl.BlockSpec((D_MODEL, tt), lambda i, e: (0, i)),
                  pl.BlockSpec((D_MODEL, 2 * et), lambda i, e: (0, jnp.maximum(e - 1, 0))),
                  tok3(rank2), tok3(e2), rows(0), rows(1), rows(0), rows(1),
                  pl.BlockSpec((tt, D_MODEL), lambda i, e: (i, 0)),
                  pl.BlockSpec((1, D_MODEL), lambda i, e: (0, 0))],
        out_specs=pl.BlockSpec((tt, D_MODEL), lambda i, e: (i, 0)),
        out_shape=jax.ShapeDtypeStruct((m, D_MODEL), F32),
        scratch_shapes=[pltpu.VMEM((2, et, tt), F32), pltpu.VMEM((2, et, tt), F32),
                        pltpu.VMEM((D_MODEL, tt), F32)],
        compiler_params=_params(("parallel", "arbitrary"), DENSE_VMEM_LIMIT),
        name="peer_dense",
    )(u, hnt, vt, rank2, e2, cntr, cntr, e1r, e1r, h, final_g)


def _transpose_cast_kernel(x_ref, o_ref):
    o_ref[...] = x_ref[...].T.astype(o_ref.dtype)


def _transpose_cast(x, tr):
    rows, cols = x.shape
    return pl.pallas_call(
        _transpose_cast_kernel,
        grid=(rows // tr,),
        in_specs=[pl.BlockSpec((tr, cols), lambda i: (i, 0))],
        out_specs=pl.BlockSpec((cols, tr), lambda i: (0, i)),
        out_shape=jax.ShapeDtypeStruct((cols, rows), BF16),
        compiler_params=_params(("parallel",)),
        name="transpose_cast",
    )(x)


def _row(a):
    return a.reshape(1, -1).astype(F32)


def _pad_rows(w, start, total):
    return jnp.zeros((total, w.shape[1]), BF16).at[start:start + w.shape[0]].set(w.astype(BF16))


def kernel(x, norm1_g, w_in, rwkv_mu, rwkv_w0, rwkv_w2, rwkv_a0, rwkv_a2, rwkv_g2, rwkv_k_k, rwkv_k_a, rwkv_r_k, rwkv_ln_g, rwkv_ln_b, w_out, norm2_g, peer_wq, peer_sub_keys, peer_u, peer_v, final_g):
    batch, seq, d = x.shape
    tokens = batch * seq
    w = RWKV_WIDTH
    xt = x.reshape(tokens, d)

    w_in_p = jnp.pad(w_in[0], ((0, 0), (0, IN_COLS_PAD - IN_COLS))).astype(BF16)
    qkv, proj = _norm_matmul(xt, _row(norm1_g[0]), w_in_p, 3 * ATTN_WIDTH, tm=1024, tn=512)
    proj = proj.reshape(batch, seq, IN_COLS_PAD - 3 * ATTN_WIDTH)

    attn = _moba(qkv.reshape(batch, seq, 3 * ATTN_WIDTH), batch, seq, heads=8)

    mu = rwkv_mu[0]
    mu_l = jnp.pad(mu[3 * w:], (0, LORA_PAD - LORA_COLS))
    hid = lax.broadcasted_iota(jnp.int32, (w, w), 0) // RWKV_HEAD_DIM
    bd = (hid == hid.T).astype(BF16)
    head_gather = (hid[:, :LANES] == lax.broadcasted_iota(jnp.int32, (w, LANES), 1)).astype(BF16)
    prep_w = [_row(mu[:w]), _row(mu[w:2 * w]), _row(mu[2 * w:3 * w]), _row(mu_l),
              _row(rwkv_w0[0]), _pad_rows(rwkv_w2[0], 0, LORA_PAD),
              _row(rwkv_a0[0]), _pad_rows(rwkv_a2[0], DECAY_LORA, LORA_PAD),
              _pad_rows(rwkv_g2[0], DECAY_LORA + AAA_LORA, LORA_PAD),
              _row(rwkv_k_k[0]), _row(rwkv_k_a[0]), _row(rwkv_r_k[0]), head_gather, head_gather.T]
    r, lw, k2, v, kk, b, g, bonus = _rwkv_prep(proj, batch, seq, prep_w, tt=256)
    y = _rwkv_scan(r, lw, k2, v, kk, b, tb=256)

    flat = lambda a: a.reshape(tokens, a.shape[-1])
    h, hn, hnt = _out_proj(xt, flat(attn), flat(y), flat(bonus), flat(g), _row(rwkv_ln_g[0]),
                      _row(rwkv_ln_b[0]), bd, w_out[0].astype(BF16), _row(norm2_g[0]), tm=256)

    keys = peer_sub_keys[0].reshape(2 * PEER_HEADS, PEER_NKEYS, PEER_HALF).astype(BF16)
    st = _peer_q(hn, peer_wq[0].astype(BF16), keys, tm=512)
    cnt1, e1, rank2, e2 = _peer_topk(st, tl=512, n1=4)
    out = _peer_dense(peer_u[0].astype(BF16), hnt, _transpose_cast(peer_v[0], tr=512),
                      cnt1, e1, rank2, e2, h, final_g.reshape(1, d), tt=512, n1=4)
    return out.reshape(batch, seq, d)
```

```python
import itertools
import math

import jax
import jax.numpy as jnp
from jax import lax
from jax.experimental import pallas as pl
from jax.experimental.pallas import tpu as pltpu

F32 = jnp.float32
BF16 = jnp.bfloat16

D_MODEL = 2048
ATTN_HEADS = 8
ATTN_HEAD_DIM = 128
ATTN_WIDTH = ATTN_HEADS * ATTN_HEAD_DIM
MOBA_BLOCK = 256
MOBA_TOPK = 3

RWKV_HEAD_DIM = 64
RWKV_WIDTH = D_MODEL - ATTN_WIDTH
RWKV_HEADS = RWKV_WIDTH // RWKV_HEAD_DIM
DECAY_LORA = 96
AAA_LORA = 96
GATE_LORA = 256
LORA_COLS = DECAY_LORA + AAA_LORA + GATE_LORA
LORA_PAD = 512
RWKV_GN_EPS = RWKV_HEAD_DIM * 1e-5
IN_COLS = 3 * ATTN_WIDTH + 3 * RWKV_WIDTH + LORA_COLS
IN_COLS_PAD = 3 * ATTN_WIDTH + 3 * RWKV_WIDTH + LORA_PAD

PEER_HEADS = 8
PEER_NKEYS = 128
PEER_HALF = 128
PEER_TOPK = 16

RMS_EPS = 1e-6
NEG = -1e30

LANES = 128
SCAN_CHUNK = 64
SCAN_INV_BLOCK = 16
DOWN_ROWS = 2048
MXU_COLS = 256
DENSE_ROWS = 32
VMEM_LIMIT = 56 * 1024 * 1024
DENSE_VMEM_LIMIT = 58 * 1024 * 1024

_NT = (((1,), (1,)), ((), ()))
_TN = (((0,), (0,)), ((), ()))


def _params(sem, vmem_limit=VMEM_LIMIT):
    return pltpu.CompilerParams(dimension_semantics=sem, vmem_limit_bytes=vmem_limit)


def _dot(a, b):
    return jnp.dot(a.astype(BF16), b.astype(BF16), preferred_element_type=F32)


def _dot_nt(a, b):
    return lax.dot_general(a.astype(BF16), b.astype(BF16), _NT, preferred_element_type=F32)


def _dot_tn(a, b):
    return lax.dot_general(a.astype(BF16), b.astype(BF16), _TN, preferred_element_type=F32)


def _split3(x):
    hi = x.astype(BF16)
    r1 = x - hi.astype(F32)
    mid = r1.astype(BF16)
    lo = (r1 - mid.astype(F32)).astype(BF16)
    return hi, mid, lo


def _dot_2piece_rhs(x, ones_bf16):
    hi = x.astype(BF16)
    lo = (x - hi.astype(F32)).astype(BF16)
    d = lambda p: jnp.dot(p, ones_bf16, preferred_element_type=F32)
    return d(hi) + d(lo)


def _head_sums(x, gather, scatter):
    return _dot_2piece_rhs(_dot_2piece_rhs(x, gather), scatter)


def _dot_3piece_lhs(ones_bf16, x):
    hi, mid, lo = _split3(x)
    d = lambda p: jnp.dot(ones_bf16, p, preferred_element_type=F32)
    return d(hi) + d(mid) + d(lo)


def _norm_matmul_kernel(x_ref, g_ref, w_ref, attn_ref, rwkv_ref, xn_ref, *, attn_tiles):
    j = pl.program_id(1)

    @pl.when(j == 0)
    def _():
        x = x_ref[...]
        inv = lax.rsqrt(jnp.mean(x * x, axis=-1, keepdims=True) + RMS_EPS)
        xn_ref[...] = (x * inv * g_ref[...]).astype(BF16)

    acc = jnp.dot(xn_ref[...], w_ref[...], preferred_element_type=F32)

    @pl.when(j < attn_tiles)
    def _():
        attn_ref[...] = acc.astype(BF16)

    @pl.when(j >= attn_tiles)
    def _():
        rwkv_ref[...] = acc


def _norm_matmul(x, g, w, n_attn, tm, tn):
    m, k = x.shape
    n = w.shape[1]
    attn_tiles = n_attn // tn
    kern = lambda *refs: _norm_matmul_kernel(*refs, attn_tiles=attn_tiles)
    return pl.pallas_call(
        kern,
        grid=(m // tm, n // tn),
        in_specs=[pl.BlockSpec((tm, k), lambda i, j: (i, 0)),
                  pl.BlockSpec((1, k), lambda i, j: (0, 0)),
                  pl.BlockSpec((k, tn), lambda i, j: (0, j))],
        out_specs=[pl.BlockSpec((tm, tn), lambda i, j: (i, jnp.minimum(j, attn_tiles - 1))),
                   pl.BlockSpec((tm, tn), lambda i, j: (i, jnp.maximum(j - attn_tiles, 0)))],
        out_shape=[jax.ShapeDtypeStruct((m, n_attn), BF16),
                   jax.ShapeDtypeStruct((m, n - n_attn), F32)],
        scratch_shapes=[pltpu.VMEM((tm, k), BF16)],
        compiler_params=_params(("parallel", "arbitrary")),
        name="in_proj",
    )(x, g, w)


def _moba_kernel(q_ref, k_ref, v_ref, o_ref, kmean_ref, vt_ref, *, nb, heads):
    blk_len = MOBA_BLOCK
    hd = ATTN_HEAD_DIM
    hg = pl.program_id(1)
    qi = pl.program_id(2)
    hrange = range(heads)

    @pl.when(qi == 0)
    def _():
        for hh in hrange:
            hl = slice(hh * hd, (hh + 1) * hd)
            for j in range(nb):
                rows = slice(j * blk_len, (j + 1) * blk_len)
                kj = k_ref[0, rows, hl].astype(F32)
                kmean_ref[hh, j:j + 1, :] = jnp.mean(kj, axis=0, keepdims=True)
                vt_ref[hh, :, rows] = v_ref[0, rows, hl].astype(F32).T.astype(BF16)

    log2e = math.log2(math.e)
    scale2 = (hd ** -0.5) * log2e
    kq = (lax.broadcasted_iota(jnp.int32, (blk_len, blk_len), 1)
          - lax.broadcasted_iota(jnp.int32, (blk_len, blk_len), 0))
    kqf = kq.astype(F32)
    blk = lax.broadcasted_iota(jnp.int32, (nb, blk_len), 0)
    past = blk < qi

    def select(hh):
        q = q_ref[0, :, hh * hd:(hh + 1) * hd].astype(F32)
        gate = lax.dot_general(kmean_ref[hh], q, _NT, precision=lax.Precision.HIGHEST,
                               preferred_element_type=F32)
        g = jnp.where(past, gate, NEG)
        selw = jnp.zeros(gate.shape, F32)
        for _ in range(min(MOBA_TOPK, nb)):
            m = jnp.max(g, axis=0, keepdims=True)
            first = jnp.min(jnp.where(g == m, blk, nb), axis=0, keepdims=True)
            pick = blk == first
            selw = jnp.where(pick, 1.0, selw)
            g = jnp.where(pick, -jnp.inf, g)
        return jnp.where(past, selw, 0.0), (q * scale2).astype(BF16)

    sel_q = [select(hh) for hh in hrange]
    selw = [x[0] for x in sel_q]
    qb = [x[1] for x in sel_q]
    slope2 = [jnp.exp2(jnp.zeros((1, 1), F32) - (hg * heads + hh + 1).astype(F32)) * log2e
              for hh in hrange]
    bias0 = [slope2[hh] * kqf for hh in hrange]

    def scores(j):
        r0 = pl.multiple_of(j * blk_len, blk_len)
        return [lax.dot_general(k_ref[0, pl.ds(r0, blk_len), hh * hd:(hh + 1) * hd], qb[hh], _NT,
                                preferred_element_type=F32) - bias0[hh] for hh in hrange]

    def weighted_values(j, p):
        r0 = pl.multiple_of(j * blk_len, blk_len)
        return [jnp.dot(vt_ref[hh, :, pl.ds(r0, blk_len)], p[hh].astype(BF16),
                        preferred_element_type=F32) for hh in hrange]

    s = [jnp.where(kq >= 0, x, NEG) for x in scores(qi)]
    m = [jnp.max(x, axis=0, keepdims=True) for x in s]
    p = [jnp.exp2(s[hh] - m[hh]) for hh in hrange]
    l = [jnp.sum(x, axis=0, keepdims=True) for x in p]
    acc = weighted_values(qi, p)

    def body(j, carry):
        m, l, acc = carry
        off = ((qi - j) * blk_len).astype(F32)
        s = scores(j)
        selj = [jnp.max(jnp.where(blk == j, selw[hh], 0.0), axis=0, keepdims=True) > 0.0
                for hh in hrange]
        shift = [slope2[hh] * off for hh in hrange]
        top = [jnp.max(s[hh], axis=0, keepdims=True) - shift[hh] for hh in hrange]
        m_new = [jnp.maximum(m[hh], jnp.where(selj[hh], top[hh], NEG)) for hh in hrange]
        alpha = [jnp.exp2(m[hh] - m_new[hh]) for hh in hrange]
        p = [jnp.exp2(s[hh] - jnp.where(selj[hh], m_new[hh] + shift[hh], jnp.inf)) for hh in hrange]
        l = [alpha[hh] * l[hh] + jnp.sum(p[hh], axis=0, keepdims=True) for hh in hrange]
        pv = weighted_values(j, p)
        acc = [alpha[hh] * acc[hh] + pv[hh] for hh in hrange]
        return m_new, l, acc

    m, l, acc = lax.fori_loop(0, qi, body, (m, l, acc))
    for hh in hrange:
        o_ref[0, :, hh * hd:(hh + 1) * hd] = (acc[hh] / l[hh]).T


def _moba(qkv, batch, seq, heads):
    nb = seq // MOBA_BLOCK
    hd = ATTN_HEAD_DIM
    hw = heads * hd
    groups = ATTN_HEADS // heads
    kern = lambda *refs: _moba_kernel(*refs, nb=nb, heads=heads)
    return pl.pallas_call(
        kern,
        grid=(batch, groups, nb),
        in_specs=[pl.BlockSpec((1, MOBA_BLOCK, hw), lambda b, h, i: (b, i, h)),
                  pl.BlockSpec((1, seq, hw), lambda b, h, i: (b, 0, groups + h)),
                  pl.BlockSpec((1, seq, hw), lambda b, h, i: (b, 0, 2 * groups + h))],
        out_specs=pl.BlockSpec((1, MOBA_BLOCK, hw), lambda b, h, i: (b, i, h)),
        out_shape=jax.ShapeDtypeStruct((batch, seq, ATTN_WIDTH), F32),
        scratch_shapes=[pltpu.VMEM((heads, nb, hd), F32),
                        pltpu.VMEM((heads, hd, seq), BF16)],
        compiler_params=_params(("parallel", "parallel", "arbitrary")),
        name="moba",
    )(qkv, qkv, qkv)


def _rwkv_prep_kernel(r_ref, k_ref, v_ref, l_ref, rp_ref, kp_ref, vp_ref, lp_ref,
                      mu_r, mu_k, mu_v, mu_l, w0, w2p, a0, a2p, g2p, kk_w, ka_w, rk_w, hg_ref, hs_ref,
                      r_o, lw_o, k_o, v_o, kk_o, b_o, g_o, bonus_o):
    first = pl.program_id(1) == 0

    def shift(cur_ref, prev_ref, mu_ref):
        cur = cur_ref[0]
        prev_last = jnp.where(first, 0.0, prev_ref[0, 7:8, :])
        rows = lax.broadcasted_iota(jnp.int32, cur.shape, 0)
        prev = jnp.where(rows == 0, prev_last, pltpu.roll(cur, 1, axis=0))
        return cur + (prev - cur) * mu_ref[...]

    r = shift(r_ref, rp_ref, mu_r)
    k = shift(k_ref, kp_ref, mu_k)
    v = shift(v_ref, vp_ref, mu_v)
    lo = shift(l_ref, lp_ref, mu_l)

    gather = hg_ref[...]
    scatter = hs_ref[...]
    z = -(w0[...] + _dot(jnp.tanh(lo), w2p[...]))
    softplus = jnp.maximum(z, 0.0) + jnp.log1p(jnp.exp(-jnp.abs(z)))
    lw = -jnp.exp(-softplus - 0.5)
    a = jax.nn.sigmoid(a0[...] + _dot(lo, a2p[...]))
    g = _dot(jax.nn.sigmoid(lo), g2p[...])
    kk = k * kk_w[...]
    kk = kk / jnp.maximum(jnp.sqrt(_head_sums(kk * kk, gather, scatter)), 1e-12)
    k2 = k * (1.0 + (a - 1.0) * ka_w[...])
    r_o[0] = r.astype(BF16)
    lw_o[0] = lw
    k_o[0] = k2.astype(BF16)
    v_o[0] = v.astype(BF16)
    kk_o[0] = kk.astype(BF16)
    b_o[0] = (kk * a).astype(BF16)
    g_o[0] = g
    bonus_o[0] = _head_sums(r * k2 * rk_w[...], gather, scatter) * v


def _rwkv_prep(proj, batch, seq, weights, tt):
    w = RWKV_WIDTH
    c0 = 0
    lc = 3 * w // LORA_PAD
    pb = tt // 8
    cur = lambda width, col: pl.BlockSpec((1, tt, width), lambda b, i: (b, i, col))
    prev = lambda width, col: pl.BlockSpec(
        (1, 8, width), lambda b, i: (b, jnp.maximum(i * pb - 1, 0), col))
    full = lambda a: pl.BlockSpec(a.shape, lambda b, i: (0,) * a.ndim)
    out = pl.BlockSpec((1, tt, w), lambda b, i: (b, i, 0))
    return pl.pallas_call(
        _rwkv_prep_kernel,
        grid=(batch, seq // tt),
        in_specs=[cur(w, c0), cur(w, c0 + 1), cur(w, c0 + 2), cur(LORA_PAD, lc),
                  prev(w, c0), prev(w, c0 + 1), prev(w, c0 + 2), prev(LORA_PAD, lc)]
                 + [full(a) for a in weights],
        out_specs=[out] * 8,
        out_shape=[jax.ShapeDtypeStruct((batch, seq, w), dt)
                   for dt in (BF16, F32, BF16, BF16, BF16, BF16, F32, F32)],
        compiler_params=_params(("parallel", "arbitrary")),
        name="rwkv_prep",
    )(proj, proj, proj, proj, proj, proj, proj, proj, *weights)


def _scan_kernel(r_ref, lw_ref, k_ref, v_ref, kk_ref, b_ref, y_ref, s_ref, *, tb, pairs):
    c = SCAN_CHUNK
    c2 = 2 * c

    @pl.when(pl.program_id(1) == 0)
    def _():
        s_ref[...] = jnp.zeros_like(s_ref)

    head0 = lax.broadcasted_iota(jnp.int32, (c, LANES), 1) < RWKV_HEAD_DIM
    row = lax.broadcasted_iota(jnp.int32, (c2, c2), 0)
    col = lax.broadcasted_iota(jnp.int32, (c2, c2), 1)
    same_head = (row // c) == (col // c)
    strict = jnp.logical_and(same_head, col < row)
    incl = jnp.logical_and(same_head, col <= row)
    inv_blk = (row // SCAN_INV_BLOCK) == (col // SCAN_INV_BLOCK)
    eye = jnp.where(row == col, 1.0, 0.0).astype(F32)
    tri = jnp.where(lax.broadcasted_iota(jnp.int32, (c, c), 0)
                    >= lax.broadcasted_iota(jnp.int32, (c, c), 1), 1.0, 0.0).astype(BF16)

    def stack(x):
        return jnp.concatenate([jnp.where(head0, x, 0.0), jnp.where(head0, 0.0, x)], axis=0)

    def pair_chunk(t0, p):
        lanes = slice(p * LANES, (p + 1) * LANES)
        rows = pl.ds(t0, c)
        lw = lw_ref[0, rows, lanes]
        cum = _dot_3piece_lhs(tri, lw)
        yield
        r = r_ref[0, rows, lanes].astype(F32)
        k = k_ref[0, rows, lanes].astype(F32)
        kk = kk_ref[0, rows, lanes].astype(F32)
        b = b_ref[0, rows, lanes].astype(F32)
        cum_end = cum[c - 1:c, :]
        e_neg = jnp.exp(-cum)
        e_end = jnp.exp(cum_end - cum)
        a_s = stack(-kk * jnp.exp(cum - lw))
        r_s = stack(r * jnp.exp(cum))
        b_s = stack(b * e_neg)
        k_s = stack(k * e_neg)
        bend_s = stack(b * e_end)
        kend_s = stack(k * e_end)
        v_s = stack(v_ref[0, rows, lanes].astype(F32))
        gram = _dot_nt(jnp.concatenate([a_s, r_s], axis=0),
                       jnp.concatenate([b_s, k_s], axis=0))
        kv = _dot_tn(v_s, kend_s)
        yield
        l_ab = jnp.where(strict, gram[:c2, :c2], 0.0)
        l_ak = jnp.where(strict, gram[:c2, c2:], 0.0)
        l_rb = jnp.where(incl, gram[c2:, :c2], 0.0)
        l_rk = jnp.where(incl, gram[c2:, c2:], 0.0)

        x = jnp.where(inv_blk, l_ab, 0.0)
        off = jnp.where(inv_blk, 0.0, l_ab)
        dinv = eye + x
        w1 = _dot(l_ak, v_s)
        y1 = _dot(l_rk, v_s)
        for _ in range(int(math.log2(SCAN_INV_BLOCK)) - 1):
            x = _dot(x, x)
            yield
            dinv = dinv + _dot(dinv, x)
            yield
        f = _dot(dinv, off)
        yield
        f2 = _dot(f, f)
        yield
        gm = eye + f
        gm = gm + _dot(gm, f2)
        yield
        t_inv = _dot(gm, dinv)
        yield
        ta = _dot(t_inv, jnp.concatenate([a_s, w1], axis=1))
        yield
        a_hat = ta[:, :LANES]
        v_hat = ta[:, LANES:]
        state = s_ref[p]
        m1 = _dot_nt(jnp.concatenate([a_hat, r_s], axis=0), state)
        yield
        u = m1[:c2] + v_hat
        y = m1[c2:] + y1 + _dot(l_rb, u)
        s_ref[p] = state * jnp.exp(cum_end) + kv + _dot_tn(u, bend_s)
        y_ref[0, rows, lanes] = y[:c] + y[c:]

    def chunk(ci, carry):
        t0 = pl.multiple_of(ci * c, c)
        for _ in itertools.zip_longest(*[pair_chunk(t0, p) for p in range(pairs)]):
            pass
        return carry

    lax.fori_loop(0, tb // c, chunk, 0)


def _rwkv_scan(r, lw, k, v, kk, b, tb):
    batch, seq, w = r.shape
    pairs = w // LANES
    spec = pl.BlockSpec((1, tb, w), lambda bi, t: (bi, t, 0))
    kern = lambda *refs: _scan_kernel(*refs, tb=tb, pairs=pairs)
    return pl.pallas_call(
        kern,
        grid=(batch, seq // tb),
        in_specs=[spec] * 6,
        out_specs=spec,
        out_shape=jax.ShapeDtypeStruct((batch, seq, w), F32),
        scratch_shapes=[pltpu.VMEM((pairs, LANES, LANES), F32)],
        compiler_params=_params(("parallel", "arbitrary")),
        name="rwkv_scan",
    )(r, lw, k, v, kk, b)


def _out_proj_kernel(x_ref, attn_ref, y_ref, bonus_ref, g_ref, lng, lnb, bd_ref, w_ref, n2g,
                     h_ref, hn_ref, hnt_ref):
    bd = bd_ref[...]
    y = y_ref[...]
    inv_n = 1.0 / RWKV_HEAD_DIM
    d = y - _dot(y, bd) * inv_n
    var = _dot(d * d, bd) * inv_n
    yn = d * lax.rsqrt(var + RWKV_GN_EPS) * lng[...] + lnb[...]
    rw = (yn + bonus_ref[...]) * g_ref[...]
    h = (x_ref[...] + _dot(attn_ref[...], w_ref[:ATTN_WIDTH, :])
         + _dot(rw, w_ref[ATTN_WIDTH:, :]))
    h_ref[...] = h
    inv = lax.rsqrt(jnp.mean(h * h, axis=-1, keepdims=True) + RMS_EPS)
    hn = h * inv * n2g[...]
    hn_ref[...] = hn.astype(BF16)
    hnt_ref[...] = hn.T.astype(BF16)


def _out_proj(x, attn, y, bonus, g, lng, lnb, bd, w_out, n2g, tm):
    m = x.shape[0]
    row = lambda width: pl.BlockSpec((tm, width), lambda i: (i, 0))
    full = lambda a: pl.BlockSpec(a.shape, lambda i: (0,) * a.ndim)
    return pl.pallas_call(
        _out_proj_kernel,
        grid=(m // tm,),
        in_specs=[row(D_MODEL), row(ATTN_WIDTH), row(RWKV_WIDTH), row(RWKV_WIDTH), row(RWKV_WIDTH),
                  full(lng), full(lnb), full(bd), full(w_out), full(n2g)],
        out_specs=[row(D_MODEL), row(D_MODEL), pl.BlockSpec((D_MODEL, tm), lambda i: (0, i))],
        out_shape=[jax.ShapeDtypeStruct((m, D_MODEL), F32),
                   jax.ShapeDtypeStruct((m, D_MODEL), BF16),
                   jax.ShapeDtypeStruct((D_MODEL, m), BF16)],
        compiler_params=_params(("parallel",)),
        name="out_proj",
    )(x, attn, y, bonus, g, lng, lnb, bd, w_out, n2g)


def _peer_q_kernel(hn_ref, wq_ref, keys_ref, st_ref):
    q = jnp.dot(hn_ref[...], wq_ref[...], preferred_element_type=F32)
    for hp in range(2 * PEER_HEADS):
        qs = q[:, hp * PEER_HALF:(hp + 1) * PEER_HALF]
        st_ref[hp] = _dot_nt(keys_ref[hp], qs)


def _peer_q(hn, wq, keys, tm):
    m = hn.shape[0]
    return pl.pallas_call(
        _peer_q_kernel,
        grid=(m // tm,),
        in_specs=[pl.BlockSpec((tm, D_MODEL), lambda i: (i, 0)),
                  pl.BlockSpec(wq.shape, lambda i: (0, 0)),
                  pl.BlockSpec(keys.shape, lambda i: (0, 0, 0))],
        out_specs=pl.BlockSpec((2 * PEER_HEADS, PEER_NKEYS, tm), lambda i: (0, 0, i)),
        out_shape=jax.ShapeDtypeStruct((2 * PEER_HEADS, PEER_NKEYS, m), F32),
        compiler_params=_params(("parallel",)),
        name="peer_q",
    )(hn, wq, keys)


def _top_desc(x, k, exact_ties, want_rank=True):
    n = x.shape[0]
    rows = lax.broadcasted_iota(jnp.int32, x.shape, 0)
    rank = jnp.full(x.shape, float(k), F32) if want_rank else None
    outs = []
    for i in range(k):
        m = jnp.max(x, axis=0, keepdims=True)
        outs.append(m)
        pick = x == m
        if exact_ties:
            pick = rows == jnp.min(jnp.where(pick, rows, n), axis=0, keepdims=True)
        x = jnp.where(pick, -jnp.inf, x)
        if want_rank:
            rank = jnp.where(pick, float(i), rank)
    gone = jnp.sum(jnp.where(x == -jnp.inf, 1.0, 0.0), axis=0, keepdims=True)
    return jnp.concatenate(outs, axis=0), rank, gone


def _peer_topk_kernel(s_ref, cnt1_ref, e1_ref, rank2_ref, e2_ref, *, n1):
    k = PEER_TOPK
    widths = [k // (a + 1) for a in range(k)]
    pad = -sum(widths) % 8

    def run(exact_ties):
        s1 = s_ref[0]
        s2 = s_ref[1]
        sv1, rank1, gone1 = _top_desc(s1, k, exact_ties, want_rank=exact_ties)
        sv2, rank2, gone2 = _top_desc(s2, k, exact_ties)
        cand = jnp.concatenate([sv1[a:a + 1] + sv2[:widths[a]] for a in range(k)]
                               + [jnp.full((pad, s1.shape[1]), -jnp.inf, F32)], axis=0)
        best, pos, gone3 = _top_desc(cand, k, exact_ties)
        chosen = jnp.where(pos < float(k), 1.0, 0.0)
        z = jnp.sum(jnp.exp(best - best[0:1]), axis=0, keepdims=True)
        cnt1 = jnp.zeros(s1.shape, F32)
        row = 0
        for a in range(k):
            cnt_a = jnp.sum(chosen[row:row + widths[a]], axis=0, keepdims=True)
            at_a = rank1 == float(a) if exact_ties else s1 == sv1[a:a + 1]
            cnt1 = jnp.where(at_a, cnt_a, cnt1)
            row += widths[a]
        e1 = jnp.exp(s1 - sv1[0:1]) / z
        for t in range(PEER_NKEYS // n1):
            cnt1_ref[0, t] = cnt1[t * n1:(t + 1) * n1]
            e1_ref[0, t] = e1[t * n1:(t + 1) * n1]
        rank2_ref[0] = rank2
        e2_ref[0] = jnp.exp(s2 - sv2[0:1])
        return jnp.max(jnp.maximum(jnp.maximum(gone1, gone2), gone3 - float(pad)))

    most_gone = run(exact_ties=False)

    @pl.when(most_gone > float(k))
    def _():
        run(exact_ties=True)


def _peer_topk(st, tl, n1):
    m = st.shape[2]
    n_tiles = PEER_NKEYS // n1
    spec = pl.BlockSpec((1, PEER_NKEYS, tl), lambda h, i: (h, 0, i))
    shape = jax.ShapeDtypeStruct((PEER_HEADS, PEER_NKEYS, m), F32)
    tile_spec = pl.BlockSpec((1, n_tiles, n1, tl), lambda h, i: (h, 0, 0, i))
    tile_shape = jax.ShapeDtypeStruct((PEER_HEADS, n_tiles, n1, m), F32)
    kern = lambda *refs: _peer_topk_kernel(*refs, n1=n1)
    return pl.pallas_call(
        kern,
        grid=(PEER_HEADS, m // tl),
        in_specs=[pl.BlockSpec((2, PEER_NKEYS, tl), lambda h, i: (h, 0, i))],
        out_specs=[tile_spec, tile_spec, spec, spec],
        out_shape=[tile_shape, tile_shape, shape, shape],
        compiler_params=_params(("parallel", "parallel")),
        name="peer_topk",
    )(st)


def _peer_dense_kernel(u_ref, hnt_ref, vt_ref, rank2_ref, e2_ref, cnta_ref, cntb_ref,
                       e1a_ref, e1b_ref, h_ref, fg_ref, out_ref, act_ref, hpre_ref, yt_ref, *, n1):
    step = pl.program_id(1)
    last_step = pl.num_programs(1) - 1
    et = hpre_ref.shape[1]
    row_refs = ((cnta_ref, e1a_ref), (cntb_ref, e1b_ref))

    def gate_tile(half, jt, ib):
        prev = 1 - half
        cnt_ref, e1r_ref = row_refs[half]
        ls = slice(jt * LANES, (jt + 1) * LANES)
        rs = slice(ib * DENSE_ROWS, (ib + 1) * DENSE_ROWS)
        gates = [jnp.zeros((DENSE_ROWS, LANES), F32) for _ in range(n1)]
        for h in range(PEER_HEADS):
            r2 = rank2_ref[h, rs, ls]
            e2 = e2_ref[h, rs, ls]
            for a in range(n1):
                hit = r2 < cnt_ref[h, 0, a:a + 1, ls]
                gates[a] = gates[a] + jnp.where(hit, e2, 0.0) * e1r_ref[h, 0, a:a + 1, ls]
        for a in range(n1):
            ers = slice(a * PEER_NKEYS + ib * DENSE_ROWS, a * PEER_NKEYS + (ib + 1) * DENSE_ROWS)
            x = hpre_ref[prev, ers, ls]
            gelu = 0.5 * x * (1.0 + lax.erf(x * (0.5 ** 0.5)))
            act_ref[prev, ers, ls] = gelu * gates[a]

    tt = hpre_ref.shape[2]
    n_ib = PEER_NKEYS // DENSE_ROWS

    def run(first, last):
        for half in range(2):
            cur = half
            hs = slice(half * et, (half + 1) * et)
            gated = not (first and half == 0) and not (last and half == 1)
            for piece in range(tt // MXU_COLS):
                cs = slice(piece * MXU_COLS, (piece + 1) * MXU_COLS)
                tiles = [(jt, ib)
                         for jt in range(piece * MXU_COLS // LANES, (piece + 1) * MXU_COLS // LANES)
                         for ib in range(n_ib)]
                if not last:
                    hpre_ref[cur, :, cs] = jnp.dot(u_ref[hs, :], hnt_ref[:, cs],
                                                   preferred_element_type=F32)
                per_down = len(tiles) * DOWN_ROWS // D_MODEL
                for i, tile in enumerate(tiles):
                    if not first and i % per_down == per_down // 2:
                        ms = slice(i // per_down * DOWN_ROWS, (i // per_down + 1) * DOWN_ROWS)
                        yt_ref[ms, cs] += jnp.dot(vt_ref[ms, hs],
                                                  act_ref[cur, :, cs].astype(BF16),
                                                  preferred_element_type=F32)
                    if gated:
                        gate_tile(half, *tile)

    @pl.when(step == 0)
    def _():
        yt_ref[...] = jnp.zeros_like(yt_ref)
        run(first=True, last=False)

    @pl.when(jnp.logical_and(step > 0, step < last_step))
    def _():
        run(first=False, last=False)

    @pl.when(step == last_step)
    def _():
        run(first=False, last=True)
        h = h_ref[...] + yt_ref[...].T
        inv = lax.rsqrt(jnp.mean(h * h, axis=-1, keepdims=True) + RMS_EPS)
        out_ref[...] = h * inv * fg_ref[...]


def _peer_dense(u, hnt, vt, cntr, e1r, rank2, e2, h, final_g, tt, n1):
    m = hnt.shape[1]
    et = n1 * PEER_NKEYS
    n_steps = PEER_NKEYS // (2 * n1)
    n_tiles = 2 * n_steps
    tok3 = lambda a: pl.BlockSpec((a.shape[0], a.shape[1], tt), lambda i, e: (0, 0, i))
    rows = lambda half: pl.BlockSpec(
        (PEER_HEADS, 1, n1, tt),
        lambda i, e: (0, jnp.clip(2 * e + half - 1, 0, n_tiles - 1), 0, i))
    kern = lambda *refs: _peer_dense_kernel(*refs, n1=n1)
    return pl.pallas_call(
        kern,
        grid=(m // tt, n_steps + 1),
        in_specs=[pl.BlockSpec((2 * et, D_MODEL), lambda i, e: (jnp.minimum(e, n_steps - 1), 0)),
                  pl.BlockSpec((D_MODEL, tt), lambda i, e: (0, i)),
                  pl.BlockSpec((D_MODEL, 2 * et), lambda i, e: (0, jnp.maximum(e - 1, 0))),
                  tok3(rank2), tok3(e2), rows(0), rows(1), rows(0), rows(1),
                  pl.BlockSpec((tt, D_MODEL), lambda i, e: (i, 0)),
                  pl.BlockSpec((1, D_MODEL), lambda i, e: (0, 0))],
        out_specs=pl.BlockSpec((tt, D_MODEL), lambda i, e: (i, 0)),
        out_shape=jax.ShapeDtypeStruct((m, D_MODEL), F32),
        scratch_shapes=[pltpu.VMEM((2, et, tt), F32), pltpu.VMEM((2, et, tt), F32),
                        pltpu.VMEM((D_MODEL, tt), F32)],
        compiler_params=_params(("parallel", "arbitrary"), DENSE_VMEM_LIMIT),
        name="peer_dense",
    )(u, hnt, vt, rank2, e2, cntr, cntr, e1r, e1r, h, final_g)


def _transpose_cast_kernel(x_ref, o_ref):
    o_ref[...] = x_ref[...].T.astype(o_ref.dtype)


def _transpose_cast(x, tr):
    rows, cols = x.shape
    return pl.pallas_call(
        _transpose_cast_kernel,
        grid=(rows // tr,),
        in_specs=[pl.BlockSpec((tr, cols), lambda i: (i, 0))],
        out_specs=pl.BlockSpec((cols, tr), lambda i: (0, i)),
        out_shape=jax.ShapeDtypeStruct((cols, rows), BF16),
        compiler_params=_params(("parallel",)),
        name="transpose_cast",
    )(x)


def _row(a):
    return a.reshape(1, -1).astype(F32)


def _pad_rows(w, start, total):
    return jnp.zeros((total, w.shape[1]), BF16).at[start:start + w.shape[0]].set(w.astype(BF16))


def kernel(x, norm1_g, w_in, rwkv_mu, rwkv_w0, rwkv_w2, rwkv_a0, rwkv_a2, rwkv_g2, rwkv_k_k, rwkv_k_a, rwkv_r_k, rwkv_ln_g, rwkv_ln_b, w_out, norm2_g, peer_wq, peer_sub_keys, peer_u, peer_v, final_g):
    batch, seq, d = x.shape
    tokens = batch * seq
    w = RWKV_WIDTH
    xt = x.reshape(tokens, d)

    w_in_p = jnp.pad(w_in[0], ((0, 0), (0, IN_COLS_PAD - IN_COLS))).astype(BF16)
    qkv, proj = _norm_matmul(xt, _row(norm1_g[0]), w_in_p, 3 * ATTN_WIDTH, tm=1024, tn=512)
    proj = proj.reshape(batch, seq, IN_COLS_PAD - 3 * ATTN_WIDTH)

    attn = _moba(qkv.reshape(batch, seq, 3 * ATTN_WIDTH), batch, seq, heads=8)

    mu = rwkv_mu[0]
    mu_l = jnp.pad(mu[3 * w:], (0, LORA_PAD - LORA_COLS))
    hid = lax.broadcasted_iota(jnp.int32, (w, w), 0) // RWKV_HEAD_DIM
    bd = (hid == hid.T).astype(BF16)
    head_gather = (hid[:, :LANES] == lax.broadcasted_iota(jnp.int32, (w, LANES), 1)).astype(BF16)
    prep_w = [_row(mu[:w]), _row(mu[w:2 * w]), _row(mu[2 * w:3 * w]), _row(mu_l),
              _row(rwkv_w0[0]), _pad_rows(rwkv_w2[0], 0, LORA_PAD),
              _row(rwkv_a0[0]), _pad_rows(rwkv_a2[0], DECAY_LORA, LORA_PAD),
              _pad_rows(rwkv_g2[0], DECAY_LORA + AAA_LORA, LORA_PAD),
              _row(rwkv_k_k[0]), _row(rwkv_k_a[0]), _row(rwkv_r_k[0]), head_gather, head_gather.T]
    r, lw, k2, v, kk, b, g, bonus = _rwkv_prep(proj, batch, seq, prep_w, tt=256)
    y = _rwkv_scan(r, lw, k2, v, kk, b, tb=256)

    flat = lambda a: a.reshape(tokens, a.shape[-1])
    h, hn, hnt = _out_proj(xt, flat(attn), flat(y), flat(bonus), flat(g), _row(rwkv_ln_g[0]),
                      _row(rwkv_ln_b[0]), bd, w_out[0].astype(BF16), _row(norm2_g[0]), tm=256)

    keys = peer_sub_keys[0].reshape(2 * PEER_HEADS, PEER_NKEYS, PEER_HALF).astype(BF16)
    st = _peer_q(hn, peer_wq[0].astype(BF16), keys, tm=512)
    cnt1, e1, rank2, e2 = _peer_topk(st, tl=512, n1=4)
    out = _peer_dense(peer_u[0].astype(BF16), hnt, _transpose_cast(peer_v[0], tr=512),
                      cnt1, e1, rank2, e2, h, final_g.reshape(1, d), tt=512, n1=4)
    return out.reshape(batch, seq, d)
```

```python
import itertools
import math

import jax
import jax.numpy as jnp
from jax import lax
from jax.experimental import pallas as pl
from jax.experimental.pallas import tpu as pltpu

F32 = jnp.float32
BF16 = jnp.bfloat16

D_MODEL = 2048
ATTN_HEADS = 8
ATTN_HEAD_DIM = 128
ATTN_WIDTH = ATTN_HEADS * ATTN_HEAD_DIM
MOBA_BLOCK = 256
MOBA_TOPK = 3

RWKV_HEAD_DIM = 64
RWKV_WIDTH = D_MODEL - ATTN_WIDTH
RWKV_HEADS = RWKV_WIDTH // RWKV_HEAD_DIM
DECAY_LORA = 96
AAA_LORA = 96
GATE_LORA = 256
LORA_COLS = DECAY_LORA + AAA_LORA + GATE_LORA
LORA_PAD = 512
RWKV_GN_EPS = RWKV_HEAD_DIM * 1e-5
IN_COLS = 3 * ATTN_WIDTH + 3 * RWKV_WIDTH + LORA_COLS
IN_COLS_PAD = 3 * ATTN_WIDTH + 3 * RWKV_WIDTH + LORA_PAD

PEER_HEADS = 8
PEER_NKEYS = 128
PEER_HALF = 128
PEER_TOPK = 16

RMS_EPS = 1e-6
NEG = -1e30

LANES = 128
SCAN_CHUNK = 64
SCAN_INV_BLOCK = 16
DOWN_ROWS = 2048
MXU_COLS = 256
DENSE_ROWS = 32
VMEM_LIMIT = 56 * 1024 * 1024
DENSE_VMEM_LIMIT = 58 * 1024 * 1024

_NT = (((1,), (1,)), ((), ()))
_TN = (((0,), (0,)), ((), ()))


def _params(sem, vmem_limit=VMEM_LIMIT):
    return pltpu.CompilerParams(dimension_semantics=sem, vmem_limit_bytes=vmem_limit)


def _dot(a, b):
    return jnp.dot(a.astype(BF16), b.astype(BF16), preferred_element_type=F32)


def _dot_nt(a, b):
    return lax.dot_general(a.astype(BF16), b.astype(BF16), _NT, preferred_element_type=F32)


def _dot_tn(a, b):
    return lax.dot_general(a.astype(BF16), b.astype(BF16), _TN, preferred_element_type=F32)


def _split3(x):
    hi = x.astype(BF16)
    r1 = x - hi.astype(F32)
    mid = r1.astype(BF16)
    lo = (r1 - mid.astype(F32)).astype(BF16)
    return hi, mid, lo


def _dot_2piece_rhs(x, ones_bf16):
    hi = x.astype(BF16)
    lo = (x - hi.astype(F32)).astype(BF16)
    d = lambda p: jnp.dot(p, ones_bf16, preferred_element_type=F32)
    return d(hi) + d(lo)


def _dot_3piece_lhs(ones_bf16, x):
    hi, mid, lo = _split3(x)
    d = lambda p: jnp.dot(ones_bf16, p, preferred_element_type=F32)
    return d(hi) + d(mid) + d(lo)


def _norm_matmul_kernel(x_ref, g_ref, w_ref, attn_ref, rwkv_ref, xn_ref, *, attn_tiles):
    j = pl.program_id(1)

    @pl.when(j == 0)
    def _():
        x = x_ref[...]
        inv = lax.rsqrt(jnp.mean(x * x, axis=-1, keepdims=True) + RMS_EPS)
        xn_ref[...] = (x * inv * g_ref[...]).astype(BF16)

    acc = jnp.dot(xn_ref[...], w_ref[...], preferred_element_type=F32)

    @pl.when(j < attn_tiles)
    def _():
        attn_ref[...] = acc.astype(BF16)

    @pl.when(j >= attn_tiles)
    def _():
        rwkv_ref[...] = acc


def _norm_matmul(x, g, w, n_attn, tm, tn):
    m, k = x.shape
    n = w.shape[1]
    attn_tiles = n_attn // tn
    kern = lambda *refs: _norm_matmul_kernel(*refs, attn_tiles=attn_tiles)
    return pl.pallas_call(
        kern,
        grid=(m // tm, n // tn),
        in_specs=[pl.BlockSpec((tm, k), lambda i, j: (i, 0)),
                  pl.BlockSpec((1, k), lambda i, j: (0, 0)),
                  pl.BlockSpec((k, tn), lambda i, j: (0, j))],
        out_specs=[pl.BlockSpec((tm, tn), lambda i, j: (i, jnp.minimum(j, attn_tiles - 1))),
                   pl.BlockSpec((tm, tn), lambda i, j: (i, jnp.maximum(j - attn_tiles, 0)))],
        out_shape=[jax.ShapeDtypeStruct((m, n_attn), BF16),
                   jax.ShapeDtypeStruct((m, n - n_attn), F32)],
        scratch_shapes=[pltpu.VMEM((tm, k), BF16)],
        compiler_params=_params(("parallel", "arbitrary")),
        name="in_proj",
    )(x, g, w)


def _moba_kernel(q_ref, k_ref, v_ref, o_ref, kmean_ref, vt_ref, *, nb, heads):
    blk_len = MOBA_BLOCK
    hd = ATTN_HEAD_DIM
    hg = pl.program_id(1)
    qi = pl.program_id(2)
    hrange = range(heads)

    @pl.when(qi == 0)
    def _():
        for hh in hrange:
            hl = slice(hh * hd, (hh + 1) * hd)
            for j in range(nb):
                rows = slice(j * blk_len, (j + 1) * blk_len)
                kj = k_ref[0, rows, hl].astype(F32)
                kmean_ref[hh, j:j + 1, :] = jnp.mean(kj, axis=0, keepdims=True)
                vt_ref[hh, :, rows] = v_ref[0, rows, hl].astype(F32).T.astype(BF16)

    log2e = math.log2(math.e)
    scale2 = (hd ** -0.5) * log2e
    kq = (lax.broadcasted_iota(jnp.int32, (blk_len, blk_len), 1)
          - lax.broadcasted_iota(jnp.int32, (blk_len, blk_len), 0))
    kqf = kq.astype(F32)
    blk = lax.broadcasted_iota(jnp.int32, (nb, blk_len), 0)
    past = blk < qi

    def select(hh):
        q = q_ref[0, :, hh * hd:(hh + 1) * hd].astype(F32)
        gate = lax.dot_general(kmean_ref[hh], q, _NT, precision=lax.Precision.HIGHEST,
                               preferred_element_type=F32)
        g = jnp.where(past, gate, NEG)
        selw = jnp.zeros(gate.shape, F32)
        for _ in range(min(MOBA_TOPK, nb)):
            m = jnp.max(g, axis=0, keepdims=True)
            first = jnp.min(jnp.where(g == m, blk, nb), axis=0, keepdims=True)
            pick = blk == first
            selw = jnp.where(pick, 1.0, selw)
            g = jnp.where(pick, -jnp.inf, g)
        return jnp.where(past, selw, 0.0), (q * scale2).astype(BF16)

    sel_q = [select(hh) for hh in hrange]
    selw = [x[0] for x in sel_q]
    qb = [x[1] for x in sel_q]
    slope2 = [jnp.exp2(jnp.zeros((1, 1), F32) - (hg * heads + hh + 1).astype(F32)) * log2e
              for hh in hrange]
    bias0 = [slope2[hh] * kqf for hh in hrange]

    def scores(j):
        r0 = pl.multiple_of(j * blk_len, blk_len)
        return [lax.dot_general(k_ref[0, pl.ds(r0, blk_len), hh * hd:(hh + 1) * hd], qb[hh], _NT,
                                preferred_element_type=F32) - bias0[hh] for hh in hrange]

    def weighted_values(j, p):
        r0 = pl.multiple_of(j * blk_len, blk_len)
        return [jnp.dot(vt_ref[hh, :, pl.ds(r0, blk_len)], p[hh].astype(BF16),
                        preferred_element_type=F32) for hh in hrange]

    s = [jnp.where(kq >= 0, x, NEG) for x in scores(qi)]
    m = [jnp.max(x, axis=0, keepdims=True) for x in s]
    p = [jnp.exp2(s[hh] - m[hh]) for hh in hrange]
    l = [jnp.sum(x, axis=0, keepdims=True) for x in p]
    acc = weighted_values(qi, p)

    def body(j, carry):
        m, l, acc = carry
        off = ((qi - j) * blk_len).astype(F32)
        s = scores(j)
        selj = [jnp.max(jnp.where(blk == j, selw[hh], 0.0), axis=0, keepdims=True) > 0.0
                for hh in hrange]
        shift = [slope2[hh] * off for hh in hrange]
        top = [jnp.max(s[hh], axis=0, keepdims=True) - shift[hh] for hh in hrange]
        m_new = [jnp.maximum(m[hh], jnp.where(selj[hh], top[hh], NEG)) for hh in hrange]
        alpha = [jnp.exp2(m[hh] - m_new[hh]) for hh in hrange]
        p = [jnp.exp2(s[hh] - jnp.where(selj[hh], m_new[hh] + shift[hh], jnp.inf)) for hh in hrange]
        l = [alpha[hh] * l[hh] + jnp.sum(p[hh], axis=0, keepdims=True) for hh in hrange]
        pv = weighted_values(j, p)
        acc = [alpha[hh] * acc[hh] + pv[hh] for hh in hrange]
        return m_new, l, acc

    m, l, acc = lax.fori_loop(0, qi, body, (m, l, acc))
    for hh in hrange:
        o_ref[0, :, hh * hd:(hh + 1) * hd] = (acc[hh] / l[hh]).T


def _moba(qkv, batch, seq, heads):
    nb = seq // MOBA_BLOCK
    hd = ATTN_HEAD_DIM
    hw = heads * hd
    groups = ATTN_HEADS // heads
    kern = lambda *refs: _moba_kernel(*refs, nb=nb, heads=heads)
    return pl.pallas_call(
        kern,
        grid=(batch, groups, nb),
        in_specs=[pl.BlockSpec((1, MOBA_BLOCK, hw), lambda b, h, i: (b, i, h)),
                  pl.BlockSpec((1, seq, hw), lambda b, h, i: (b, 0, groups + h)),
                  pl.BlockSpec((1, seq, hw), lambda b, h, i: (b, 0, 2 * groups + h))],
        out_specs=pl.BlockSpec((1, MOBA_BLOCK, hw), lambda b, h, i: (b, i, h)),
        out_shape=jax.ShapeDtypeStruct((batch, seq, ATTN_WIDTH), F32),
        scratch_shapes=[pltpu.VMEM((heads, nb, hd), F32),
                        pltpu.VMEM((heads, hd, seq), BF16)],
        compiler_params=_params(("parallel", "parallel", "arbitrary")),
        name="moba",
    )(qkv, qkv, qkv)


def _rwkv_prep_kernel(r_ref, k_ref, v_ref, l_ref, rp_ref, kp_ref, vp_ref, lp_ref,
                      mu_r, mu_k, mu_v, mu_l, w0, w2p, a0, a2p, g2p, kk_w, ka_w, rk_w, bd_ref,
                      r_o, lw_o, k_o, v_o, kk_o, b_o, g_o, bonus_o):
    first = pl.program_id(1) == 0

    def shift(cur_ref, prev_ref, mu_ref):
        cur = cur_ref[0]
        prev_last = jnp.where(first, 0.0, prev_ref[0, 7:8, :])
        rows = lax.broadcasted_iota(jnp.int32, cur.shape, 0)
        prev = jnp.where(rows == 0, prev_last, pltpu.roll(cur, 1, axis=0))
        return cur + (prev - cur) * mu_ref[...]

    r = shift(r_ref, rp_ref, mu_r)
    k = shift(k_ref, kp_ref, mu_k)
    v = shift(v_ref, vp_ref, mu_v)
    lo = shift(l_ref, lp_ref, mu_l)

    bd = bd_ref[...]
    z = -(w0[...] + _dot(jnp.tanh(lo), w2p[...]))
    softplus = jnp.maximum(z, 0.0) + jnp.log1p(jnp.exp(-jnp.abs(z)))
    lw = -jnp.exp(-softplus - 0.5)
    a = jax.nn.sigmoid(a0[...] + _dot(lo, a2p[...]))
    g = _dot(jax.nn.sigmoid(lo), g2p[...])
    kk = k * kk_w[...]
    kk = kk / jnp.maximum(jnp.sqrt(_dot_2piece_rhs(kk * kk, bd)), 1e-12)
    k2 = k * (1.0 + (a - 1.0) * ka_w[...])
    r_o[0] = r
    lw_o[0] = lw
    k_o[0] = k2
    v_o[0] = v
    kk_o[0] = kk
    b_o[0] = kk * a
    g_o[0] = g
    bonus_o[0] = _dot_2piece_rhs(r * k2 * rk_w[...], bd) * v


def _rwkv_prep(proj, batch, seq, weights, tt):
    w = RWKV_WIDTH
    c0 = 0
    lc = 3 * w // LORA_PAD
    pb = tt // 8
    cur = lambda width, col: pl.BlockSpec((1, tt, width), lambda b, i: (b, i, col))
    prev = lambda width, col: pl.BlockSpec(
        (1, 8, width), lambda b, i: (b, jnp.maximum(i * pb - 1, 0), col))
    full = lambda a: pl.BlockSpec(a.shape, lambda b, i: (0,) * a.ndim)
    out = pl.BlockSpec((1, tt, w), lambda b, i: (b, i, 0))
    return pl.pallas_call(
        _rwkv_prep_kernel,
        grid=(batch, seq // tt),
        in_specs=[cur(w, c0), cur(w, c0 + 1), cur(w, c0 + 2), cur(LORA_PAD, lc),
                  prev(w, c0), prev(w, c0 + 1), prev(w, c0 + 2), prev(LORA_PAD, lc)]
                 + [full(a) for a in weights],
        out_specs=[out] * 8,
        out_shape=[jax.ShapeDtypeStruct((batch, seq, w), F32)] * 8,
        compiler_params=_params(("parallel", "arbitrary")),
        name="rwkv_prep",
    )(proj, proj, proj, proj, proj, proj, proj, proj, *weights)


def _scan_kernel(r_ref, lw_ref, k_ref, v_ref, kk_ref, b_ref, y_ref, s_ref, *, tb, pairs, batch):
    c = SCAN_CHUNK
    c2 = 2 * c

    @pl.when(pl.program_id(0) == 0)
    def _():
        s_ref[...] = jnp.zeros_like(s_ref)

    head0 = lax.broadcasted_iota(jnp.int32, (c, LANES), 1) < RWKV_HEAD_DIM
    row = lax.broadcasted_iota(jnp.int32, (c2, c2), 0)
    col = lax.broadcasted_iota(jnp.int32, (c2, c2), 1)
    same_head = (row // c) == (col // c)
    strict = jnp.logical_and(same_head, col < row)
    incl = jnp.logical_and(same_head, col <= row)
    inv_blk = (row // SCAN_INV_BLOCK) == (col // SCAN_INV_BLOCK)
    eye = jnp.where(row == col, 1.0, 0.0).astype(F32)
    tri = jnp.where(lax.broadcasted_iota(jnp.int32, (c, c), 0)
                    >= lax.broadcasted_iota(jnp.int32, (c, c), 1), 1.0, 0.0).astype(BF16)

    def stack(x):
        return jnp.concatenate([jnp.where(head0, x, 0.0), jnp.where(head0, 0.0, x)], axis=0)

    def pair_chunk(t0, bi, p):
        lanes = slice(p * LANES, (p + 1) * LANES)
        rows = pl.ds(t0, c)
        lw = lw_ref[bi, rows, lanes]
        cum = _dot_3piece_lhs(tri, lw)
        yield
        r = r_ref[bi, rows, lanes]
        k = k_ref[bi, rows, lanes]
        kk = kk_ref[bi, rows, lanes]
        b = b_ref[bi, rows, lanes]
        cum_end = cum[c - 1:c, :]
        e_neg = jnp.exp(-cum)
        e_end = jnp.exp(cum_end - cum)
        a_s = stack(-kk * jnp.exp(cum - lw))
        r_s = stack(r * jnp.exp(cum))
        b_s = stack(b * e_neg)
        k_s = stack(k * e_neg)
        bend_s = stack(b * e_end)
        kend_s = stack(k * e_end)
        v_s = stack(v_ref[bi, rows, lanes])
        gram = _dot_nt(jnp.concatenate([a_s, r_s], axis=0),
                       jnp.concatenate([b_s, k_s], axis=0))
        kv = _dot_tn(v_s, kend_s)
        yield
        l_ab = jnp.where(strict, gram[:c2, :c2], 0.0)
        l_ak = jnp.where(strict, gram[:c2, c2:], 0.0)
        l_rb = jnp.where(incl, gram[c2:, :c2], 0.0)
        l_rk = jnp.where(incl, gram[c2:, c2:], 0.0)

        x = jnp.where(inv_blk, l_ab, 0.0)
        off = jnp.where(inv_blk, 0.0, l_ab)
        dinv = eye + x
        w1 = _dot(l_ak, v_s)
        y1 = _dot(l_rk, v_s)
        for _ in range(int(math.log2(SCAN_INV_BLOCK)) - 1):
            x = _dot(x, x)
            yield
            dinv = dinv + _dot(dinv, x)
            yield
        f = _dot(dinv, off)
        yield
        f2 = _dot(f, f)
        yield
        gm = eye + f
        gm = gm + _dot(gm, f2)
        yield
        t_inv = _dot(gm, dinv)
        yield
        ta = _dot(t_inv, jnp.concatenate([a_s, w1], axis=1))
        yield
        a_hat = ta[:, :LANES]
        v_hat = ta[:, LANES:]
        state = s_ref[bi * pairs + p]
        m1 = _dot_nt(jnp.concatenate([a_hat, r_s], axis=0), state)
        yield
        u = m1[:c2] + v_hat
        y = m1[c2:] + y1 + _dot(l_rb, u)
        s_ref[bi * pairs + p] = state * jnp.exp(cum_end) + kv + _dot_tn(u, bend_s)
        y_ref[bi, rows, lanes] = y[:c] + y[c:]

    def chunk(ci, carry):
        t0 = pl.multiple_of(ci * c, c)
        chains = [pair_chunk(t0, bi, p) for bi in range(batch) for p in range(pairs)]
        for _ in itertools.zip_longest(*chains):
            pass
        return carry

    lax.fori_loop(0, tb // c, chunk, 0)


def _rwkv_scan(r, lw, k, v, kk, b, tb):
    batch, seq, w = r.shape
    pairs = w // LANES
    spec = pl.BlockSpec((batch, tb, w), lambda t: (0, t, 0))
    kern = lambda *refs: _scan_kernel(*refs, tb=tb, pairs=pairs, batch=batch)
    return pl.pallas_call(
        kern,
        grid=(seq // tb,),
        in_specs=[spec] * 6,
        out_specs=spec,
        out_shape=jax.ShapeDtypeStruct((batch, seq, w), F32),
        scratch_shapes=[pltpu.VMEM((batch * pairs, LANES, LANES), F32)],
        compiler_params=_params(("arbitrary",)),
        name="rwkv_scan",
    )(r, lw, k, v, kk, b)


def _out_proj_kernel(x_ref, attn_ref, y_ref, bonus_ref, g_ref, lng, lnb, bd_ref, w_ref, n2g,
                     h_ref, hn_ref, hnt_ref):
    bd = bd_ref[...]
    y = y_ref[...]
    inv_n = 1.0 / RWKV_HEAD_DIM
    d = y - _dot(y, bd) * inv_n
    var = _dot(d * d, bd) * inv_n
    yn = d * lax.rsqrt(var + RWKV_GN_EPS) * lng[...] + lnb[...]
    rw = (yn + bonus_ref[...]) * g_ref[...]
    h = (x_ref[...] + _dot(attn_ref[...], w_ref[:ATTN_WIDTH, :])
         + _dot(rw, w_ref[ATTN_WIDTH:, :]))
    h_ref[...] = h
    inv = lax.rsqrt(jnp.mean(h * h, axis=-1, keepdims=True) + RMS_EPS)
    hn = h * inv * n2g[...]
    hn_ref[...] = hn.astype(BF16)
    hnt_ref[...] = hn.T.astype(BF16)


def _out_proj(x, attn, y, bonus, g, lng, lnb, bd, w_out, n2g, tm):
    m = x.shape[0]
    row = lambda width: pl.BlockSpec((tm, width), lambda i: (i, 0))
    full = lambda a: pl.BlockSpec(a.shape, lambda i: (0,) * a.ndim)
    return pl.pallas_call(
        _out_proj_kernel,
        grid=(m // tm,),
        in_specs=[row(D_MODEL), row(ATTN_WIDTH), row(RWKV_WIDTH), row(RWKV_WIDTH), row(RWKV_WIDTH),
                  full(lng), full(lnb), full(bd), full(w_out), full(n2g)],
        out_specs=[row(D_MODEL), row(D_MODEL), pl.BlockSpec((D_MODEL, tm), lambda i: (0, i))],
        out_shape=[jax.ShapeDtypeStruct((m, D_MODEL), F32),
                   jax.ShapeDtypeStruct((m, D_MODEL), BF16),
                   jax.ShapeDtypeStruct((D_MODEL, m), BF16)],
        compiler_params=_params(("parallel",)),
        name="out_proj",
    )(x, attn, y, bonus, g, lng, lnb, bd, w_out, n2g)


def _peer_q_kernel(hn_ref, wq_ref, keys_ref, st_ref):
    q = jnp.dot(hn_ref[...], wq_ref[...], preferred_element_type=F32)
    for hp in range(2 * PEER_HEADS):
        qs = q[:, hp * PEER_HALF:(hp + 1) * PEER_HALF]
        st_ref[hp] = _dot_nt(keys_ref[hp], qs)


def _peer_q(hn, wq, keys, tm):
    m = hn.shape[0]
    return pl.pallas_call(
        _peer_q_kernel,
        grid=(m // tm,),
        in_specs=[pl.BlockSpec((tm, D_MODEL), lambda i: (i, 0)),
                  pl.BlockSpec(wq.shape, lambda i: (0, 0)),
                  pl.BlockSpec(keys.shape, lambda i: (0, 0, 0))],
        out_specs=pl.BlockSpec((2 * PEER_HEADS, PEER_NKEYS, tm), lambda i: (0, 0, i)),
        out_shape=jax.ShapeDtypeStruct((2 * PEER_HEADS, PEER_NKEYS, m), F32),
        compiler_params=_params(("parallel",)),
        name="peer_q",
    )(hn, wq, keys)


def _top_desc(x, k, exact_ties, want_rank=True):
    n = x.shape[0]
    rows = lax.broadcasted_iota(jnp.int32, x.shape, 0)
    rank = jnp.full(x.shape, float(k), F32) if want_rank else None
    outs = []
    for i in range(k):
        m = jnp.max(x, axis=0, keepdims=True)
        outs.append(m)
        pick = x == m
        if exact_ties:
            pick = rows == jnp.min(jnp.where(pick, rows, n), axis=0, keepdims=True)
        x = jnp.where(pick, -jnp.inf, x)
        if want_rank:
            rank = jnp.where(pick, float(i), rank)
    gone = jnp.sum(jnp.where(x == -jnp.inf, 1.0, 0.0), axis=0, keepdims=True)
    return jnp.concatenate(outs, axis=0), rank, gone


def _peer_topk_kernel(s_ref, cnt1_ref, e1_ref, rank2_ref, e2_ref, *, n1):
    k = PEER_TOPK
    widths = [k // (a + 1) for a in range(k)]
    pad = -sum(widths) % 8

    def run(exact_ties):
        s1 = s_ref[0]
        s2 = s_ref[1]
        sv1, rank1, gone1 = _top_desc(s1, k, exact_ties, want_rank=exact_ties)
        sv2, rank2, gone2 = _top_desc(s2, k, exact_ties)
        cand = jnp.concatenate([sv1[a:a + 1] + sv2[:widths[a]] for a in range(k)]
                               + [jnp.full((pad, s1.shape[1]), -jnp.inf, F32)], axis=0)
        best, pos, gone3 = _top_desc(cand, k, exact_ties)
        chosen = jnp.where(pos < float(k), 1.0, 0.0)
        z = jnp.sum(jnp.exp(best - best[0:1]), axis=0, keepdims=True)
        cnt1 = jnp.zeros(s1.shape, F32)
        row = 0
        for a in range(k):
            cnt_a = jnp.sum(chosen[row:row + widths[a]], axis=0, keepdims=True)
            at_a = rank1 == float(a) if exact_ties else s1 == sv1[a:a + 1]
            cnt1 = jnp.where(at_a, cnt_a, cnt1)
            row += widths[a]
        e1 = jnp.exp(s1 - sv1[0:1]) / z
        for t in range(PEER_NKEYS // n1):
            cnt1_ref[0, t] = cnt1[t * n1:(t + 1) * n1]
            e1_ref[0, t] = e1[t * n1:(t + 1) * n1]
        rank2_ref[0] = rank2
        e2_ref[0] = jnp.exp(s2 - sv2[0:1])
        return jnp.max(jnp.maximum(jnp.maximum(gone1, gone2), gone3 - float(pad)))

    most_gone = run(exact_ties=False)

    @pl.when(most_gone > float(k))
    def _():
        run(exact_ties=True)


def _peer_topk(st, tl, n1):
    m = st.shape[2]
    n_tiles = PEER_NKEYS // n1
    spec = pl.BlockSpec((1, PEER_NKEYS, tl), lambda h, i: (h, 0, i))
    shape = jax.ShapeDtypeStruct((PEER_HEADS, PEER_NKEYS, m), F32)
    tile_spec = pl.BlockSpec((1, n_tiles, n1, tl), lambda h, i: (h, 0, 0, i))
    tile_shape = jax.ShapeDtypeStruct((PEER_HEADS, n_tiles, n1, m), F32)
    kern = lambda *refs: _peer_topk_kernel(*refs, n1=n1)
    return pl.pallas_call(
        kern,
        grid=(PEER_HEADS, m // tl),
        in_specs=[pl.BlockSpec((2, PEER_NKEYS, tl), lambda h, i: (h, 0, i))],
        out_specs=[tile_spec, tile_spec, spec, spec],
        out_shape=[tile_shape, tile_shape, shape, shape],
        compiler_params=_params(("parallel", "parallel")),
        name="peer_topk",
    )(st)


def _peer_dense_kernel(u_ref, hnt_ref, vt_ref, rank2_ref, e2_ref, cnta_ref, cntb_ref,
                       e1a_ref, e1b_ref, h_ref, fg_ref, out_ref, act_ref, hpre_ref, yt_ref, *, n1):
    step = pl.program_id(1)
    last_step = pl.num_programs(1) - 1
    et = hpre_ref.shape[1]
    row_refs = ((cnta_ref, e1a_ref), (cntb_ref, e1b_ref))

    def gate_tile(half, jt, ib):
        prev = 1 - half
        cnt_ref, e1r_ref = row_refs[half]
        ls = slice(jt * LANES, (jt + 1) * LANES)
        rs = slice(ib * DENSE_ROWS, (ib + 1) * DENSE_ROWS)
        gates = [jnp.zeros((DENSE_ROWS, LANES), F32) for _ in range(n1)]
        for h in range(PEER_HEADS):
            r2 = rank2_ref[h, rs, ls]
            e2 = e2_ref[h, rs, ls]
            for a in range(n1):
                hit = r2 < cnt_ref[h, 0, a:a + 1, ls]
                gates[a] = gates[a] + jnp.where(hit, e2, 0.0) * e1r_ref[h, 0, a:a + 1, ls]
        for a in range(n1):
            ers = slice(a * PEER_NKEYS + ib * DENSE_ROWS, a * PEER_NKEYS + (ib + 1) * DENSE_ROWS)
            x = hpre_ref[prev, ers, ls]
            gelu = 0.5 * x * (1.0 + lax.erf(x * (0.5 ** 0.5)))
            act_ref[prev, ers, ls] = gelu * gates[a]

    tt = hpre_ref.shape[2]
    n_ib = PEER_NKEYS // DENSE_ROWS

    def run(first, last):
        for half in range(2):
            cur = half
            hs = slice(half * et, (half + 1) * et)
            gated = not (first and half == 0) and not (last and half == 1)
            for piece in range(tt // MXU_COLS):
                cs = slice(piece * MXU_COLS, (piece + 1) * MXU_COLS)
                tiles = [(jt, ib)
                         for jt in range(piece * MXU_COLS // LANES, (piece + 1) * MXU_COLS // LANES)
                         for ib in range(n_ib)]
                if not last:
                    hpre_ref[cur, :, cs] = jnp.dot(u_ref[hs, :], hnt_ref[:, cs],
                                                   preferred_element_type=F32)
                per_down = len(tiles) * DOWN_ROWS // D_MODEL
                for i, tile in enumerate(tiles):
                    if not first and i % per_down == per_down // 2:
                        ms = slice(i // per_down * DOWN_ROWS, (i // per_down + 1) * DOWN_ROWS)
                        yt_ref[ms, cs] += jnp.dot(vt_ref[ms, hs],
                                                  act_ref[cur, :, cs].astype(BF16),
                                                  preferred_element_type=F32)
                    if gated:
                        gate_tile(half, *tile)

    @pl.when(step == 0)
    def _():
        yt_ref[...] = jnp.zeros_like(yt_ref)
        run(first=True, last=False)

    @pl.when(jnp.logical_and(step > 0, step < last_step))
    def _():
        run(first=False, last=False)

    @pl.when(step == last_step)
    def _():
        run(first=False, last=True)
        h = h_ref[...] + yt_ref[...].T
        inv = lax.rsqrt(jnp.mean(h * h, axis=-1, keepdims=True) + RMS_EPS)
        out_ref[...] = h * inv * fg_ref[...]


def _peer_dense(u, hnt, vt, cntr, e1r, rank2, e2, h, final_g, tt, n1):
    m = hnt.shape[1]
    et = n1 * PEER_NKEYS
    n_steps = PEER_NKEYS // (2 * n1)
    n_tiles = 2 * n_steps
    tok3 = lambda a: pl.BlockSpec((a.shape[0], a.shape[1], tt), lambda i, e: (0, 0, i))
    rows = lambda half: pl.BlockSpec(
        (PEER_HEADS, 1, n1, tt),
        lambda i, e: (0, jnp.clip(2 * e + half - 1, 0, n_tiles - 1), 0, i))
    kern = lambda *refs: _peer_dense_kernel(*refs, n1=n1)
    return pl.pallas_call(
        kern,
        grid=(m // tt, n_steps + 1),
        in_specs=[pl.BlockSpec((2 * et, D_MODEL), lambda i, e: (jnp.minimum(e, n_steps - 1), 0)),
                  pl.BlockSpec((D_MODEL, tt), lambda i, e: (0, i)),
                  pl.BlockSpec((D_MODEL, 2 * et), lambda i, e: (0, jnp.maximum(e - 1, 0))),
                  tok3(rank2), tok3(e2), rows(0), rows(1), rows(0), rows(1),
                  pl.BlockSpec((tt, D_MODEL), lambda i, e: (i, 0)),
                  pl.BlockSpec((1, D_MODEL), lambda i, e: (0, 0))],
        out_specs=pl.BlockSpec((tt, D_MODEL), lambda i, e: (i, 0)),
        out_shape=jax.ShapeDtypeStruct((m, D_MODEL), F32),
        scratch_shapes=[pltpu.VMEM((2, et, tt), F32), pltpu.VMEM((2, et, tt), F32),
                        pltpu.VMEM((D_MODEL, tt), F32)],
        compiler_params=_params(("parallel", "arbitrary"), DENSE_VMEM_LIMIT),
        name="peer_dense",
    )(u, hnt, vt, rank2, e2, cntr, cntr, e1r, e1r, h, final_g)


def _transpose_cast_kernel(x_ref, o_ref):
    o_ref[...] = x_ref[...].T.astype(o_ref.dtype)


def _transpose_cast(x, tr):
    rows, cols = x.shape
    return pl.pallas_call(
        _transpose_cast_kernel,
        grid=(rows // tr,),
        in_specs=[pl.BlockSpec((tr, cols), lambda i: (i, 0))],
        out_specs=pl.BlockSpec((cols, tr), lambda i: (0, i)),
        out_shape=jax.ShapeDtypeStruct((cols, rows), BF16),
        compiler_params=_params(("parallel",)),
        name="transpose_cast",
    )(x)


def _row(a):
    return a.reshape(1, -1).astype(F32)


def _pad_rows(w, start, total):
    return jnp.zeros((total, w.shape[1]), BF16).at[start:start + w.shape[0]].set(w.astype(BF16))


def kernel(x, norm1_g, w_in, rwkv_mu, rwkv_w0, rwkv_w2, rwkv_a0, rwkv_a2, rwkv_g2, rwkv_k_k, rwkv_k_a, rwkv_r_k, rwkv_ln_g, rwkv_ln_b, w_out, norm2_g, peer_wq, peer_sub_keys, peer_u, peer_v, final_g):
    batch, seq, d = x.shape
    tokens = batch * seq
    w = RWKV_WIDTH
    xt = x.reshape(tokens, d)

    w_in_p = jnp.pad(w_in[0], ((0, 0), (0, IN_COLS_PAD - IN_COLS))).astype(BF16)
    qkv, proj = _norm_matmul(xt, _row(norm1_g[0]), w_in_p, 3 * ATTN_WIDTH, tm=1024, tn=512)
    proj = proj.reshape(batch, seq, IN_COLS_PAD - 3 * ATTN_WIDTH)

    attn = _moba(qkv.reshape(batch, seq, 3 * ATTN_WIDTH), batch, seq, heads=8)

    mu = rwkv_mu[0]
    mu_l = jnp.pad(mu[3 * w:], (0, LORA_PAD - LORA_COLS))
    hid = lax.broadcasted_iota(jnp.int32, (w, w), 0) // RWKV_HEAD_DIM
    bd = (hid == hid.T).astype(BF16)
    prep_w = [_row(mu[:w]), _row(mu[w:2 * w]), _row(mu[2 * w:3 * w]), _row(mu_l),
              _row(rwkv_w0[0]), _pad_rows(rwkv_w2[0], 0, LORA_PAD),
              _row(rwkv_a0[0]), _pad_rows(rwkv_a2[0], DECAY_LORA, LORA_PAD),
              _pad_rows(rwkv_g2[0], DECAY_LORA + AAA_LORA, LORA_PAD),
              _row(rwkv_k_k[0]), _row(rwkv_k_a[0]), _row(rwkv_r_k[0]), bd]
    r, lw, k2, v, kk, b, g, bonus = _rwkv_prep(proj, batch, seq, prep_w, tt=256)
    y = _rwkv_scan(r, lw, k2, v, kk, b, tb=256)

    flat = lambda a: a.reshape(tokens, a.shape[-1])
    h, hn, hnt = _out_proj(xt, flat(attn), flat(y), flat(bonus), flat(g), _row(rwkv_ln_g[0]),
                      _row(rwkv_ln_b[0]), bd, w_out[0].astype(BF16), _row(norm2_g[0]), tm=256)

    keys = peer_sub_keys[0].reshape(2 * PEER_HEADS, PEER_NKEYS, PEER_HALF).astype(BF16)
    st = _peer_q(hn, peer_wq[0].astype(BF16), keys, tm=512)
    cnt1, e1, rank2, e2 = _peer_topk(st, tl=512, n1=4)
    out = _peer_dense(peer_u[0].astype(BF16), hnt, _transpose_cast(peer_v[0], tr=512),
                      cnt1, e1, rank2, e2, h, final_g.reshape(1, d), tt=512, n1=4)
    return out.reshape(batch, seq, d)
```
